```python
import functools
import jax, jax.numpy as jnp
from jax import lax
import numpy as np

D_MODEL = 1024
BATCH = 8
SEQ = 2048
DEPTH = 2
DEC_BATCH = 128
DEC_SEQ = 1
PAST_LEN = 2048
PAGE_SIZE = 128

BRANCH_WIDTH = D_MODEL // 2
N_BRANCH = 3
GLA_HEADS = 4
GLA_DV = BRANCH_WIDTH // GLA_HEADS
GLA_DK = GLA_DV // 2
GLA_GATE_RANK = 16
GLA_GATE_NORMALIZER = 16.0
GLA_CHUNK = 64
GLA_NORM_EPS = 1e-5
MOBA_HEAD_DIM = 64
MOBA_HEADS = BRANCH_WIDTH // MOBA_HEAD_DIM
MOBA_BLOCK = 256
MOBA_TOPK = 3
MOBA_QBLOCK = 128
ROPE_THETA = 10000.0
RWKV_HEAD = 64
RWKV_HEADS = BRANCH_WIDTH // RWKV_HEAD
RWKV_DECAY_RANK = 64
RWKV_A_RANK = 64
RWKV_GATE_RANK = 128
RWKV_DECAY_SCALE = 0.606531
RWKV_GN_EPS = 64e-5
RWKV_PROJ = 3 * BRANCH_WIDTH + RWKV_DECAY_RANK + RWKV_A_RANK + RWKV_GATE_RANK
GLA_COLS = (GLA_HEADS * GLA_DK, GLA_HEADS * GLA_DK, BRANCH_WIDTH, GLA_GATE_RANK, BRANCH_WIDTH)
MOBA_COLS = (BRANCH_WIDTH, BRANCH_WIDTH, BRANCH_WIDTH)
IN_SPLITS = GLA_COLS + MOBA_COLS + (RWKV_PROJ, N_BRANCH * D_MODEL)
IN_PROJ = sum(IN_SPLITS)
RWKV_SPLITS = (BRANCH_WIDTH, BRANCH_WIDTH, BRANCH_WIDTH, RWKV_DECAY_RANK, RWKV_A_RANK, RWKV_GATE_RANK)
D_FF = 4 * D_MODEL
ALPHA = (2 * DEPTH) ** 0.25
BETA = (8 * DEPTH) ** -0.25
LN_EPS = 1e-5

kernel_name = 'hybrid_gla_moba_rwkv7_deepnorm_step'


def split_cols(t, sizes):
    out, o = [], 0
    for s in sizes:
        out.append(t[..., o:o + s])
        o += s
    return out


def layer_norm(x, g, b):
    xf = x.astype(jnp.float32)
    mu = xf.mean(-1, keepdims=True)
    var = jnp.square(xf - mu).mean(-1, keepdims=True)
    return ((xf - mu) * lax.rsqrt(var + LN_EPS) * g + b).astype(x.dtype)


def rope(x, pos):
    half = x.shape[-1] // 2
    inv = ROPE_THETA ** (-jnp.arange(half, dtype=jnp.float32) / half)
    ang = pos.astype(jnp.float32)[:, None] * inv[None, :]
    cos = jnp.cos(ang)[:, None, :]
    sin = jnp.sin(ang)[:, None, :]
    xf = x.astype(jnp.float32)
    x1, x2 = xf[..., :half], xf[..., half:]
    return jnp.concatenate([x1 * cos - x2 * sin, x1 * sin + x2 * cos], -1).astype(x.dtype)


def gla_chunked(q, k, v, log_a, S0):
    f32 = jnp.float32
    B, L, H, DK = q.shape
    DV = v.shape[-1]
    C = GLA_CHUNK
    n = -(-L // C)
    padw = ((0, 0), (0, n * C - L), (0, 0), (0, 0))

    def chunks(t):
        t = jnp.pad(t.astype(f32), padw)
        return t.reshape(B, n, C, H, t.shape[-1]).transpose(1, 0, 3, 2, 4)

    qc, kc, vc, gc = chunks(q), chunks(k), chunks(v), chunks(log_a)
    G = jnp.cumsum(gc, axis=3)
    G_last = G[:, :, :, -1:, :]
    q_dec = qc * jnp.exp(G)
    k_inv = kc * jnp.exp(-G)
    k_tail = kc * jnp.exp(G_last - G)
    causal = jnp.tril(jnp.ones((C, C), dtype=bool))
    A = jnp.where(causal, jnp.einsum('nbhid,nbhjd->nbhij', q_dec, k_inv), 0.0)
    o_intra = jnp.einsum('nbhij,nbhjv->nbhiv', A, vc)

    def step(S, inp):
        q_d, k_t, v_c, g_l = inp
        o_inter = jnp.einsum('bhid,bhdv->bhiv', q_d, S)
        S = S * jnp.exp(g_l)[:, :, 0, :, None] + jnp.einsum('bhjd,bhjv->bhdv', k_t, v_c)
        return S, o_inter

    S_fin, o_inter = lax.scan(step, S0.astype(f32), (q_dec, k_tail, vc, G_last))
    o = (o_intra + o_inter).transpose(1, 0, 3, 2, 4).reshape(B, n * C, H, DV)[:, :L]
    return o, S_fin.astype(S0.dtype)


def rwkv7_scan(r, w, k, v, kk, a, S0):
    f32 = jnp.float32
    xs = tuple(jnp.moveaxis(t.astype(f32), 1, 0) for t in (r, w, k, v, kk, a))

    def step(S, inp):
        r_t, w_t, k_t, v_t, kk_t, a_t = inp
        sa = jnp.einsum('bhij,bhj->bhi', S, -kk_t)
        S = (S * w_t[:, :, None, :] + sa[..., None] * (kk_t * a_t)[:, :, None, :]
             + v_t[..., None] * k_t[:, :, None, :])
        return S, jnp.einsum('bhij,bhj->bhi', S, r_t)

    S, y = lax.scan(step, S0.astype(f32), xs)
    return jnp.moveaxis(y, 0, 1), S.astype(S0.dtype)


def moba_attend(q, q_pos, k_blocks, v_blocks, k_mean):
    f32 = jnp.float32
    n_blocks, blk, H, dh = k_blocks.shape
    Lq = q.shape[0]
    n_sel = min(MOBA_TOPK, n_blocks)
    own = q_pos // blk
    qf = q.astype(f32)
    gate = jnp.einsum('qhd,nhd->qhn', qf, k_mean)
    fully_past = jnp.arange(n_blocks)[None, :] < own[:, None]
    gate = jnp.where(fully_past[:, None, :], gate, -jnp.inf)
    top_val, top_idx = lax.top_k(gate, n_sel)
    own_idx = jnp.broadcast_to(own[:, None, None], (Lq, H, 1)).astype(jnp.int32)
    blk_idx = jnp.concatenate([top_idx.astype(jnp.int32), own_idx], -1)
    blk_ok = jnp.concatenate([jnp.isfinite(top_val), jnp.ones((Lq, H, 1), dtype=bool)], -1)
    head = jnp.arange(H)[None, :, None]
    kg = jnp.moveaxis(k_blocks, 2, 0)[head, blk_idx]
    vg = jnp.moveaxis(v_blocks, 2, 0)[head, blk_idx]
    s = jnp.einsum('qhd,qhsbd->qhsb', qf, kg.astype(f32)) * (dh ** -0.5)
    key_pos = blk_idx[..., None] * blk + jnp.arange(blk)
    ok = blk_ok[..., None] & (key_pos <= q_pos[:, None, None, None])
    s = jnp.where(ok, s, -jnp.inf)
    p = jax.nn.softmax(s.reshape(Lq, H, -1), axis=-1).reshape(s.shape)
    out = jnp.einsum('qhsb,qhsbd->qhd', p, vg.astype(f32))
    return out.astype(q.dtype)


def moba_prompt(q, k, v, pos):
    B, L, H, dh = q.shape
    nb = -(-L // MOBA_BLOCK)
    padw = ((0, 0), (0, nb * MOBA_BLOCK - L), (0, 0), (0, 0))
    kb = jnp.pad(k, padw).reshape(B, nb, MOBA_BLOCK, H, dh)
    vb = jnp.pad(v, padw).reshape(B, nb, MOBA_BLOCK, H, dh)
    k_mean = kb.astype(jnp.float32).mean(axis=2)
    nq = L // MOBA_QBLOCK
    qb = q.reshape(B * nq, MOBA_QBLOCK, H, dh)
    pb = jnp.broadcast_to(pos.reshape(1, nq, MOBA_QBLOCK), (B, nq, MOBA_QBLOCK)).reshape(B * nq, MOBA_QBLOCK)
    bi = jnp.repeat(jnp.arange(B), nq)

    def one(args):
        qq, pp, b = args
        return moba_attend(qq, pp, kb[b], vb[b], k_mean[b])

    return lax.map(one, (qb, pb, bi)).reshape(B, L, H, dh)


def moba_sample(q, k, v, pos, cache_k_l, cache_v_l, page_table):
    Bd, Ld, H, dh = q.shape
    past = page_table.shape[1] * cache_k_l.shape[1]
    total = past + Ld
    nb = -(-total // MOBA_BLOCK)
    padw = ((0, nb * MOBA_BLOCK - total), (0, 0), (0, 0))

    def one(args):
        pt, qq, kn, vn = args
        kf = jnp.concatenate([cache_k_l[pt].reshape(past, H, dh), kn], 0)
        vf = jnp.concatenate([cache_v_l[pt].reshape(past, H, dh), vn], 0)
        kf = jnp.pad(kf, padw).reshape(nb, MOBA_BLOCK, H, dh)
        vf = jnp.pad(vf, padw).reshape(nb, MOBA_BLOCK, H, dh)
        return moba_attend(qq, pos, kf, vf, kf.astype(jnp.float32).mean(axis=1))

    return lax.map(one, (page_table, q, k, v))


def mixer(x, pos, p, gla_S0, wkv_S0, shift0, attend):
    f32 = jnp.float32
    B, L, _ = x.shape
    W = BRANCH_WIDTH
    gq, gk, gv, g_low, g_out, mq, mk, mv, pr, pg = split_cols(x @ p['w_in'], IN_SPLITS)

    q = (gq * GLA_DK ** -0.5).reshape(B, L, GLA_HEADS, GLA_DK)
    k = gk.reshape(B, L, GLA_HEADS, GLA_DK)
    v = gv.reshape(B, L, GLA_HEADS, GLA_DV)
    log_a = jax.nn.log_sigmoid((g_low @ p['gla_gk_up'] + p['gla_gk_bias']).astype(f32)) / GLA_GATE_NORMALIZER
    o, gla_S = gla_chunked(q, k, v, log_a.reshape(B, L, GLA_HEADS, GLA_DK), gla_S0)
    o = o * lax.rsqrt(jnp.square(o).mean(-1, keepdims=True) + GLA_NORM_EPS) * p['gla_norm_w']
    y_gla = (o * jax.nn.silu(g_out.astype(f32)).reshape(B, L, GLA_HEADS, GLA_DV)).reshape(B, L, W).astype(x.dtype)

    mq = rope(mq.reshape(B, L, MOBA_HEADS, MOBA_HEAD_DIM), pos)
    mk = rope(mk.reshape(B, L, MOBA_HEADS, MOBA_HEAD_DIM), pos)
    mv = mv.reshape(B, L, MOBA_HEADS, MOBA_HEAD_DIM)
    y_moba = attend(mq, mk, mv).reshape(B, L, W).astype(x.dtype)

    prev = jnp.concatenate([shift0[:, None, :].astype(pr.dtype), pr[:, :-1]], axis=1)
    ps = pr + p['rwkv_mu'] * (prev - pr)
    r, k7, v7, w_low, a_low, gr_low = split_cols(ps, RWKV_SPLITS)
    hs = (B, L, RWKV_HEADS, RWKV_HEAD)
    w = jnp.exp(-RWKV_DECAY_SCALE * jax.nn.sigmoid((p['rwkv_w0'] + jnp.tanh(w_low) @ p['rwkv_w_up']).astype(f32)))
    a = jax.nn.sigmoid((p['rwkv_a0'] + a_low @ p['rwkv_a_up']).astype(f32))
    g7 = jax.nn.sigmoid(gr_low) @ p['rwkv_g_up']
    kk = (k7 * p['rwkv_k_k']).astype(f32).reshape(hs)
    kk = kk * lax.rsqrt(jnp.sum(jnp.square(kk), -1, keepdims=True) + 1e-12)
    k7 = (k7.astype(f32) * (1.0 + (a - 1.0) * p['rwkv_k_a'])).reshape(hs)
    r = r.astype(f32).reshape(hs)
    v7 = v7.astype(f32).reshape(hs)
    yw, wkv_S = rwkv7_scan(r, w.reshape(hs), k7, v7, kk, a.reshape(hs), wkv_S0)
    mu_w = yw.mean(-1, keepdims=True)
    var_w = jnp.square(yw - mu_w).mean(-1, keepdims=True)
    yw = ((yw - mu_w) * lax.rsqrt(var_w + RWKV_GN_EPS)).reshape(B, L, W) * p['rwkv_ln_w'] + p['rwkv_ln_b']
    bonus = jnp.sum(r * k7 * p['rwkv_r_k'], -1, keepdims=True) * v7
    y_rwkv = ((yw + bonus.reshape(B, L, W)) * g7).astype(x.dtype)

    gates = jax.nn.sigmoid((pg + p['b_gate'].reshape(-1)).astype(f32)).reshape(B, L, N_BRANCH, D_MODEL)
    branches = jnp.stack([y_gla, y_moba, y_rwkv], axis=2)
    proj = jnp.einsum('blnw,nwd->blnd', branches, p['w_branch'])
    merged = jnp.einsum('blnd,blnd->bld', gates, proj.astype(f32)).astype(x.dtype)
    out = merged @ p['w_out']
    return out, (mk, mv, gla_S, wkv_S, pr[:, -1, :])


def layer(x, pos, p, gla_S0, wkv_S0, shift0, attend):
    mix, new_state = mixer(x, pos, p, gla_S0, wkv_S0, shift0, attend)
    x = layer_norm(ALPHA * x + mix, p['ln1_g'], p['ln1_b'])
    h = jnp.square(jax.nn.relu(x @ p['w_up'])) @ p['w_down']
    x = layer_norm(ALPHA * x + h, p['ln2_g'], p['ln2_b'])
    return x, new_state


def setup_inputs(seed: int = 0) -> dict:
    key = jax.random.key(seed)
    ks = iter(jax.random.split(key, 40))

    def nrm(shape, scale):
        return jax.random.normal(next(ks), shape, jnp.float32) * scale

    W = BRANCH_WIDTH
    n_pages = PAST_LEN // PAGE_SIZE
    n_used = DEC_BATCH * n_pages
    n_pool = n_used + max(1, n_used // 4)
    page_table = jax.random.permutation(next(ks), n_pool)[:n_used].reshape(DEC_BATCH, n_pages).astype(jnp.int32)
    kv_shape = (DEPTH, n_pool, PAGE_SIZE, MOBA_HEADS, MOBA_HEAD_DIM)
    return {
        'x_prompt': nrm((BATCH, SEQ, D_MODEL), 1.0),
        'x_sample': nrm((DEC_BATCH, DEC_SEQ, D_MODEL), 1.0),
        'cache_k': nrm(kv_shape, 1.0),
        'cache_v': nrm(kv_shape, 1.0),
        'page_table': page_table,
        'state_gla': nrm((DEPTH, DEC_BATCH, GLA_HEADS, GLA_DK, GLA_DV), 0.3),
        'state_wkv': nrm((DEPTH, DEC_BATCH, RWKV_HEADS, RWKV_HEAD, RWKV_HEAD), 0.3),
        'state_shift': nrm((DEPTH, DEC_BATCH, RWKV_PROJ), 1.0),
        'w_in': nrm((DEPTH, D_MODEL, IN_PROJ), D_MODEL ** -0.5),
        'b_gate': nrm((DEPTH, N_BRANCH, D_MODEL), 0.1),
        'gla_gk_up': nrm((DEPTH, GLA_GATE_RANK, GLA_HEADS * GLA_DK), GLA_GATE_RANK ** -0.5),
        'gla_gk_bias': nrm((DEPTH, GLA_HEADS * GLA_DK), 0.1),
        'gla_norm_w': 1.0 + nrm((DEPTH, GLA_DV), 0.02),
        'rwkv_mu': jax.random.uniform(next(ks), (DEPTH, RWKV_PROJ), jnp.float32),
        'rwkv_w0': nrm((DEPTH, W), 0.5) - 0.5,
        'rwkv_w_up': nrm((DEPTH, RWKV_DECAY_RANK, W), 0.1),
        'rwkv_a0': nrm((DEPTH, W), 0.1),
        'rwkv_a_up': nrm((DEPTH, RWKV_A_RANK, W), RWKV_A_RANK ** -0.5),
        'rwkv_g_up': nrm((DEPTH, RWKV_GATE_RANK, W), RWKV_GATE_RANK ** -0.5),
        'rwkv_k_k': 0.85 + nrm((DEPTH, W), 0.02),
        'rwkv_k_a': 1.0 + nrm((DEPTH, W), 0.02),
        'rwkv_r_k': nrm((DEPTH, RWKV_HEADS, RWKV_HEAD), 0.1),
        'rwkv_ln_w': 1.0 + nrm((DEPTH, W), 0.02),
        'rwkv_ln_b': nrm((DEPTH, W), 0.02),
        'w_branch': nrm((DEPTH, N_BRANCH, W, D_MODEL), BETA * W ** -0.5),
        'w_out': nrm((DEPTH, D_MODEL, D_MODEL), BETA * D_MODEL ** -0.5),
        'ln1_g': 1.0 + nrm((DEPTH, D_MODEL), 0.02),
        'ln1_b': nrm((DEPTH, D_MODEL), 0.02),
        'w_up': nrm((DEPTH, D_MODEL, D_FF), BETA * D_MODEL ** -0.5),
        'w_down': nrm((DEPTH, D_FF, D_MODEL), BETA * D_FF ** -0.5),
        'ln2_g': 1.0 + nrm((DEPTH, D_MODEL), 0.02),
        'ln2_b': nrm((DEPTH, D_MODEL), 0.02),
    }


def reference(x_prompt, x_sample, cache_k, cache_v, page_table, state_gla, state_wkv, state_shift,
              w_in, b_gate, gla_gk_up, gla_gk_bias, gla_norm_w, rwkv_mu, rwkv_w0, rwkv_w_up,
              rwkv_a0, rwkv_a_up, rwkv_g_up, rwkv_k_k, rwkv_k_a, rwkv_r_k, rwkv_ln_w, rwkv_ln_b,
              w_branch, w_out, ln1_g, ln1_b, w_up, w_down, ln2_g, ln2_b):
    B, L, _ = x_prompt.shape
    Bd, Ld, _ = x_sample.shape
    past = page_table.shape[1] * cache_k.shape[2]
    pos_p = jnp.arange(L, dtype=jnp.int32)
    pos_s = past + jnp.arange(Ld, dtype=jnp.int32)
    attend_p = functools.partial(moba_prompt, pos=pos_p)
    yp, ys = x_prompt, x_sample
    outs_p, outs_s = [], []
    for l in range(DEPTH):
        p = {
            'w_in': w_in[l], 'b_gate': b_gate[l], 'gla_gk_up': gla_gk_up[l], 'gla_gk_bias': gla_gk_bias[l],
            'gla_norm_w': gla_norm_w[l], 'rwkv_mu': rwkv_mu[l], 'rwkv_w0': rwkv_w0[l],
            'rwkv_w_up': rwkv_w_up[l], 'rwkv_a0': rwkv_a0[l], 'rwkv_a_up': rwkv_a_up[l],
            'rwkv_g_up': rwkv_g_up[l], 'rwkv_k_k': rwkv_k_k[l], 'rwkv_k_a': rwkv_k_a[l],
            'rwkv_r_k': rwkv_r_k[l], 'rwkv_ln_w': rwkv_ln_w[l], 'rwkv_ln_b': rwkv_ln_b[l],
            'w_branch': w_branch[l], 'w_out': w_out[l], 'ln1_g': ln1_g[l], 'ln1_b': ln1_b[l],
            'w_up': w_up[l], 'w_down': w_down[l], 'ln2_g': ln2_g[l], 'ln2_b': ln2_b[l],
        }
        yp, st_p = layer(yp, pos_p, p,
                         jnp.zeros((B, GLA_HEADS, GLA_DK, GLA_DV), state_gla.dtype),
                         jnp.zeros((B, RWKV_HEADS, RWKV_HEAD, RWKV_HEAD), state_wkv.dtype),
                         jnp.zeros((B, RWKV_PROJ), state_shift.dtype),
                         attend_p)
        attend_s = functools.partial(moba_sample, pos=pos_s, cache_k_l=cache_k[l],
                                     cache_v_l=cache_v[l], page_table=page_table)
        ys, st_s = layer(ys, pos_s, p, state_gla[l], state_wkv[l], state_shift[l], attend_s)
        outs_p.append(st_p)
        outs_s.append(st_s)
    k_p, v_p, gla_p, wkv_p, shift_p = [jnp.stack(t) for t in zip(*outs_p)]
    k_s, v_s, gla_s, wkv_s, shift_s = [jnp.stack(t) for t in zip(*outs_s)]
    return (yp, ys, k_p, v_p, k_s, v_s, gla_p, gla_s, wkv_p, wkv_s, shift_p, shift_s)
```

```python
import functools

import jax
import jax.numpy as jnp
from jax import lax
from jax.experimental import pallas as pl
from jax.experimental.pallas import tpu as pltpu

f32, bf16 = jnp.float32, jnp.bfloat16

D_MODEL = 1024
DEPTH = 2
PAGE_SIZE = 128
BRANCH_WIDTH = D_MODEL // 2
N_BRANCH = 3
GLA_HEADS = 4
GLA_DV = BRANCH_WIDTH // GLA_HEADS
GLA_DK = GLA_DV // 2
GLA_GATE_RANK = 16
GLA_GATE_NORMALIZER = 16.0
GLA_CHUNK = 64
GLA_NORM_EPS = 1e-5
MOBA_HEAD_DIM = 64
MOBA_HEADS = BRANCH_WIDTH // MOBA_HEAD_DIM
MOBA_BLOCK = 256
MOBA_TOPK = 3
ROPE_THETA = 10000.0
RWKV_HEAD = 64
RWKV_HEADS = BRANCH_WIDTH // RWKV_HEAD
RWKV_DECAY_RANK = 64
RWKV_A_RANK = 64
RWKV_GATE_RANK = 128
RWKV_DECAY_SCALE = 0.606531
RWKV_GN_EPS = 64e-5
RWKV_PROJ = 3 * BRANCH_WIDTH + RWKV_DECAY_RANK + RWKV_A_RANK + RWKV_GATE_RANK
RWKV_CHUNK = 64
RWKV_INV_BLOCK = 16
D_FF = 4 * D_MODEL
ALPHA = (2 * DEPTH) ** 0.25
LN_EPS = 1e-5

LANES = 128
VMEM_LIMIT_BYTES = 56 * 1024 * 1024

W = BRANCH_WIDTH
C_GQ, C_GK, C_GV, C_GO = 0, 256, 512, 1024
C_MQ, C_MK, C_MV = 1536, 2048, 2560
C_PG = 3072
C_PR = 6144
C_R, C_K7, C_V7 = C_PR, C_PR + W, C_PR + 2 * W
C_WA = C_PR + 3 * W
C_GR = C_WA + LANES
C_GLOW = C_PR + RWKV_PROJ
N_PACK = C_GLOW + LANES

_O_GQ, _O_GK, _O_GV, _O_GLOW, _O_GO = 0, 256, 512, 1024, 1040
_O_MQ, _O_MK, _O_MV, _O_PR, _O_PG = 1552, 2064, 2576, 3088, 4880

NT_DIMS = (((1,), (1,)), ((), ()))
TN_DIMS = (((0,), (0,)), ((), ()))


def _params(sem):
    return pltpu.CompilerParams(dimension_semantics=sem, vmem_limit_bytes=VMEM_LIMIT_BYTES)


def _bdot(a, b):
    return jnp.dot(a.astype(bf16), b.astype(bf16), preferred_element_type=f32)


def _bdot_nt(a, b):
    return lax.dot_general(a.astype(bf16), b.astype(bf16), NT_DIMS, preferred_element_type=f32)


def _bdot_tn(a, b):
    return lax.dot_general(a.astype(bf16), b.astype(bf16), TN_DIMS, preferred_element_type=f32)


def _split2(x):
    hi = x.astype(bf16)
    lo = (x - hi.astype(f32)).astype(bf16)
    return hi, lo


def _dot_exact_rhs(x, m_bf16):
    hi, lo = _split2(x)
    return (jnp.dot(hi, m_bf16, preferred_element_type=f32)
            + jnp.dot(lo, m_bf16, preferred_element_type=f32))


def _dot_exact_lhs(m_bf16, x):
    hi, lo = _split2(x)
    return (jnp.dot(m_bf16, hi, preferred_element_type=f32)
            + jnp.dot(m_bf16, lo, preferred_element_type=f32))


def _dot3_nt(a, b):
    ah, al = _split2(a)
    bh, bl = _split2(b)
    d = lambda u, v: lax.dot_general(u, v, NT_DIMS, preferred_element_type=f32)
    return d(ah, bh) + d(al, bh) + d(ah, bl)


def _iota(shape, dim):
    return lax.broadcasted_iota(jnp.int32, shape, dim)


def _seg_matrix(n, seg):
    return (_iota((n, n), 0) // seg == _iota((n, n), 1) // seg).astype(bf16)


def _layer_norm(z, g, b):
    mu = jnp.mean(z, -1, keepdims=True)
    d = z - mu
    var = jnp.mean(d * d, -1, keepdims=True)
    return d * lax.rsqrt(var + LN_EPS) * g + b


def _row_to_col(row, n):
    eye = _iota((n, n), 0) == _iota((n, n), 1)
    return jnp.sum(jnp.where(eye, jnp.broadcast_to(row, (n, n)), 0.0), axis=1, keepdims=True)


def _set_row(ref, idx, row):
    cur = ref[...]
    ref[...] = jnp.where(_iota(cur.shape, 0) == idx, jnp.broadcast_to(row, cur.shape), cur)


def _col_to_row(col, n):
    eye = _iota((n, n), 0) == _iota((n, n), 1)
    return jnp.sum(jnp.where(eye, jnp.broadcast_to(col, (n, n)), 0.0), axis=0, keepdims=True)


def _inproj_body(x_ref, w_ref, o_ref, xb_ref):
    @pl.when(pl.program_id(1) == 0)
    def _():
        xb_ref[...] = x_ref[...].astype(bf16)

    o_ref[...] = jnp.dot(xb_ref[...], w_ref[...], preferred_element_type=f32)


def in_proj(x, w, tm, tn):
    m, k = x.shape
    n = w.shape[1]
    return pl.pallas_call(
        _inproj_body,
        grid=(m // tm, n // tn),
        in_specs=[pl.BlockSpec((tm, k), lambda i, j: (i, 0)),
                  pl.BlockSpec((k, tn), lambda i, j: (0, j))],
        out_specs=pl.BlockSpec((tm, tn), lambda i, j: (i, j)),
        out_shape=jax.ShapeDtypeStruct((m, n), f32),
        scratch_shapes=[pltpu.VMEM((tm, k), bf16)],
        compiler_params=_params(("arbitrary", "arbitrary")),
        name="in_proj",
    )(x, w)


def _merge_body(x_ref, pg0_ref, pg1_ref, pg2_ref, y0_ref, y1_ref, y2_ref, wb_ref, bg_ref, wo_ref,
                g_ref, b_ref, o_ref):
    merged = None
    for n, (pg_ref, y_ref) in enumerate(((pg0_ref, y0_ref), (pg1_ref, y1_ref), (pg2_ref, y2_ref))):
        gate = jax.nn.sigmoid(pg_ref[...] + bg_ref[n:n + 1, :])
        term = gate * jnp.dot(y_ref[...], wb_ref[n], preferred_element_type=f32)
        merged = term if merged is None else merged + term
    out = jnp.dot(merged.astype(bf16), wo_ref[...], preferred_element_type=f32)
    o_ref[...] = _layer_norm(ALPHA * x_ref[...] + out, g_ref[...], b_ref[...])


def merge_ln(x, proj, y_gla, y_moba, y_rwkv, w_branch, b_gate, w_out, g, b, tm):
    m = x.shape[0]
    row = lambda i: (i, 0)
    const2 = lambda i: (0, 0)
    pg_spec = lambda n: pl.BlockSpec((tm, D_MODEL), lambda i, n=n: (i, C_PG // D_MODEL + n))
    y_spec = pl.BlockSpec((tm, W), row)
    return pl.pallas_call(
        _merge_body,
        grid=(m // tm,),
        in_specs=[pl.BlockSpec((tm, D_MODEL), row), pg_spec(0), pg_spec(1), pg_spec(2),
                  y_spec, y_spec, y_spec,
                  pl.BlockSpec((N_BRANCH, W, D_MODEL), lambda i: (0, 0, 0)),
                  pl.BlockSpec((N_BRANCH, D_MODEL), const2),
                  pl.BlockSpec((D_MODEL, D_MODEL), const2),
                  pl.BlockSpec((1, D_MODEL), const2), pl.BlockSpec((1, D_MODEL), const2)],
        out_specs=pl.BlockSpec((tm, D_MODEL), row),
        out_shape=jax.ShapeDtypeStruct((m, D_MODEL), f32),
        compiler_params=_params(("arbitrary",)),
        name="merge_ln",
    )(x, proj, proj, proj, y_gla, y_moba, y_rwkv, w_branch, b_gate, w_out, g, b)


FF_CHUNK = 1024


def _mlp_body(x_ref, wu_ref, wd_ref, g_ref, b_ref, o_ref):
    x = x_ref[...]
    xb = x.astype(bf16)
    acc = None
    for c in range(D_FF // FF_CHUNK):
        h = jnp.dot(xb, wu_ref[:, c * FF_CHUNK:(c + 1) * FF_CHUNK], preferred_element_type=f32)
        h = jnp.square(jnp.maximum(h, 0.0)).astype(bf16)
        part = jnp.dot(h, wd_ref[c * FF_CHUNK:(c + 1) * FF_CHUNK, :], preferred_element_type=f32)
        acc = part if acc is None else acc + part
    o_ref[...] = _layer_norm(ALPHA * x + acc, g_ref[...], b_ref[...])


def mlp_ln(x, w_up, w_down, g, b, tm):
    m = x.shape[0]
    row = lambda i: (i, 0)
    const2 = lambda i: (0, 0)
    return pl.pallas_call(
        _mlp_body,
        grid=(m // tm,),
        in_specs=[pl.BlockSpec((tm, D_MODEL), row),
                  pl.BlockSpec((D_MODEL, D_FF), const2, pipeline_mode=pl.Buffered(1)),
                  pl.BlockSpec((D_FF, D_MODEL), const2, pipeline_mode=pl.Buffered(1)),
                  pl.BlockSpec((1, D_MODEL), const2), pl.BlockSpec((1, D_MODEL), const2)],
        out_specs=pl.BlockSpec((tm, D_MODEL), row),
        out_shape=jax.ShapeDtypeStruct((m, D_MODEL), f32),
        compiler_params=_params(("arbitrary",)),
        name="mlp_ln",
    )(x, w_up, w_down, g, b)


def _gla_log_gate(glow, up_ref, bias_ref):
    z = _bdot(glow, up_ref[...]) + bias_ref[...]
    return jax.nn.log_sigmoid(z) * (1.0 / GLA_GATE_NORMALIZER)


def _gla_out(o, norm_w, g_out):
    o = o * lax.rsqrt(jnp.mean(jnp.square(o), -1, keepdims=True) + GLA_NORM_EPS) * norm_w
    return o * jax.nn.silu(g_out)


def _gla_prompt_body(q_ref, k_ref, v_ref, go_ref, glow_ref, up_ref, bias_ref, nw_ref,
                     y_ref, s_out_ref, s_ref):
    t = pl.program_id(1)
    rows = q_ref.shape[0]
    c_sz = GLA_CHUNK

    @pl.when(t == 0)
    def _():
        s_ref[...] = jnp.zeros_like(s_ref)

    lg = _gla_log_gate(glow_ref[...], up_ref, bias_ref)
    tril = (_iota((c_sz, c_sz), 1) <= _iota((c_sz, c_sz), 0))
    tril_b = tril.astype(bf16)
    for c in range(rows // c_sz):
        rs = slice(c * c_sz, (c + 1) * c_sz)
        g_cum = _dot_exact_lhs(tril_b, lg[rs])
        g_last = g_cum[c_sz - 1:c_sz, :]
        q_dec = q_ref[rs, :] * (GLA_DK ** -0.5) * jnp.exp(g_cum)
        k = k_ref[rs, :]
        k_inv = k * jnp.exp(-g_cum)
        k_tail = k * jnp.exp(g_last - g_cum)
        e_last = jnp.exp(g_last)
        for h in range(GLA_HEADS):
            ks = slice(h * GLA_DK, (h + 1) * GLA_DK)
            vs = slice(h * GLA_DV, (h + 1) * GLA_DV)
            v_h = v_ref[rs, vs]
            qd = q_dec[:, ks]
            a = jnp.where(tril, _bdot_nt(qd, k_inv[:, ks]), 0.0)
            s_h = s_ref[h]
            o = _bdot(a, v_h) + _bdot(qd, s_h)
            s_ref[h] = s_h * _row_to_col(e_last[:, ks], GLA_DK) + _bdot_tn(k_tail[:, ks], v_h)
            y_ref[rs, vs] = _gla_out(o, nw_ref[...], go_ref[rs, vs]).astype(y_ref.dtype)
    s_out_ref[...] = s_ref[...]


def gla_prompt(proj, up_pad, bias, norm_w, n_batch, seq, rt):
    nt = seq // rt
    rowmap = lambda width, col: pl.BlockSpec((rt, width), lambda b, t: (b * nt + t, col // width))
    const2 = lambda b, t: (0, 0)
    return pl.pallas_call(
        _gla_prompt_body,
        grid=(n_batch, nt),
        in_specs=[rowmap(GLA_HEADS * GLA_DK, C_GQ), rowmap(GLA_HEADS * GLA_DK, C_GK), rowmap(W, C_GV),
                  rowmap(W, C_GO), rowmap(LANES, C_GLOW),
                  pl.BlockSpec((LANES, GLA_HEADS * GLA_DK), const2),
                  pl.BlockSpec((1, GLA_HEADS * GLA_DK), const2),
                  pl.BlockSpec((1, GLA_DV), const2)],
        out_specs=[pl.BlockSpec((rt, W), lambda b, t: (b * nt + t, 0)),
                   pl.BlockSpec((None, GLA_HEADS, GLA_DK, GLA_DV), lambda b, t: (b, 0, 0, 0))],
        out_shape=[jax.ShapeDtypeStruct((n_batch * seq, W), bf16),
                   jax.ShapeDtypeStruct((n_batch, GLA_HEADS, GLA_DK, GLA_DV), f32)],
        scratch_shapes=[pltpu.VMEM((GLA_HEADS, GLA_DK, GLA_DV), f32)],
        compiler_params=_params(("arbitrary", "arbitrary")),
        name="gla_prompt",
    )(proj, proj, proj, proj, proj, up_pad, bias, norm_w)


def _rwkv_pre(r_s, k_s, v_s, wa_s, gr_s, p):
    seg = _seg_matrix(W, RWKV_HEAD)
    logw = -RWKV_DECAY_SCALE * jax.nn.sigmoid(p["w0"][...] + _bdot(jnp.tanh(wa_s), p["w_up"][...]))
    a = jax.nn.sigmoid(p["a0"][...] + _bdot(wa_s, p["a_up"][...]))
    g7 = _bdot(jax.nn.sigmoid(gr_s), p["g_up"][...])
    kk = k_s * p["k_k"][...]
    kk = kk * lax.rsqrt(_dot_exact_rhs(jnp.square(kk), seg) + 1e-12)
    k7 = k_s * (1.0 + (a - 1.0) * p["k_a"][...])
    bonus = _dot_exact_rhs(r_s * k7 * p["r_k"][...], seg) * v_s
    return dict(r=r_s, logw=logw, k=k7, v=v_s, kk=kk, a=a, g7=g7, bonus=bonus)


def _rwkv_post(yw, bonus, g7, p):
    seg = _seg_matrix(W, RWKV_HEAD)
    inv_n = 1.0 / RWKV_HEAD
    mu = _dot_exact_rhs(yw, seg) * inv_n
    d = yw - mu
    var = _dot_exact_rhs(jnp.square(d), seg) * inv_n
    yn = d * lax.rsqrt(var + RWKV_GN_EPS) * p["ln_w"][...] + p["ln_b"][...]
    return (yn + bonus) * g7


def _inv_unit_lower(n_mat):
    c = n_mat.shape[0]
    row, col = _iota((c, c), 0), _iota((c, c), 1)
    eye = (row == col).astype(f32)
    blk = (row // RWKV_INV_BLOCK) == (col // RWKV_INV_BLOCK)
    d = jnp.where(blk, n_mat, 0.0)
    lo = n_mat - d
    d2 = _bdot(d, d)
    d4 = _bdot(d2, d2)
    d8 = _bdot(d4, d4)
    d_inv = _bdot(_bdot(eye - d, eye + d2), _bdot(eye + d4, eye + d8))
    e = _bdot(d_inv, lo)
    e2 = _bdot(e, e)
    return _bdot(_bdot(eye - e, eye + e2), d_inv)


def _rwkv_chunk_head(r_t, a_t, b_t, k_t, b_g, k_g, v, gam, s):
    c = r_t.shape[0]
    kd = r_t.shape[1]
    row, col = _iota((2 * c, 2 * c), 0), _iota((2 * c, 2 * c), 1)
    mask = (col % c < row % c) | ((row >= c) & (col % c == row % c))
    prod = jnp.where(mask, _bdot_nt(jnp.concatenate([a_t, r_t], 0), jnp.concatenate([b_t, k_t], 0)), 0.0)
    n_ab, a_ak = prod[:c, :c], prod[:c, c:]
    p_rb, p_rk = prod[c:, :c], prod[c:, c:]
    t_inv = _inv_unit_lower(n_ab)
    x = _bdot(t_inv, jnp.concatenate([a_t, _bdot(a_ak, v)], 1))
    px = _bdot(p_rb, x)
    r_q = r_t - px[:, :kd]
    y_c = _bdot(p_rk, v) - px[:, kd:]
    xb = _bdot_tn(x, b_g)
    eye_k = _iota((kd, kd), 0) == _iota((kd, kd), 1)
    tm_t = jnp.where(eye_k, jnp.broadcast_to(gam, (kd, kd)), 0.0) - xb[:kd]
    mc_t = _bdot_tn(v, k_g) - xb[kd:]
    y = _bdot_nt(r_q, s) + y_c
    return y, _bdot(s, tm_t) + mc_t


_RWKV_PARAM_NAMES = ("mu_r", "mu_k", "mu_v", "mu_wa", "mu_gr", "w0", "w_up", "a0", "a_up", "g_up",
                     "k_k", "k_a", "r_k", "ln_w", "ln_b")


def _rwkv_prompt_body(r_ref, k_ref, v_ref, wa_ref, gr_ref, *rest):
    np_ = len(_RWKV_PARAM_NAMES)
    p = dict(zip(_RWKV_PARAM_NAMES, rest[:np_]))
    y_ref, s_out_ref = rest[np_:np_ + 2]
    s_ref, prev_ref, r_s, lw_s, k_s, v_s, kk_s, b_s, y_s = rest[np_ + 2:]
    t = pl.program_id(1)
    rows = r_ref.shape[0]
    c_sz = RWKV_CHUNK

    @pl.when(t == 0)
    def _():
        s_ref[...] = jnp.zeros_like(s_ref)
        prev_ref[...] = jnp.zeros_like(prev_ref)

    def shifted(x_ref, mu_ref, c0):
        x = x_ref[...]
        width = x.shape[1]
        prev = jnp.where(_iota(x.shape, 0) == 0, prev_ref[0:1, c0:c0 + width], pltpu.roll(x, 1, 0))
        prev_ref[0:1, c0:c0 + width] = x[rows - 1:rows, :]
        return x + mu_ref[...] * (prev - x)

    pre = _rwkv_pre(shifted(r_ref, p["mu_r"], 0), shifted(k_ref, p["mu_k"], W), shifted(v_ref, p["mu_v"], 2 * W),
                    shifted(wa_ref, p["mu_wa"], 3 * W), shifted(gr_ref, p["mu_gr"], 3 * W + LANES), p)
    r_s[...] = pre["r"]
    lw_s[...] = pre["logw"]
    k_s[...] = pre["k"]
    v_s[...] = pre["v"]
    kk_s[...] = pre["kk"]
    b_s[...] = pre["kk"] * pre["a"]

    tril_b = (_iota((c_sz, c_sz), 1) <= _iota((c_sz, c_sz), 0)).astype(bf16)

    def chunk(c, carry):
        rs = pl.ds(pl.multiple_of(c * c_sz, c_sz), c_sz)
        lw = lw_s[rs, :]
        g_cum = _dot_exact_lhs(tril_b, lw)
        g_last = g_cum[c_sz - 1:c_sz, :]
        e_g = jnp.exp(g_cum)
        e_ng = jnp.exp(-g_cum)
        e_gl = jnp.exp(g_last - g_cum)
        kk, b, k, v = kk_s[rs, :], b_s[rs, :], k_s[rs, :], v_s[rs, :]
        r_t = r_s[rs, :] * e_g
        a_t = kk * jnp.exp(g_cum - lw)
        b_t = b * e_ng
        k_t = k * e_ng
        b_g = b * e_gl
        k_g = k * e_gl
        gam = jnp.exp(g_last)
        y_parts = []
        for h in range(RWKV_HEADS):
            hs = slice(h * RWKV_HEAD, (h + 1) * RWKV_HEAD)
            y, s_new = _rwkv_chunk_head(r_t[:, hs], a_t[:, hs], b_t[:, hs], k_t[:, hs], b_g[:, hs], k_g[:, hs],
                                        v[:, hs], gam[:, hs], s_ref[h])
            s_ref[h] = s_new
            y_parts.append(y)
        y_s[rs, :] = jnp.concatenate(y_parts, axis=1)
        return carry

    lax.fori_loop(0, rows // c_sz, chunk, 0)
    y_ref[...] = _rwkv_post(y_s[...], pre["bonus"], pre["g7"], p).astype(y_ref.dtype)
    s_out_ref[...] = s_ref[...]


def _rwkv_param_specs(const_map):
    vec = lambda width: pl.BlockSpec((1, width), const_map)
    mat = lambda r: pl.BlockSpec((r, W), const_map)
    return [vec(W), vec(W), vec(W), vec(LANES), vec(LANES), vec(W), mat(LANES), vec(W), mat(LANES),
            mat(RWKV_GATE_RANK), vec(W), vec(W), vec(W), vec(W), vec(W)]


def rwkv_prompt(proj, params, n_batch, seq, rt):
    nt = seq // rt
    rowmap = lambda width, col: pl.BlockSpec((rt, width), lambda b, t: (b * nt + t, col // width))
    const2 = lambda b, t: (0, 0)
    scr = lambda: pltpu.VMEM((rt, W), f32)
    return pl.pallas_call(
        _rwkv_prompt_body,
        grid=(n_batch, nt),
        in_specs=[rowmap(W, C_R), rowmap(W, C_K7), rowmap(W, C_V7), rowmap(LANES, C_WA), rowmap(LANES, C_GR)]
        + _rwkv_param_specs(const2),
        out_specs=[pl.BlockSpec((rt, W), lambda b, t: (b * nt + t, 0)),
                   pl.BlockSpec((None, RWKV_HEADS, RWKV_HEAD, RWKV_HEAD), lambda b, t: (b, 0, 0, 0))],
        out_shape=[jax.ShapeDtypeStruct((n_batch * seq, W), bf16),
                   jax.ShapeDtypeStruct((n_batch, RWKV_HEADS, RWKV_HEAD, RWKV_HEAD), f32)],
        scratch_shapes=[pltpu.VMEM((RWKV_HEADS, RWKV_HEAD, RWKV_HEAD), f32), pltpu.VMEM((8, RWKV_PROJ), f32),
                        scr(), scr(), scr(), scr(), scr(), scr(), scr()],
        compiler_params=_params(("arbitrary", "arbitrary")),
        name="rwkv_prompt",
    )(proj, proj, proj, proj, proj, *[params[n] for n in _RWKV_PARAM_NAMES])


def _rope(x, cos, sin_signed):
    lane = _iota(x.shape, 1)
    half = MOBA_HEAD_DIM // 2
    width = x.shape[1]
    swapped = jnp.where(lane % MOBA_HEAD_DIM < half, pltpu.roll(x, width - half, 1), pltpu.roll(x, half, 1))
    return x * cos + swapped * sin_signed


def _sample_body(gq_ref, gk_ref, gv_ref, go_ref, glow_ref, r_ref, k_ref, v_ref, wa_ref, gr_ref, mq_ref, mk_ref,
                 sg_ref, sw_ref, shift_ref, cos_ref, sin_ref, up_ref, bias_ref, nw_ref, *rest):
    np_ = len(_RWKV_PARAM_NAMES)
    p = dict(zip(_RWKV_PARAM_NAMES, rest[:np_]))
    yg_ref, yr_ref, qs_ref, ks_ref, sg_out, sw_out = rest[np_:np_ + 6]
    q_s, kg_s, eg_s, r_s, w_s, k_s, v_s, kk_s, b_s, og_s, yw_s = rest[np_ + 6:]
    n_seq = gq_ref.shape[0]

    qs_ref[...] = _rope(mq_ref[...], cos_ref[...], sin_ref[...]) * (MOBA_HEAD_DIM ** -0.5)
    ks_ref[...] = _rope(mk_ref[...], cos_ref[...], sin_ref[...])

    lg = _gla_log_gate(glow_ref[...], up_ref, bias_ref)
    q_s[...] = gq_ref[...] * (GLA_DK ** -0.5)
    kg_s[...] = gk_ref[...]
    eg_s[...] = jnp.exp(lg)

    def shifted(x_ref, mu_ref, c0):
        x = x_ref[...]
        return x + mu_ref[...] * (shift_ref[:, c0:c0 + x.shape[1]] - x)

    pre = _rwkv_pre(shifted(r_ref, p["mu_r"], 0), shifted(k_ref, p["mu_k"], W), shifted(v_ref, p["mu_v"], 2 * W),
                    shifted(wa_ref, p["mu_wa"], 3 * W), shifted(gr_ref, p["mu_gr"], 3 * W + LANES), p)
    r_s[...] = pre["r"]
    w_s[...] = jnp.exp(pre["logw"])
    k_s[...] = pre["k"]
    v_s[...] = pre["v"]
    kk_s[...] = pre["kk"]
    b_s[...] = pre["kk"] * pre["a"]

    def one_seq(s, carry):
        row = pl.ds(s, 1)
        eg_row, kg_row, q_row, gv_row = eg_s[row, :], kg_s[row, :], q_s[row, :], gv_ref[row, :]
        o_parts = []
        for h in range(GLA_HEADS):
            ks = slice(h * GLA_DK, (h + 1) * GLA_DK)
            vs = slice(h * GLA_DV, (h + 1) * GLA_DV)
            st = sg_ref[s, h]
            st = (st * _row_to_col(eg_row[:, ks], GLA_DK)
                  + _row_to_col(kg_row[:, ks], GLA_DK) * gv_row[:, vs])
            sg_out[s, h] = st
            o_parts.append(jnp.sum(_row_to_col(q_row[:, ks], GLA_DK) * st, axis=0, keepdims=True))
        _set_row(og_s, s, jnp.concatenate(o_parts, axis=1))
        kk_row, w_row, b_row, v_row, k_row, r_row = (kk_s[row, :], w_s[row, :], b_s[row, :], v_s[row, :],
                                                     k_s[row, :], r_s[row, :])
        y_parts = []
        for h in range(RWKV_HEADS):
            hs = slice(h * RWKV_HEAD, (h + 1) * RWKV_HEAD)
            st = sw_ref[s, h]
            sa = jnp.sum(st * (-kk_row[:, hs]), axis=1, keepdims=True)
            st = st * w_row[:, hs] + sa * b_row[:, hs] + _row_to_col(v_row[:, hs], RWKV_HEAD) * k_row[:, hs]
            sw_out[s, h] = st
            y_col = jnp.sum(st * r_row[:, hs], axis=1, keepdims=True)
            y_parts.append(_col_to_row(y_col, RWKV_HEAD))
        _set_row(yw_s, s, jnp.concatenate(y_parts, axis=1))
        return carry

    lax.fori_loop(0, n_seq, one_seq, 0)

    for h in range(GLA_HEADS):
        vs = slice(h * GLA_DV, (h + 1) * GLA_DV)
        yg_ref[:, vs] = _gla_out(og_s[:, vs], nw_ref[...], go_ref[:, vs]).astype(yg_ref.dtype)
    yr_ref[...] = _rwkv_post(yw_s[...], pre["bonus"], pre["g7"], p).astype(yr_ref.dtype)


def sample_step(proj, row0, n_seq, sb, layer, state_gla, state_wkv, state_shift, cos_s, sin_s,
                up_pad, bias, norm_w, params):
    r0 = row0 // sb
    rowmap = lambda width, col: pl.BlockSpec((sb, width), lambda s: (r0 + s, col // width))
    const2 = lambda s: (0, 0)
    hk = GLA_HEADS * GLA_DK
    out_row = lambda width: pl.BlockSpec((sb, width), lambda s: (s, 0))
    scr = lambda width: pltpu.VMEM((sb, width), f32)
    return pl.pallas_call(
        _sample_body,
        grid=(n_seq // sb,),
        in_specs=[rowmap(hk, C_GQ), rowmap(hk, C_GK), rowmap(W, C_GV), rowmap(W, C_GO), rowmap(LANES, C_GLOW),
                  rowmap(W, C_R), rowmap(W, C_K7), rowmap(W, C_V7), rowmap(LANES, C_WA), rowmap(LANES, C_GR),
                  rowmap(W, C_MQ), rowmap(W, C_MK),
                  pl.BlockSpec((None, sb, GLA_HEADS, GLA_DK, GLA_DV), lambda s: (layer, s, 0, 0, 0)),
                  pl.BlockSpec((None, sb, RWKV_HEADS, RWKV_HEAD, RWKV_HEAD), lambda s: (layer, s, 0, 0, 0)),
                  pl.BlockSpec((None, sb, RWKV_PROJ), lambda s: (layer, s, 0)),
                  pl.BlockSpec((1, W), const2), pl.BlockSpec((1, W), const2),
                  pl.BlockSpec((LANES, hk), const2), pl.BlockSpec((1, hk), const2), pl.BlockSpec((1, GLA_DV), const2)]
        + _rwkv_param_specs(const2),
        out_specs=[out_row(W), out_row(W), out_row(W), out_row(W),
                   pl.BlockSpec((sb, GLA_HEADS, GLA_DK, GLA_DV), lambda s: (s, 0, 0, 0)),
                   pl.BlockSpec((sb, RWKV_HEADS, RWKV_HEAD, RWKV_HEAD), lambda s: (s, 0, 0, 0))],
        out_shape=[jax.ShapeDtypeStruct((n_seq, W), bf16), jax.ShapeDtypeStruct((n_seq, W), bf16),
                   jax.ShapeDtypeStruct((n_seq, W), f32), jax.ShapeDtypeStruct((n_seq, W), f32),
                   jax.ShapeDtypeStruct((n_seq, GLA_HEADS, GLA_DK, GLA_DV), f32),
                   jax.ShapeDtypeStruct((n_seq, RWKV_HEADS, RWKV_HEAD, RWKV_HEAD), f32)],
        scratch_shapes=[scr(hk), scr(hk), scr(hk), scr(W), scr(W), scr(W), scr(W), scr(W), scr(W), scr(W), scr(W)],
        compiler_params=_params(("arbitrary",)),
        name="sample_step",
    )(proj, proj, proj, proj, proj, proj, proj, proj, proj, proj, proj, proj,
      state_gla, state_wkv, state_shift, cos_s, sin_s, up_pad, bias, norm_w,
      *[params[n] for n in _RWKV_PARAM_NAMES])


def _topk_select(gate, n_valid, axis_len, axis):
    idx = _iota(gate.shape, axis)
    cnt = jnp.zeros(gate.shape, f32)
    for m in range(axis_len):
        g_m = lax.slice_in_dim(gate, m, m + 1, axis=axis)
        g_m = jnp.broadcast_to(g_m, gate.shape)
        beats = (g_m > gate) | ((g_m == gate) & (m < idx))
        cnt = cnt + jnp.where(beats, 1.0, 0.0) * jnp.where(m < n_valid, 1.0, 0.0)
    return (idx < n_valid) & (cnt < MOBA_TOPK)


def _rope_prep_body(mq_ref, mk_ref, mv_ref, cos_ref, sin_ref, k_out, qb_out, kb_out, vb_out, sel_out, km_ref):
    i = pl.program_id(1)
    blk = mq_ref.shape[0]
    nb = km_ref.shape[0]
    nh = MOBA_HEADS

    @pl.when(i == 0)
    def _():
        km_ref[...] = jnp.zeros_like(km_ref)

    q = _rope(mq_ref[...], cos_ref[...], sin_ref[...]) * (MOBA_HEAD_DIM ** -0.5)
    k = _rope(mk_ref[...], cos_ref[...], sin_ref[...])
    k_out[...] = k
    qb_out[...] = q.astype(bf16)
    kb_out[...] = k.astype(bf16)
    vb_out[...] = mv_ref[...].astype(bf16)

    km = km_ref[...]
    km_rows = jnp.concatenate([km] * nh, axis=0)
    head_of_row = _iota(km_rows.shape, 0) // nb
    km_rows = jnp.where(head_of_row == _iota(km_rows.shape, 1) // MOBA_HEAD_DIM, km_rows, 0.0)
    gate = _dot3_nt(km_rows, q).reshape(nh, nb, blk)
    sel = _topk_select(gate, i, nb, 1).astype(bf16).reshape(nh * nb, blk)
    sel = jnp.concatenate([sel, jnp.zeros((LANES - nh * nb, blk), bf16)], axis=0)
    eye = (_iota((blk, blk), 0) == _iota((blk, blk), 1)).astype(bf16)
    sel_out[...] = lax.dot_general(eye, sel, NT_DIMS, preferred_element_type=f32)

    _set_row(km_ref, i, jnp.sum(k, axis=0, keepdims=True) * (1.0 / blk))


def rope_prep(proj, cos, sin, n_batch, seq):
    blk = MOBA_BLOCK
    nb = seq // blk
    m = n_batch * seq
    rowmap = lambda col: pl.BlockSpec((blk, W), lambda b, i: (b * nb + i, col // W))
    tab = pl.BlockSpec((blk, W), lambda b, i: (i, 0))
    out = lambda width: pl.BlockSpec((blk, width), lambda b, i: (b * nb + i, 0))
    return pl.pallas_call(
        _rope_prep_body,
        grid=(n_batch, nb),
        in_specs=[rowmap(C_MQ), rowmap(C_MK), rowmap(C_MV), tab, tab],
        out_specs=[out(W), out(W), out(W), out(W), out(LANES)],
        out_shape=[jax.ShapeDtypeStruct((m, W), f32), jax.ShapeDtypeStruct((m, W), bf16),
                   jax.ShapeDtypeStruct((m, W), bf16), jax.ShapeDtypeStruct((m, W), bf16),
                   jax.ShapeDtypeStruct((m, LANES), f32)],
        scratch_shapes=[pltpu.VMEM((nb, W), f32)],
        compiler_params=_params(("arbitrary", "arbitrary")),
        name="rope_prep",
    )(proj, proj, proj, cos, sin)


def _moba_prompt_body(qb_ref, kb_ref, vb_ref, sel_ref, y_ref, m_ref, l_ref, acc_ref):
    i = pl.program_id(1)
    blk = qb_ref.shape[0]
    nb = kb_ref.shape[0] // blk
    lane = _iota((blk, LANES), 1)
    low_half = lane < MOBA_HEAD_DIM
    causal = _iota((blk, blk), 1) <= _iota((blk, blk), 0)
    neg_inf = -jnp.inf

    def head_q(pair, u):
        qp = qb_ref[:, pair * LANES:(pair + 1) * LANES]
        return jnp.where(low_half if u == 0 else ~low_half, qp, jnp.zeros_like(qp))

    def kv(n, pair):
        rs = pl.ds(pl.multiple_of(n * blk, blk), blk)
        ls = slice(pair * LANES, (pair + 1) * LANES)
        return kb_ref[rs, ls], vb_ref[rs, ls]

    wide = lambda x: jnp.concatenate([x, x], axis=1)

    for pair in range(MOBA_HEADS // 2):
        k_n, v_n = kv(i, pair)
        for u in range(2):
            h = 2 * pair + u
            s = jnp.where(causal, lax.dot_general(head_q(pair, u), k_n, NT_DIMS, preferred_element_type=f32), neg_inf)
            m = jnp.broadcast_to(jnp.max(s, axis=1, keepdims=True), (blk, LANES))
            p = jnp.exp(s - wide(m))
            m_ref[h] = m
            l_ref[h] = jnp.broadcast_to(jnp.sum(p, axis=1, keepdims=True), (blk, LANES))
            acc_ref[h] = jnp.dot(p.astype(bf16), v_n, preferred_element_type=f32)

    def past(n, carry):
        sel = sel_ref[...]
        for pair in range(MOBA_HEADS // 2):
            k_n, v_n = kv(n, pair)
            for u in range(2):
                h = 2 * pair + u
                chosen = jnp.sum(jnp.where(lane == h * nb + n, sel, 0.0), axis=1, keepdims=True)
                s = lax.dot_general(head_q(pair, u), k_n, NT_DIMS, preferred_element_type=f32)
                s = jnp.where(jnp.broadcast_to(chosen, s.shape) > 0.5, s, neg_inf)
                m_old = m_ref[h]
                m_new = jnp.maximum(m_old, jnp.broadcast_to(jnp.max(s, axis=1, keepdims=True), (blk, LANES)))
                alpha = jnp.exp(m_old - m_new)
                p = jnp.exp(s - wide(m_new))
                m_ref[h] = m_new
                l_ref[h] = alpha * l_ref[h] + jnp.broadcast_to(jnp.sum(p, axis=1, keepdims=True), (blk, LANES))
                acc_ref[h] = alpha * acc_ref[h] + jnp.dot(p.astype(bf16), v_n, preferred_element_type=f32)
        return carry

    lax.fori_loop(0, i, past, 0)

    for pair in range(MOBA_HEADS // 2):
        o0 = acc_ref[2 * pair] / l_ref[2 * pair]
        o1 = acc_ref[2 * pair + 1] / l_ref[2 * pair + 1]
        y_ref[:, pair * LANES:(pair + 1) * LANES] = jnp.where(low_half, o0, o1).astype(y_ref.dtype)


def moba_prompt(qb, kb, vb, sel, n_batch, seq):
    blk = MOBA_BLOCK
    nb = seq // blk
    row = lambda width: pl.BlockSpec((blk, width), lambda b, i: (b * nb + i, 0))
    full = pl.BlockSpec((seq, W), lambda b, i: (b, 0))
    scr = lambda: pltpu.VMEM((MOBA_HEADS, blk, LANES), f32)
    return pl.pallas_call(
        _moba_prompt_body,
        grid=(n_batch, nb),
        in_specs=[row(W), full, full, row(LANES)],
        out_specs=row(W),
        out_shape=jax.ShapeDtypeStruct((n_batch * seq, W), bf16),
        scratch_shapes=[scr(), scr(), scr()],
        compiler_params=_params(("arbitrary", "arbitrary")),
        name="moba_prompt",
    )(qb, kb, vb, sel)


SAMPLE_PAGES_PER_STEP = 8
SAMPLE_HEAD_ROWS = 16


def _moba_sample_body(pt_ref, q_ref, k_ref, v_ref, *rest):
    pps = SAMPLE_PAGES_PER_STEP
    k_pages, v_pages = rest[:pps], rest[pps:2 * pps]
    y_ref = rest[2 * pps]
    km_ref, m_ref, l_ref, acc_ref = rest[2 * pps + 1:]
    j = pl.program_id(1)
    n_steps = pl.num_programs(1)
    hr = SAMPLE_HEAD_ROWS
    pages_per_block = MOBA_BLOCK // PAGE_SIZE
    blocks_per_step = pps // pages_per_block
    lane = _iota((hr, LANES), 1)

    @pl.when(j == 0)
    def _():
        km_ref[...] = jnp.zeros_like(km_ref)
        m_ref[...] = jnp.zeros_like(m_ref)
        l_ref[...] = jnp.zeros_like(l_ref)

    q_row = q_ref[...]
    head_lane = _iota((hr, W), 1) // MOBA_HEAD_DIM == _iota((hr, W), 0)
    q_heads = jnp.where(head_lane, jnp.broadcast_to(q_row, (hr, W)), 0.0)
    q_heads_b = q_heads.astype(bf16)

    for u in range(blocks_per_step):
        n = j * blocks_per_step + u
        k_blk = jnp.concatenate([k_pages[u * pages_per_block + t][...] for t in range(pages_per_block)], axis=0)
        v_blk = jnp.concatenate([v_pages[u * pages_per_block + t][...] for t in range(pages_per_block)], axis=0)
        _set_row(km_ref, n, jnp.sum(k_blk, axis=0, keepdims=True) * (1.0 / MOBA_BLOCK))
        s = lax.dot_general(q_heads_b, k_blk.astype(bf16), NT_DIMS, preferred_element_type=f32)
        m = jnp.max(s, axis=1, keepdims=True)
        p = jnp.exp(s - m)
        m_ref[...] = jnp.where(lane == n, m, m_ref[...])
        l_ref[...] = jnp.where(lane == n, jnp.sum(p, axis=1, keepdims=True), l_ref[...])
        acc_ref[n] = jnp.dot(p.astype(bf16), v_blk.astype(bf16), preferred_element_type=f32)

    @pl.when(j == n_steps - 1)
    def _():
        nb = n_steps * blocks_per_step
        km = jnp.concatenate([km_ref[...], jnp.zeros((LANES - nb, W), f32)], axis=0)
        gate = _dot3_nt(q_heads, km)
        sel = _topk_select(gate, nb, nb, 1)
        s_own = jnp.sum(q_heads * k_ref[...], axis=1, keepdims=True)
        m_all = m_ref[...]
        m_tot = jnp.maximum(jnp.max(jnp.where(sel, m_all, -jnp.inf), axis=1, keepdims=True), s_own)
        wts = jnp.where(sel, jnp.exp(m_all - m_tot), 0.0)
        w_own = jnp.exp(s_own - m_tot)
        l_tot = jnp.sum(wts * l_ref[...], axis=1, keepdims=True) + w_own
        out = w_own * jnp.broadcast_to(v_ref[...], (hr, W))
        for n in range(nb):
            out = out + wts[:, n:n + 1] * acc_ref[n]
        out = out / l_tot
        y_ref[...] = jnp.sum(jnp.where(head_lane, out, 0.0), axis=0, keepdims=True)


def moba_sample(page_table_flat, q_s, k_s, v_s, cache_k, cache_v, layer, n_pages):
    n_seq = q_s.shape[0]
    pps = SAMPLE_PAGES_PER_STEP
    n_steps = n_pages // pps
    nb = n_pages * PAGE_SIZE // MOBA_BLOCK
    row = pl.BlockSpec((None, 1, W), lambda s, j, pt: (s, 0, 0))
    page = lambda u: pl.BlockSpec((None, None, PAGE_SIZE, W),
                                  lambda s, j, pt, u=u: (layer, pt[s * n_pages + j * pps + u], 0, 0))
    grid_spec = pltpu.PrefetchScalarGridSpec(
        num_scalar_prefetch=1,
        grid=(n_seq, n_steps),
        in_specs=[row, row, row] + [page(u) for u in range(pps)] + [page(u) for u in range(pps)],
        out_specs=row,
        scratch_shapes=[pltpu.VMEM((nb, W), f32), pltpu.VMEM((SAMPLE_HEAD_ROWS, LANES), f32),
                        pltpu.VMEM((SAMPLE_HEAD_ROWS, LANES), f32), pltpu.VMEM((nb, SAMPLE_HEAD_ROWS, W), f32)],
    )
    return pl.pallas_call(
        _moba_sample_body,
        grid_spec=grid_spec,
        out_shape=jax.ShapeDtypeStruct((n_seq, 1, W), f32),
        compiler_params=_params(("arbitrary", "arbitrary")),
        name="moba_sample",
    )(page_table_flat, q_s, k_s, v_s, *([cache_k] * pps), *([cache_v] * pps))


def _rope_tables(pos):
    half = MOBA_HEAD_DIM // 2
    inv = ROPE_THETA ** (-jnp.arange(half, dtype=f32) / half)
    ang = pos.astype(f32)[:, None] * inv[None, :]
    cos, sin = jnp.cos(ang), jnp.sin(ang)
    cos_h = jnp.concatenate([cos, cos], -1)
    sin_h = jnp.concatenate([-sin, sin], -1)
    return jnp.tile(cos_h, (1, MOBA_HEADS)), jnp.tile(sin_h, (1, MOBA_HEADS))


def _pack_w_in(w):
    cols = lambda o, n: w[:, o:o + n]
    hk = GLA_HEADS * GLA_DK
    parts = [cols(_O_GQ, hk), cols(_O_GK, hk), cols(_O_GV, W), cols(_O_GO, W),
             cols(_O_MQ, W), cols(_O_MK, W), cols(_O_MV, W), cols(_O_PG, N_BRANCH * D_MODEL),
             cols(_O_PR, RWKV_PROJ), cols(_O_GLOW, GLA_GATE_RANK),
             jnp.zeros((w.shape[0], LANES - GLA_GATE_RANK), w.dtype)]
    return jnp.concatenate(parts, axis=1).astype(bf16)


def _pad_rows(m, rows):
    return jnp.concatenate([m, jnp.zeros((rows - m.shape[0], m.shape[1]), m.dtype)], axis=0)


SUBLANES = 8
IN_PROJ_ROWS_TARGET, IN_PROJ_TN = 1100, 896
TOKEN_ROWS_TARGET = 700
MIXER_ROWS = 256
SAMPLE_SEQ_BLOCK = 16


def _row_tile(m, target):
    best = SUBLANES
    for t in range(SUBLANES, target + 1, SUBLANES):
        if m % t == 0:
            best = t
    assert m % best == 0
    return best


def kernel(x_prompt, x_sample, cache_k, cache_v, page_table, state_gla, state_wkv, state_shift, w_in, b_gate, gla_gk_up, gla_gk_bias, gla_norm_w, rwkv_mu, rwkv_w0, rwkv_w_up, rwkv_a0, rwkv_a_up, rwkv_g_up, rwkv_k_k, rwkv_k_a, rwkv_r_k, rwkv_ln_w, rwkv_ln_b, w_branch, w_out, ln1_g, ln1_b, w_up, w_down, ln2_g, ln2_b):
    n_batch, seq, _ = x_prompt.shape
    n_seq = x_sample.shape[0]
    n_pages = page_table.shape[1]
    n_pool = cache_k.shape[1]
    mp = n_batch * seq
    past = n_pages * PAGE_SIZE

    cos_p, sin_p = _rope_tables(jnp.arange(seq, dtype=jnp.int32))
    cos_s, sin_s = _rope_tables(jnp.full((1,), past, jnp.int32))
    cache_k = cache_k.reshape(DEPTH, n_pool, PAGE_SIZE, W)
    cache_v = cache_v.reshape(DEPTH, n_pool, PAGE_SIZE, W)
    pt_flat = page_table.reshape(-1)

    x = jnp.concatenate([x_prompt.reshape(mp, D_MODEL), x_sample.reshape(n_seq, D_MODEL)], axis=0)
    token_tm = _row_tile(x.shape[0], TOKEN_ROWS_TARGET)
    outs = {k: [] for k in ("k_p", "v_p", "k_s", "v_s", "gla_p", "gla_s", "wkv_p", "wkv_s", "shift_p", "shift_s")}
    for l in range(DEPTH):
        row2 = lambda v: v.reshape(1, -1)
        mu = rwkv_mu[l]
        rp = dict(
            mu_r=row2(mu[:W]), mu_k=row2(mu[W:2 * W]), mu_v=row2(mu[2 * W:3 * W]),
            mu_wa=row2(mu[3 * W:3 * W + LANES]), mu_gr=row2(mu[3 * W + LANES:]),
            w0=row2(rwkv_w0[l]), w_up=_pad_rows(rwkv_w_up[l], LANES).astype(bf16),
            a0=row2(rwkv_a0[l]),
            a_up=jnp.concatenate([jnp.zeros((RWKV_DECAY_RANK, W), f32), rwkv_a_up[l]], axis=0).astype(bf16),
            g_up=rwkv_g_up[l].astype(bf16), k_k=row2(rwkv_k_k[l]), k_a=row2(rwkv_k_a[l]),
            r_k=row2(rwkv_r_k[l]), ln_w=row2(rwkv_ln_w[l]), ln_b=row2(rwkv_ln_b[l]))
        up_pad = _pad_rows(gla_gk_up[l], LANES).astype(bf16)
        gk_bias = row2(gla_gk_bias[l])
        norm_w = row2(gla_norm_w[l])

        proj = in_proj(x, _pack_w_in(w_in[l]), _row_tile(x.shape[0], IN_PROJ_ROWS_TARGET), IN_PROJ_TN)

        y_gla_p, gla_p = gla_prompt(proj, up_pad, gk_bias, norm_w, n_batch, seq, MIXER_ROWS)
        y_rwkv_p, wkv_p = rwkv_prompt(proj, rp, n_batch, seq, MIXER_ROWS)
        k_rope, qb, kb, vb, sel = rope_prep(proj, cos_p, sin_p, n_batch, seq)
        y_moba_p = moba_prompt(qb, kb, vb, sel, n_batch, seq)

        y_gla_s, y_rwkv_s, q_s, k_s, gla_s, wkv_s = sample_step(
            proj, mp, n_seq, SAMPLE_SEQ_BLOCK, l, state_gla, state_wkv, state_shift, cos_s, sin_s,
            up_pad, gk_bias, norm_w, rp)
        v_s = proj[mp:, C_MV:C_MV + W]
        y_moba_s = moba_sample(pt_flat, q_s.reshape(n_seq, 1, W), k_s.reshape(n_seq, 1, W),
                               v_s.reshape(n_seq, 1, W), cache_k, cache_v, l, n_pages)

        y_gla = jnp.concatenate([y_gla_p, y_gla_s], axis=0)
        y_moba = jnp.concatenate([y_moba_p, y_moba_s.reshape(n_seq, W).astype(bf16)], axis=0)
        y_rwkv = jnp.concatenate([y_rwkv_p, y_rwkv_s], axis=0)

        x1 = merge_ln(x, proj, y_gla, y_moba, y_rwkv, w_branch[l].astype(bf16), b_gate[l], w_out[l].astype(bf16),
                      row2(ln1_g[l]), row2(ln1_b[l]), token_tm)
        x = mlp_ln(x1, w_up[l].astype(bf16), w_down[l].astype(bf16), row2(ln2_g[l]), row2(ln2_b[l]), token_tm)

        hd = (MOBA_HEADS, MOBA_HEAD_DIM)
        outs["k_p"].append(k_rope.reshape(n_batch, seq, *hd))
        outs["v_p"].append(proj[:mp, C_MV:C_MV + W].reshape(n_batch, seq, *hd))
        outs["k_s"].append(k_s.reshape(n_seq, 1, *hd))
        outs["v_s"].append(v_s.reshape(n_seq, 1, *hd))
        outs["gla_p"].append(gla_p)
        outs["gla_s"].append(gla_s)
        outs["wkv_p"].append(wkv_p)
        outs["wkv_s"].append(wkv_s)
        outs["shift_p"].append(proj[:mp].reshape(n_batch, seq, N_PACK)[:, seq - 1, C_PR:C_PR + RWKV_PROJ])
        outs["shift_s"].append(proj[mp:, C_PR:C_PR + RWKV_PROJ])

    st = lambda k: jnp.stack(outs[k])
    return (x[:mp].reshape(n_batch, seq, D_MODEL), x[mp:].reshape(n_seq, 1, D_MODEL),
            st("k_p"), st("v_p"), st("k_s"), st("v_s"), st("gla_p"), st("gla_s"),
            st("wkv_p"), st("wkv_s"), st("shift_p"), st("shift_s"))
```

```python
import functools

import jax
import jax.numpy as jnp
from jax import lax
from jax.experimental import pallas as pl
from jax.experimental.pallas import tpu as pltpu

f32, bf16 = jnp.float32, jnp.bfloat16

D_MODEL = 1024
DEPTH = 2
PAGE_SIZE = 128
BRANCH_WIDTH = D_MODEL // 2
N_BRANCH = 3
GLA_HEADS = 4
GLA_DV = BRANCH_WIDTH // GLA_HEADS
GLA_DK = GLA_DV // 2
GLA_GATE_RANK = 16
GLA_GATE_NORMALIZER = 16.0
GLA_CHUNK = 64
GLA_NORM_EPS = 1e-5
MOBA_HEAD_DIM = 64
MOBA_HEADS = BRANCH_WIDTH // MOBA_HEAD_DIM
MOBA_BLOCK = 256
MOBA_TOPK = 3
ROPE_THETA = 10000.0
RWKV_HEAD = 64
RWKV_HEADS = BRANCH_WIDTH // RWKV_HEAD
RWKV_DECAY_RANK = 64
RWKV_A_RANK = 64
RWKV_GATE_RANK = 128
RWKV_DECAY_SCALE = 0.606531
RWKV_GN_EPS = 64e-5
RWKV_PROJ = 3 * BRANCH_WIDTH + RWKV_DECAY_RANK + RWKV_A_RANK + RWKV_GATE_RANK
RWKV_CHUNK = 64
RWKV_INV_BLOCK = 16
D_FF = 4 * D_MODEL
ALPHA = (2 * DEPTH) ** 0.25
LN_EPS = 1e-5

LANES = 128
VMEM_LIMIT_BYTES = 56 * 1024 * 1024

W = BRANCH_WIDTH
C_GQ, C_GK, C_GV, C_GO = 0, 256, 512, 1024
C_MQ, C_MK, C_MV = 1536, 2048, 2560
C_PG = 3072
C_PR = 6144
C_R, C_K7, C_V7 = C_PR, C_PR + W, C_PR + 2 * W
C_WA = C_PR + 3 * W
C_GR = C_WA + LANES
C_GLOW = C_PR + RWKV_PROJ
N_PACK = C_GLOW + LANES

_O_GQ, _O_GK, _O_GV, _O_GLOW, _O_GO = 0, 256, 512, 1024, 1040
_O_MQ, _O_MK, _O_MV, _O_PR, _O_PG = 1552, 2064, 2576, 3088, 4880

NT_DIMS = (((1,), (1,)), ((), ()))
TN_DIMS = (((0,), (0,)), ((), ()))


def _params(sem):
    return pltpu.CompilerParams(dimension_semantics=sem, vmem_limit_bytes=VMEM_LIMIT_BYTES)


def _bdot(a, b):
    return jnp.dot(a.astype(bf16), b.astype(bf16), preferred_element_type=f32)


def _bdot_nt(a, b):
    return lax.dot_general(a.astype(bf16), b.astype(bf16), NT_DIMS, preferred_element_type=f32)


def _bdot_tn(a, b):
    return lax.dot_general(a.astype(bf16), b.astype(bf16), TN_DIMS, preferred_element_type=f32)


def _split2(x):
    hi = x.astype(bf16)
    lo = (x - hi.astype(f32)).astype(bf16)
    return hi, lo


def _dot_exact_rhs(x, m_bf16):
    hi, lo = _split2(x)
    return (jnp.dot(hi, m_bf16, preferred_element_type=f32)
            + jnp.dot(lo, m_bf16, preferred_element_type=f32))


def _dot_exact_lhs(m_bf16, x):
    hi, lo = _split2(x)
    return (jnp.dot(m_bf16, hi, preferred_element_type=f32)
            + jnp.dot(m_bf16, lo, preferred_element_type=f32))


def _dot3_nt(a, b):
    ah, al = _split2(a)
    bh, bl = _split2(b)
    d = lambda u, v: lax.dot_general(u, v, NT_DIMS, preferred_element_type=f32)
    return d(ah, bh) + d(al, bh) + d(ah, bl)


def _iota(shape, dim):
    return lax.broadcasted_iota(jnp.int32, shape, dim)


def _seg_matrix(n, seg):
    return (_iota((n, n), 0) // seg == _iota((n, n), 1) // seg).astype(bf16)


def _layer_norm(z, g, b):
    mu = jnp.mean(z, -1, keepdims=True)
    d = z - mu
    var = jnp.mean(d * d, -1, keepdims=True)
    return d * lax.rsqrt(var + LN_EPS) * g + b


def _row_to_col(row, n):
    eye = _iota((n, n), 0) == _iota((n, n), 1)
    return jnp.sum(jnp.where(eye, jnp.broadcast_to(row, (n, n)), 0.0), axis=1, keepdims=True)


def _set_row(ref, idx, row):
    cur = ref[...]
    ref[...] = jnp.where(_iota(cur.shape, 0) == idx, jnp.broadcast_to(row, cur.shape), cur)


def _col_to_row(col, n):
    eye = _iota((n, n), 0) == _iota((n, n), 1)
    return jnp.sum(jnp.where(eye, jnp.broadcast_to(col, (n, n)), 0.0), axis=0, keepdims=True)


def _inproj_body(x_ref, w_ref, o_ref, xb_ref):
    @pl.when(pl.program_id(1) == 0)
    def _():
        xb_ref[...] = x_ref[...].astype(bf16)

    o_ref[...] = jnp.dot(xb_ref[...], w_ref[...], preferred_element_type=f32)


def in_proj(x, w, tm, tn):
    m, k = x.shape
    n = w.shape[1]
    return pl.pallas_call(
        _inproj_body,
        grid=(m // tm, n // tn),
        in_specs=[pl.BlockSpec((tm, k), lambda i, j: (i, 0)),
                  pl.BlockSpec((k, tn), lambda i, j: (0, j))],
        out_specs=pl.BlockSpec((tm, tn), lambda i, j: (i, j)),
        out_shape=jax.ShapeDtypeStruct((m, n), f32),
        scratch_shapes=[pltpu.VMEM((tm, k), bf16)],
        compiler_params=_params(("arbitrary", "arbitrary")),
        name="in_proj",
    )(x, w)


def _merge_body(x_ref, pg0_ref, pg1_ref, pg2_ref, y0_ref, y1_ref, y2_ref, wb_ref, bg_ref, wo_ref,
                g_ref, b_ref, o_ref):
    merged = None
    for n, (pg_ref, y_ref) in enumerate(((pg0_ref, y0_ref), (pg1_ref, y1_ref), (pg2_ref, y2_ref))):
        gate = jax.nn.sigmoid(pg_ref[...] + bg_ref[n:n + 1, :])
        term = gate * jnp.dot(y_ref[...], wb_ref[n], preferred_element_type=f32)
        merged = term if merged is None else merged + term
    out = jnp.dot(merged.astype(bf16), wo_ref[...], preferred_element_type=f32)
    o_ref[...] = _layer_norm(ALPHA * x_ref[...] + out, g_ref[...], b_ref[...])


def merge_ln(x, proj, y_gla, y_moba, y_rwkv, w_branch, b_gate, w_out, g, b, tm):
    m = x.shape[0]
    row = lambda i: (i, 0)
    const2 = lambda i: (0, 0)
    pg_spec = lambda n: pl.BlockSpec((tm, D_MODEL), lambda i, n=n: (i, C_PG // D_MODEL + n))
    y_spec = pl.BlockSpec((tm, W), row)
    return pl.pallas_call(
        _merge_body,
        grid=(m // tm,),
        in_specs=[pl.BlockSpec((tm, D_MODEL), row), pg_spec(0), pg_spec(1), pg_spec(2),
                  y_spec, y_spec, y_spec,
                  pl.BlockSpec((N_BRANCH, W, D_MODEL), lambda i: (0, 0, 0)),
                  pl.BlockSpec((N_BRANCH, D_MODEL), const2),
                  pl.BlockSpec((D_MODEL, D_MODEL), const2),
                  pl.BlockSpec((1, D_MODEL), const2), pl.BlockSpec((1, D_MODEL), const2)],
        out_specs=pl.BlockSpec((tm, D_MODEL), row),
        out_shape=jax.ShapeDtypeStruct((m, D_MODEL), f32),
        compiler_params=_params(("arbitrary",)),
        name="merge_ln",
    )(x, proj, proj, proj, y_gla, y_moba, y_rwkv, w_branch, b_gate, w_out, g, b)


FF_CHUNK = 1024


def _mlp_body(x_ref, wu_ref, wd_ref, g_ref, b_ref, o_ref):
    x = x_ref[...]
    xb = x.astype(bf16)
    acc = None
    for c in range(D_FF // FF_CHUNK):
        h = jnp.dot(xb, wu_ref[:, c * FF_CHUNK:(c + 1) * FF_CHUNK], preferred_element_type=f32)
        h = jnp.square(jnp.maximum(h, 0.0)).astype(bf16)
        part = jnp.dot(h, wd_ref[c * FF_CHUNK:(c + 1) * FF_CHUNK, :], preferred_element_type=f32)
        acc = part if acc is None else acc + part
    o_ref[...] = _layer_norm(ALPHA * x + acc, g_ref[...], b_ref[...])


def mlp_ln(x, w_up, w_down, g, b, tm):
    m = x.shape[0]
    row = lambda i: (i, 0)
    const2 = lambda i: (0, 0)
    return pl.pallas_call(
        _mlp_body,
        grid=(m // tm,),
        in_specs=[pl.BlockSpec((tm, D_MODEL), row),
                  pl.BlockSpec((D_MODEL, D_FF), const2, pipeline_mode=pl.Buffered(1)),
                  pl.BlockSpec((D_FF, D_MODEL), const2, pipeline_mode=pl.Buffered(1)),
                  pl.BlockSpec((1, D_MODEL), const2), pl.BlockSpec((1, D_MODEL), const2)],
        out_specs=pl.BlockSpec((tm, D_MODEL), row),
        out_shape=jax.ShapeDtypeStruct((m, D_MODEL), f32),
        compiler_params=_params(("arbitrary",)),
        name="mlp_ln",
    )(x, w_up, w_down, g, b)


def _gla_log_gate(glow, up_ref, bias_ref):
    z = _bdot(glow, up_ref[...]) + bias_ref[...]
    return jax.nn.log_sigmoid(z) * (1.0 / GLA_GATE_NORMALIZER)


def _gla_out(o, norm_w, g_out):
    o = o * lax.rsqrt(jnp.mean(jnp.square(o), -1, keepdims=True) + GLA_NORM_EPS) * norm_w
    return o * jax.nn.silu(g_out)


def _gla_prompt_body(q_ref, k_ref, v_ref, go_ref, glow_ref, up_ref, bias_ref, nw_ref,
                     y_ref, s_out_ref, s_ref):
    t = pl.program_id(1)
    rows = q_ref.shape[0]
    c_sz = GLA_CHUNK

    @pl.when(t == 0)
    def _():
        s_ref[...] = jnp.zeros_like(s_ref)

    lg = _gla_log_gate(glow_ref[...], up_ref, bias_ref)
    tril = (_iota((c_sz, c_sz), 1) <= _iota((c_sz, c_sz), 0))
    tril_b = tril.astype(bf16)
    n_chunks = rows // c_sz
    probs = []
    for c in range(n_chunks):
        rs = slice(c * c_sz, (c + 1) * c_sz)
        g_cum = _dot_exact_lhs(tril_b, lg[rs])
        g_last = g_cum[c_sz - 1:c_sz, :]
        q_dec = q_ref[rs, :] * (GLA_DK ** -0.5) * jnp.exp(g_cum)
        k = k_ref[rs, :]
        k_inv = k * jnp.exp(-g_cum)
        k_tail = k * jnp.exp(g_last - g_cum)
        e_last = jnp.exp(g_last)
        for h in range(GLA_HEADS):
            ks = slice(h * GLA_DK, (h + 1) * GLA_DK)
            vs = slice(h * GLA_DV, (h + 1) * GLA_DV)
            probs.append((q_dec[:, ks], k_inv[:, ks], k_tail[:, ks], e_last[:, ks], v_ref[rs, vs], rs, vs))
    a = [jnp.where(tril, _bdot_nt(q[0], q[1]), 0.0) for q in probs]
    o_intra = [_bdot(x, q[4]) for x, q in zip(a, probs)]
    kv = [_bdot_tn(q[2], q[4]) for q in probs]
    e_col = [_row_to_col(q[3], GLA_DK) for q in probs]

    state = [s_ref[h] for h in range(GLA_HEADS)]
    for c in range(n_chunks):
        sl = slice(c * GLA_HEADS, (c + 1) * GLA_HEADS)
        o = [oi + _bdot(q[0], s) for oi, q, s in zip(o_intra[sl], probs[sl], state)]
        state = [s * e + x for s, e, x in zip(state, e_col[sl], kv[sl])]
        for oo, q in zip(o, probs[sl]):
            rs, vs = q[5], q[6]
            y_ref[rs, vs] = _gla_out(oo, nw_ref[...], go_ref[rs, vs]).astype(y_ref.dtype)
    for h in range(GLA_HEADS):
        s_ref[h] = state[h]
    s_out_ref[...] = s_ref[...]


def gla_prompt(proj, up_pad, bias, norm_w, n_batch, seq, rt):
    nt = seq // rt
    rowmap = lambda width, col: pl.BlockSpec((rt, width), lambda b, t: (b * nt + t, col // width))
    const2 = lambda b, t: (0, 0)
    return pl.pallas_call(
        _gla_prompt_body,
        grid=(n_batch, nt),
        in_specs=[rowmap(GLA_HEADS * GLA_DK, C_GQ), rowmap(GLA_HEADS * GLA_DK, C_GK), rowmap(W, C_GV),
                  rowmap(W, C_GO), rowmap(LANES, C_GLOW),
                  pl.BlockSpec((LANES, GLA_HEADS * GLA_DK), const2),
                  pl.BlockSpec((1, GLA_HEADS * GLA_DK), const2),
                  pl.BlockSpec((1, GLA_DV), const2)],
        out_specs=[pl.BlockSpec((rt, W), lambda b, t: (b * nt + t, 0)),
                   pl.BlockSpec((None, GLA_HEADS, GLA_DK, GLA_DV), lambda b, t: (b, 0, 0, 0))],
        out_shape=[jax.ShapeDtypeStruct((n_batch * seq, W), bf16),
                   jax.ShapeDtypeStruct((n_batch, GLA_HEADS, GLA_DK, GLA_DV), f32)],
        scratch_shapes=[pltpu.VMEM((GLA_HEADS, GLA_DK, GLA_DV), f32)],
        compiler_params=_params(("arbitrary", "arbitrary")),
        name="gla_prompt",
    )(proj, proj, proj, proj, proj, up_pad, bias, norm_w)


def _rwkv_pre(r_s, k_s, v_s, wa_s, gr_s, p):
    seg = _seg_matrix(W, RWKV_HEAD)
    logw = -RWKV_DECAY_SCALE * jax.nn.sigmoid(p["w0"][...] + _bdot(jnp.tanh(wa_s), p["w_up"][...]))
    a = jax.nn.sigmoid(p["a0"][...] + _bdot(wa_s, p["a_up"][...]))
    g7 = _bdot(jax.nn.sigmoid(gr_s), p["g_up"][...])
    kk = k_s * p["k_k"][...]
    kk = kk * lax.rsqrt(_dot_exact_rhs(jnp.square(kk), seg) + 1e-12)
    k7 = k_s * (1.0 + (a - 1.0) * p["k_a"][...])
    bonus = _dot_exact_rhs(r_s * k7 * p["r_k"][...], seg) * v_s
    return dict(r=r_s, logw=logw, k=k7, v=v_s, kk=kk, a=a, g7=g7, bonus=bonus)


def _rwkv_post(yw, bonus, g7, p):
    seg = _seg_matrix(W, RWKV_HEAD)
    inv_n = 1.0 / RWKV_HEAD
    mu = _dot_exact_rhs(yw, seg) * inv_n
    d = yw - mu
    var = _dot_exact_rhs(jnp.square(d), seg) * inv_n
    yn = d * lax.rsqrt(var + RWKV_GN_EPS) * p["ln_w"][...] + p["ln_b"][...]
    return (yn + bonus) * g7


def _inv_unit_lower(n_mats):
    c = n_mats[0].shape[0]
    row, col = _iota((c, c), 0), _iota((c, c), 1)
    eye = (row == col).astype(f32)
    blk = (row // RWKV_INV_BLOCK) == (col // RWKV_INV_BLOCK)
    d = [jnp.where(blk, n, 0.0) for n in n_mats]
    lo = [n - x for n, x in zip(n_mats, d)]
    d2 = [_bdot(x, x) for x in d]
    d4 = [_bdot(x, x) for x in d2]
    d8 = [_bdot(x, x) for x in d4]
    pa = [_bdot(eye - x, eye + y) for x, y in zip(d, d2)]
    pb = [_bdot(eye + x, eye + y) for x, y in zip(d4, d8)]
    d_inv = [_bdot(x, y) for x, y in zip(pa, pb)]
    e = [_bdot(x, y) for x, y in zip(d_inv, lo)]
    e2 = [_bdot(x, x) for x in e]
    f = [_bdot(eye - x, eye + y) for x, y in zip(e, e2)]
    return [_bdot(x, y) for x, y in zip(f, d_inv)]


def _rwkv_chunk_coeffs(probs):
    c, kd = probs[0][0].shape
    row, col = _iota((2 * c, 2 * c), 0), _iota((2 * c, 2 * c), 1)
    mask = (col % c < row % c) | ((row >= c) & (col % c == row % c))
    eye_k = _iota((kd, kd), 0) == _iota((kd, kd), 1)
    prod = [jnp.where(mask, _bdot_nt(jnp.concatenate([a_t, r_t], 0), jnp.concatenate([b_t, k_t], 0)), 0.0)
            for r_t, a_t, b_t, k_t, _, _, _, _ in probs]
    t_inv = _inv_unit_lower([p[:c, :c] for p in prod])
    av = [_bdot(p[:c, c:], q[6]) for p, q in zip(prod, probs)]
    x = [_bdot(ti, jnp.concatenate([q[1], a], 1)) for ti, q, a in zip(t_inv, probs, av)]
    px = [_bdot(p[c:, :c], xx) for p, xx in zip(prod, x)]
    prkv = [_bdot(p[c:, c:], q[6]) for p, q in zip(prod, probs)]
    xb = [_bdot_tn(xx, q[4]) for xx, q in zip(x, probs)]
    vk = [_bdot_tn(q[6], q[5]) for q in probs]
    out = []
    for q, pxx, pv, xbb, vkk in zip(probs, px, prkv, xb, vk):
        tm_t = jnp.where(eye_k, jnp.broadcast_to(q[7], (kd, kd)), 0.0) - xbb[:kd]
        out.append((q[0] - pxx[:, :kd], pv - pxx[:, kd:], tm_t, vkk - xbb[kd:]))
    return out


_RWKV_PARAM_NAMES = ("mu_r", "mu_k", "mu_v", "mu_wa", "mu_gr", "w0", "w_up", "a0", "a_up", "g_up",
                     "k_k", "k_a", "r_k", "ln_w", "ln_b")


def _rwkv_prompt_body(r_ref, k_ref, v_ref, wa_ref, gr_ref, *rest):
    np_ = len(_RWKV_PARAM_NAMES)
    p = dict(zip(_RWKV_PARAM_NAMES, rest[:np_]))
    y_ref, s_out_ref = rest[np_:np_ + 2]
    s_ref, prev_ref = rest[np_ + 2:]
    t = pl.program_id(1)
    rows = r_ref.shape[0]
    c_sz = RWKV_CHUNK

    @pl.when(t == 0)
    def _():
        s_ref[...] = jnp.zeros_like(s_ref)
        prev_ref[...] = jnp.zeros_like(prev_ref)

    def shifted(x_ref, mu_ref, c0):
        x = x_ref[...]
        width = x.shape[1]
        prev = jnp.where(_iota(x.shape, 0) == 0, prev_ref[0:1, c0:c0 + width], pltpu.roll(x, 1, 0))
        prev_ref[0:1, c0:c0 + width] = x[rows - 1:rows, :]
        return x + mu_ref[...] * (prev - x)

    pre = _rwkv_pre(shifted(r_ref, p["mu_r"], 0), shifted(k_ref, p["mu_k"], W), shifted(v_ref, p["mu_v"], 2 * W),
                    shifted(wa_ref, p["mu_wa"], 3 * W), shifted(gr_ref, p["mu_gr"], 3 * W + LANES), p)
    tril_b = (_iota((c_sz, c_sz), 1) <= _iota((c_sz, c_sz), 0)).astype(bf16)
    n_chunks = rows // c_sz
    b_all = pre["kk"] * pre["a"]

    probs = []
    for c in range(n_chunks):
        rs = slice(c * c_sz, (c + 1) * c_sz)
        lw = pre["logw"][rs]
        g_cum = _dot_exact_lhs(tril_b, lw)
        g_last = g_cum[c_sz - 1:c_sz, :]
        e_ng = jnp.exp(-g_cum)
        e_gl = jnp.exp(g_last - g_cum)
        kk, b, k, v = pre["kk"][rs], b_all[rs], pre["k"][rs], pre["v"][rs]
        r_t = pre["r"][rs] * jnp.exp(g_cum)
        a_t = kk * jnp.exp(g_cum - lw)
        b_t, k_t, b_g, k_g = b * e_ng, k * e_ng, b * e_gl, k * e_gl
        gam = jnp.exp(g_last)
        for h in range(RWKV_HEADS):
            hs = slice(h * RWKV_HEAD, (h + 1) * RWKV_HEAD)
            probs.append((r_t[:, hs], a_t[:, hs], b_t[:, hs], k_t[:, hs], b_g[:, hs], k_g[:, hs], v[:, hs],
                          gam[:, hs]))
    coeffs = _rwkv_chunk_coeffs(probs)

    state = [s_ref[h] for h in range(RWKV_HEADS)]
    y_rows = []
    for c in range(n_chunks):
        cf = coeffs[c * RWKV_HEADS:(c + 1) * RWKV_HEADS]
        y_rows.append(jnp.concatenate([_bdot_nt(q[0], s) + q[1] for q, s in zip(cf, state)], axis=1))
        state = [_bdot(s, q[2]) + q[3] for q, s in zip(cf, state)]
    for h in range(RWKV_HEADS):
        s_ref[h] = state[h]
    yw = jnp.concatenate(y_rows, axis=0)
    y_ref[...] = _rwkv_post(yw, pre["bonus"], pre["g7"], p).astype(y_ref.dtype)
    s_out_ref[...] = s_ref[...]


def _rwkv_param_specs(const_map):
    vec = lambda width: pl.BlockSpec((1, width), const_map)
    mat = lambda r: pl.BlockSpec((r, W), const_map)
    return [vec(W), vec(W), vec(W), vec(LANES), vec(LANES), vec(W), mat(LANES), vec(W), mat(LANES),
            mat(RWKV_GATE_RANK), vec(W), vec(W), vec(W), vec(W), vec(W)]


def rwkv_prompt(proj, params, n_batch, seq, rt):
    nt = seq // rt
    rowmap = lambda width, col: pl.BlockSpec((rt, width), lambda b, t: (b * nt + t, col // width))
    const2 = lambda b, t: (0, 0)
    return pl.pallas_call(
        _rwkv_prompt_body,
        grid=(n_batch, nt),
        in_specs=[rowmap(W, C_R), rowmap(W, C_K7), rowmap(W, C_V7), rowmap(LANES, C_WA), rowmap(LANES, C_GR)]
        + _rwkv_param_specs(const2),
        out_specs=[pl.BlockSpec((rt, W), lambda b, t: (b * nt + t, 0)),
                   pl.BlockSpec((None, RWKV_HEADS, RWKV_HEAD, RWKV_HEAD), lambda b, t: (b, 0, 0, 0))],
        out_shape=[jax.ShapeDtypeStruct((n_batch * seq, W), bf16),
                   jax.ShapeDtypeStruct((n_batch, RWKV_HEADS, RWKV_HEAD, RWKV_HEAD), f32)],
        scratch_shapes=[pltpu.VMEM((RWKV_HEADS, RWKV_HEAD, RWKV_HEAD), f32), pltpu.VMEM((8, RWKV_PROJ), f32)],
        compiler_params=_params(("arbitrary", "arbitrary")),
        name="rwkv_prompt",
    )(proj, proj, proj, proj, proj, *[params[n] for n in _RWKV_PARAM_NAMES])


def _rope(x, cos, sin_signed):
    lane = _iota(x.shape, 1)
    half = MOBA_HEAD_DIM // 2
    width = x.shape[1]
    swapped = jnp.where(lane % MOBA_HEAD_DIM < half, pltpu.roll(x, width - half, 1), pltpu.roll(x, half, 1))
    return x * cos + swapped * sin_signed


def _sample_body(gq_ref, gk_ref, gv_ref, go_ref, glow_ref, r_ref, k_ref, v_ref, wa_ref, gr_ref, mq_ref, mk_ref,
                 sg_ref, sw_ref, shift_ref, cos_ref, sin_ref, up_ref, bias_ref, nw_ref, *rest):
    np_ = len(_RWKV_PARAM_NAMES)
    p = dict(zip(_RWKV_PARAM_NAMES, rest[:np_]))
    yg_ref, yr_ref, qs_ref, ks_ref, sg_out, sw_out = rest[np_:np_ + 6]
    q_s, kg_s, eg_s, r_s, w_s, k_s, v_s, kk_s, b_s, og_s, yw_s = rest[np_ + 6:]
    n_seq = gq_ref.shape[0]

    qs_ref[...] = _rope(mq_ref[...], cos_ref[...], sin_ref[...]) * (MOBA_HEAD_DIM ** -0.5)
    ks_ref[...] = _rope(mk_ref[...], cos_ref[...], sin_ref[...])

    lg = _gla_log_gate(glow_ref[...], up_ref, bias_ref)
    q_s[...] = gq_ref[...] * (GLA_DK ** -0.5)
    kg_s[...] = gk_ref[...]
    eg_s[...] = jnp.exp(lg)

    def shifted(x_ref, mu_ref, c0):
        x = x_ref[...]
        return x + mu_ref[...] * (shift_ref[:, c0:c0 + x.shape[1]] - x)

    pre = _rwkv_pre(shifted(r_ref, p["mu_r"], 0), shifted(k_ref, p["mu_k"], W), shifted(v_ref, p["mu_v"], 2 * W),
                    shifted(wa_ref, p["mu_wa"], 3 * W), shifted(gr_ref, p["mu_gr"], 3 * W + LANES), p)
    r_s[...] = pre["r"]
    w_s[...] = jnp.exp(pre["logw"])
    k_s[...] = pre["k"]
    v_s[...] = pre["v"]
    kk_s[...] = pre["kk"]
    b_s[...] = pre["kk"] * pre["a"]

    def one_seq(s, carry):
        row = pl.ds(s, 1)
        eg_row, kg_row, q_row, gv_row = eg_s[row, :], kg_s[row, :], q_s[row, :], gv_ref[row, :]
        o_parts = []
        for h in range(GLA_HEADS):
            ks = slice(h * GLA_DK, (h + 1) * GLA_DK)
            vs = slice(h * GLA_DV, (h + 1) * GLA_DV)
            st = sg_ref[s, h]
            st = (st * _row_to_col(eg_row[:, ks], GLA_DK)
                  + _row_to_col(kg_row[:, ks], GLA_DK) * gv_row[:, vs])
            sg_out[s, h] = st
            o_parts.append(jnp.sum(_row_to_col(q_row[:, ks], GLA_DK) * st, axis=0, keepdims=True))
        _set_row(og_s, s, jnp.concatenate(o_parts, axis=1))
        kk_row, w_row, b_row, v_row, k_row, r_row = (kk_s[row, :], w_s[row, :], b_s[row, :], v_s[row, :],
                                                     k_s[row, :], r_s[row, :])
        y_parts = []
        for h in range(RWKV_HEADS):
            hs = slice(h * RWKV_HEAD, (h + 1) * RWKV_HEAD)
            st = sw_ref[s, h]
            sa = jnp.sum(st * (-kk_row[:, hs]), axis=1, keepdims=True)
            st = st * w_row[:, hs] + sa * b_row[:, hs] + _row_to_col(v_row[:, hs], RWKV_HEAD) * k_row[:, hs]
            sw_out[s, h] = st
            y_col = jnp.sum(st * r_row[:, hs], axis=1, keepdims=True)
            y_parts.append(_col_to_row(y_col, RWKV_HEAD))
        _set_row(yw_s, s, jnp.concatenate(y_parts, axis=1))
        return carry

    lax.fori_loop(0, n_seq, one_seq, 0)

    for h in range(GLA_HEADS):
        vs = slice(h * GLA_DV, (h + 1) * GLA_DV)
        yg_ref[:, vs] = _gla_out(og_s[:, vs], nw_ref[...], go_ref[:, vs]).astype(yg_ref.dtype)
    yr_ref[...] = _rwkv_post(yw_s[...], pre["bonus"], pre["g7"], p).astype(yr_ref.dtype)


def sample_step(proj, row0, n_seq, sb, layer, state_gla, state_wkv, state_shift, cos_s, sin_s,
                up_pad, bias, norm_w, params):
    r0 = row0 // sb
    rowmap = lambda width, col: pl.BlockSpec((sb, width), lambda s: (r0 + s, col // width))
    const2 = lambda s: (0, 0)
    hk = GLA_HEADS * GLA_DK
    out_row = lambda width: pl.BlockSpec((sb, width), lambda s: (s, 0))
    scr = lambda width: pltpu.VMEM((sb, width), f32)
    return pl.pallas_call(
        _sample_body,
        grid=(n_seq // sb,),
        in_specs=[rowmap(hk, C_GQ), rowmap(hk, C_GK), rowmap(W, C_GV), rowmap(W, C_GO), rowmap(LANES, C_GLOW),
                  rowmap(W, C_R), rowmap(W, C_K7), rowmap(W, C_V7), rowmap(LANES, C_WA), rowmap(LANES, C_GR),
                  rowmap(W, C_MQ), rowmap(W, C_MK),
                  pl.BlockSpec((None, sb, GLA_HEADS, GLA_DK, GLA_DV), lambda s: (layer, s, 0, 0, 0)),
                  pl.BlockSpec((None, sb, RWKV_HEADS, RWKV_HEAD, RWKV_HEAD), lambda s: (layer, s, 0, 0, 0)),
                  pl.BlockSpec((None, sb, RWKV_PROJ), lambda s: (layer, s, 0)),
                  pl.BlockSpec((1, W), const2), pl.BlockSpec((1, W), const2),
                  pl.BlockSpec((LANES, hk), const2), pl.BlockSpec((1, hk), const2), pl.BlockSpec((1, GLA_DV), const2)]
        + _rwkv_param_specs(const2),
        out_specs=[out_row(W), out_row(W), out_row(W), out_row(W),
                   pl.BlockSpec((sb, GLA_HEADS, GLA_DK, GLA_DV), lambda s: (s, 0, 0, 0)),
                   pl.BlockSpec((sb, RWKV_HEADS, RWKV_HEAD, RWKV_HEAD), lambda s: (s, 0, 0, 0))],
        out_shape=[jax.ShapeDtypeStruct((n_seq, W), bf16), jax.ShapeDtypeStruct((n_seq, W), bf16),
                   jax.ShapeDtypeStruct((n_seq, W), f32), jax.ShapeDtypeStruct((n_seq, W), f32),
                   jax.ShapeDtypeStruct((n_seq, GLA_HEADS, GLA_DK, GLA_DV), f32),
                   jax.ShapeDtypeStruct((n_seq, RWKV_HEADS, RWKV_HEAD, RWKV_HEAD), f32)],
        scratch_shapes=[scr(hk), scr(hk), scr(hk), scr(W), scr(W), scr(W), scr(W), scr(W), scr(W), scr(W), scr(W)],
        compiler_params=_params(("arbitrary",)),
        name="sample_step",
    )(proj, proj, proj, proj, proj, proj, proj, proj, proj, proj, proj, proj,
      state_gla, state_wkv, state_shift, cos_s, sin_s, up_pad, bias, norm_w,
      *[params[n] for n in _RWKV_PARAM_NAMES])


def _topk_select(gate, n_valid, axis_len, axis):
    idx = _iota(gate.shape, axis)
    cnt = jnp.zeros(gate.shape, f32)
    for m in range(axis_len):
        g_m = lax.slice_in_dim(gate, m, m + 1, axis=axis)
        g_m = jnp.broadcast_to(g_m, gate.shape)
        beats = (g_m > gate) | ((g_m == gate) & (m < idx))
        cnt = cnt + jnp.where(beats, 1.0, 0.0) * jnp.where(m < n_valid, 1.0, 0.0)
    return (idx < n_valid) & (cnt < MOBA_TOPK)


def _rope_prep_body(mq_ref, mk_ref, mv_ref, cos_ref, sin_ref, k_out, qb_out, kb_out, vb_out, sel_out, km_ref):
    i = pl.program_id(1)
    blk = mq_ref.shape[0]
    nb = km_ref.shape[0]
    nh = MOBA_HEADS

    @pl.when(i == 0)
    def _():
        km_ref[...] = jnp.zeros_like(km_ref)

    q = _rope(mq_ref[...], cos_ref[...], sin_ref[...]) * (MOBA_HEAD_DIM ** -0.5)
    k = _rope(mk_ref[...], cos_ref[...], sin_ref[...])
    k_out[...] = k
    qb_out[...] = q.astype(bf16)
    kb_out[...] = k.astype(bf16)
    vb_out[...] = mv_ref[...].astype(bf16)

    km = km_ref[...]
    km_rows = jnp.concatenate([km] * nh, axis=0)
    head_of_row = _iota(km_rows.shape, 0) // nb
    km_rows = jnp.where(head_of_row == _iota(km_rows.shape, 1) // MOBA_HEAD_DIM, km_rows, 0.0)
    gate = _dot3_nt(km_rows, q).reshape(nh, nb, blk)
    sel = _topk_select(gate, i, nb, 1).astype(bf16).reshape(nh * nb, blk)
    sel = jnp.concatenate([sel, jnp.zeros((LANES - nh * nb, blk), bf16)], axis=0)
    eye = (_iota((blk, blk), 0) == _iota((blk, blk), 1)).astype(bf16)
    sel_out[...] = lax.dot_general(eye, sel, NT_DIMS, preferred_element_type=f32)

    _set_row(km_ref, i, jnp.sum(k, axis=0, keepdims=True) * (1.0 / blk))


def rope_prep(proj, cos, sin, n_batch, seq):
    blk = MOBA_BLOCK
    nb = seq // blk
    m = n_batch * seq
    rowmap = lambda col: pl.BlockSpec((blk, W), lambda b, i: (b * nb + i, col // W))
    tab = pl.BlockSpec((blk, W), lambda b, i: (i, 0))
    out = lambda width: pl.BlockSpec((blk, width), lambda b, i: (b * nb + i, 0))
    return pl.pallas_call(
        _rope_prep_body,
        grid=(n_batch, nb),
        in_specs=[rowmap(C_MQ), rowmap(C_MK), rowmap(C_MV), tab, tab],
        out_specs=[out(W), out(W), out(W), out(W), out(LANES)],
        out_shape=[jax.ShapeDtypeStruct((m, W), f32), jax.ShapeDtypeStruct((m, W), bf16),
                   jax.ShapeDtypeStruct((m, W), bf16), jax.ShapeDtypeStruct((m, W), bf16),
                   jax.ShapeDtypeStruct((m, LANES), f32)],
        scratch_shapes=[pltpu.VMEM((nb, W), f32)],
        compiler_params=_params(("arbitrary", "arbitrary")),
        name="rope_prep",
    )(proj, proj, proj, cos, sin)


def _moba_prompt_body(qb_ref, kb_ref, vb_ref, sel_ref, y_ref, m_ref, l_ref, acc_ref):
    i = pl.program_id(1)
    blk = qb_ref.shape[0]
    nb = kb_ref.shape[0] // blk
    lane = _iota((blk, LANES), 1)
    low_half = lane < MOBA_HEAD_DIM
    causal = _iota((blk, blk), 1) <= _iota((blk, blk), 0)
    neg_inf = -jnp.inf

    def head_q(pair, u):
        qp = qb_ref[:, pair * LANES:(pair + 1) * LANES]
        return jnp.where(low_half if u == 0 else ~low_half, qp, jnp.zeros_like(qp))

    def kv(n, pair):
        rs = pl.ds(pl.multiple_of(n * blk, blk), blk)
        ls = slice(pair * LANES, (pair + 1) * LANES)
        return kb_ref[rs, ls], vb_ref[rs, ls]

    wide = lambda x: jnp.concatenate([x, x], axis=1)

    for pair in range(MOBA_HEADS // 2):
        k_n, v_n = kv(i, pair)
        for u in range(2):
            h = 2 * pair + u
            s = jnp.where(causal, lax.dot_general(head_q(pair, u), k_n, NT_DIMS, preferred_element_type=f32), neg_inf)
            m = jnp.broadcast_to(jnp.max(s, axis=1, keepdims=True), (blk, LANES))
            p = jnp.exp(s - wide(m))
            m_ref[h] = m
            l_ref[h] = jnp.broadcast_to(jnp.sum(p, axis=1, keepdims=True), (blk, LANES))
            acc_ref[h] = jnp.dot(p.astype(bf16), v_n, preferred_element_type=f32)

    def past(n, carry):
        sel = sel_ref[...]
        for pair in range(MOBA_HEADS // 2):
            k_n, v_n = kv(n, pair)
            for u in range(2):
                h = 2 * pair + u
                chosen = jnp.sum(jnp.where(lane == h * nb + n, sel, 0.0), axis=1, keepdims=True)
                s = lax.dot_general(head_q(pair, u), k_n, NT_DIMS, preferred_element_type=f32)
                s = jnp.where(jnp.broadcast_to(chosen, s.shape) > 0.5, s, neg_inf)
                m_old = m_ref[h]
                m_new = jnp.maximum(m_old, jnp.broadcast_to(jnp.max(s, axis=1, keepdims=True), (blk, LANES)))
                alpha = jnp.exp(m_old - m_new)
                p = jnp.exp(s - wide(m_new))
                m_ref[h] = m_new
                l_ref[h] = alpha * l_ref[h] + jnp.broadcast_to(jnp.sum(p, axis=1, keepdims=True), (blk, LANES))
                acc_ref[h] = alpha * acc_ref[h] + jnp.dot(p.astype(bf16), v_n, preferred_element_type=f32)
        return carry

    lax.fori_loop(0, i, past, 0)

    for pair in range(MOBA_HEADS // 2):
        o0 = acc_ref[2 * pair] / l_ref[2 * pair]
        o1 = acc_ref[2 * pair + 1] / l_ref[2 * pair + 1]
        y_ref[:, pair * LANES:(pair + 1) * LANES] = jnp.where(low_half, o0, o1).astype(y_ref.dtype)


def moba_prompt(qb, kb, vb, sel, n_batch, seq):
    blk = MOBA_BLOCK
    nb = seq // blk
    row = lambda width: pl.BlockSpec((blk, width), lambda b, i: (b * nb + i, 0))
    full = pl.BlockSpec((seq, W), lambda b, i: (b, 0))
    scr = lambda: pltpu.VMEM((MOBA_HEADS, blk, LANES), f32)
    return pl.pallas_call(
        _moba_prompt_body,
        grid=(n_batch, nb),
        in_specs=[row(W), full, full, row(LANES)],
        out_specs=row(W),
        out_shape=jax.ShapeDtypeStruct((n_batch * seq, W), bf16),
        scratch_shapes=[scr(), scr(), scr()],
        compiler_params=_params(("arbitrary", "arbitrary")),
        name="moba_prompt",
    )(qb, kb, vb, sel)


SAMPLE_PAGES_PER_STEP = 8
SAMPLE_HEAD_ROWS = 16


def _moba_sample_body(pt_ref, q_ref, k_ref, v_ref, *rest):
    pps = SAMPLE_PAGES_PER_STEP
    k_pages, v_pages = rest[:pps], rest[pps:2 * pps]
    y_ref = rest[2 * pps]
    km_ref, m_ref, l_ref, acc_ref = rest[2 * pps + 1:]
    j = pl.program_id(1)
    n_steps = pl.num_programs(1)
    hr = SAMPLE_HEAD_ROWS
    nh, dh = MOBA_HEADS, MOBA_HEAD_DIM
    pages_per_block = MOBA_BLOCK // PAGE_SIZE
    blocks_per_step = pps // pages_per_block
    rows_blk = MOBA_BLOCK * nh
    lane = _iota((hr, LANES), 1)

    @pl.when(j == 0)
    def _():
        m_ref[...] = jnp.zeros_like(m_ref)
        l_ref[...] = jnp.zeros_like(l_ref)

    pad_heads = lambda x: jnp.concatenate([x, jnp.zeros((hr - nh, dh), f32)], axis=0)
    q_heads = pad_heads(q_ref[...])
    q_heads_b = q_heads.astype(bf16)
    own_head = _iota((hr, rows_blk), 1) % nh == _iota((hr, rows_blk), 0) % nh

    for u in range(blocks_per_step):
        n = j * blocks_per_step + u
        k_pg = [k_pages[u * pages_per_block + t][...] for t in range(pages_per_block)]
        v_pg = [v_pages[u * pages_per_block + t][...] for t in range(pages_per_block)]
        km_ref[n] = sum(jnp.sum(x, axis=0) for x in k_pg) * (1.0 / MOBA_BLOCK)
        k_blk = jnp.concatenate([x.reshape(PAGE_SIZE * nh, dh) for x in k_pg], axis=0).astype(bf16)
        v_blk = jnp.concatenate([x.reshape(PAGE_SIZE * nh, dh) for x in v_pg], axis=0).astype(bf16)
        s = lax.dot_general(q_heads_b, k_blk, NT_DIMS, preferred_element_type=f32)
        s = jnp.where(own_head, s, -jnp.inf)
        m = jnp.max(s, axis=1, keepdims=True)
        p = jnp.exp(s - m)
        m_ref[...] = jnp.where(lane == n, m, m_ref[...])
        l_ref[...] = jnp.where(lane == n, jnp.sum(p, axis=1, keepdims=True), l_ref[...])
        acc_ref[n] = jnp.dot(p.astype(bf16), v_blk, preferred_element_type=f32)

    @pl.when(j == n_steps - 1)
    def _():
        nb = n_steps * blocks_per_step
        gate = jnp.zeros((hr, LANES), f32)
        for n in range(nb):
            g_n = jnp.sum(q_heads * pad_heads(km_ref[n]), axis=1, keepdims=True)
            gate = jnp.where(lane == n, g_n, gate)
        sel = _topk_select(gate, nb, nb, 1)
        s_own = jnp.sum(q_heads * pad_heads(k_ref[...]), axis=1, keepdims=True)
        m_all = m_ref[...]
        m_tot = jnp.maximum(jnp.max(jnp.where(sel, m_all, -jnp.inf), axis=1, keepdims=True), s_own)
        wts = jnp.where(sel, jnp.exp(m_all - m_tot), 0.0)
        w_own = jnp.exp(s_own - m_tot)
        l_tot = jnp.sum(wts * l_ref[...], axis=1, keepdims=True) + w_own
        out = w_own * pad_heads(v_ref[...])
        for n in range(nb):
            out = out + wts[:, n:n + 1] * acc_ref[n]
        y_ref[...] = (out / l_tot)[:nh]


def moba_sample(page_table_flat, q_s, k_s, v_s, cache_k, cache_v, layer, n_pages):
    n_seq = q_s.shape[0]
    nh, dh = MOBA_HEADS, MOBA_HEAD_DIM
    pps = SAMPLE_PAGES_PER_STEP
    n_steps = n_pages // pps
    nb = n_pages * PAGE_SIZE // MOBA_BLOCK
    row = pl.BlockSpec((None, nh, dh), lambda s, j, pt: (s, 0, 0))
    page = lambda u: pl.BlockSpec((None, None, PAGE_SIZE, nh, dh),
                                  lambda s, j, pt, u=u: (layer, pt[s * n_pages + j * pps + u], 0, 0, 0))
    grid_spec = pltpu.PrefetchScalarGridSpec(
        num_scalar_prefetch=1,
        grid=(n_seq, n_steps),
        in_specs=[row, row, row] + [page(u) for u in range(pps)] + [page(u) for u in range(pps)],
        out_specs=row,
        scratch_shapes=[pltpu.VMEM((nb, nh, dh), f32), pltpu.VMEM((SAMPLE_HEAD_ROWS, LANES), f32),
                        pltpu.VMEM((SAMPLE_HEAD_ROWS, LANES), f32), pltpu.VMEM((nb, SAMPLE_HEAD_ROWS, dh), f32)],
    )
    return pl.pallas_call(
        _moba_sample_body,
        grid_spec=grid_spec,
        out_shape=jax.ShapeDtypeStruct((n_seq, nh, dh), f32),
        compiler_params=_params(("arbitrary", "arbitrary")),
        name="moba_sample",
    )(page_table_flat, q_s, k_s, v_s, *([cache_k] * pps), *([cache_v] * pps))


def _rope_tables(pos):
    half = MOBA_HEAD_DIM // 2
    inv = ROPE_THETA ** (-jnp.arange(half, dtype=f32) / half)
    ang = pos.astype(f32)[:, None] * inv[None, :]
    cos, sin = jnp.cos(ang), jnp.sin(ang)
    cos_h = jnp.concatenate([cos, cos], -1)
    sin_h = jnp.concatenate([-sin, sin], -1)
    return jnp.tile(cos_h, (1, MOBA_HEADS)), jnp.tile(sin_h, (1, MOBA_HEADS))


def _pack_w_in(w):
    cols = lambda o, n: w[:, o:o + n]
    hk = GLA_HEADS * GLA_DK
    parts = [cols(_O_GQ, hk), cols(_O_GK, hk), cols(_O_GV, W), cols(_O_GO, W),
             cols(_O_MQ, W), cols(_O_MK, W), cols(_O_MV, W), cols(_O_PG, N_BRANCH * D_MODEL),
             cols(_O_PR, RWKV_PROJ), cols(_O_GLOW, GLA_GATE_RANK),
             jnp.zeros((w.shape[0], LANES - GLA_GATE_RANK), w.dtype)]
    return jnp.concatenate(parts, axis=1).astype(bf16)


def _pad_rows(m, rows):
    return jnp.concatenate([m, jnp.zeros((rows - m.shape[0], m.shape[1]), m.dtype)], axis=0)


SUBLANES = 8
IN_PROJ_ROWS_TARGET, IN_PROJ_TN = 1100, 896
TOKEN_ROWS_TARGET = 700
MIXER_ROWS = 256
SAMPLE_SEQ_BLOCK = 16


def _row_tile(m, target):
    best = SUBLANES
    for t in range(SUBLANES, target + 1, SUBLANES):
        if m % t == 0:
            best = t
    assert m % best == 0
    return best


def kernel(x_prompt, x_sample, cache_k, cache_v, page_table, state_gla, state_wkv, state_shift, w_in, b_gate, gla_gk_up, gla_gk_bias, gla_norm_w, rwkv_mu, rwkv_w0, rwkv_w_up, rwkv_a0, rwkv_a_up, rwkv_g_up, rwkv_k_k, rwkv_k_a, rwkv_r_k, rwkv_ln_w, rwkv_ln_b, w_branch, w_out, ln1_g, ln1_b, w_up, w_down, ln2_g, ln2_b):
    n_batch, seq, _ = x_prompt.shape
    n_seq = x_sample.shape[0]
    n_pages = page_table.shape[1]
    mp = n_batch * seq
    past = n_pages * PAGE_SIZE

    cos_p, sin_p = _rope_tables(jnp.arange(seq, dtype=jnp.int32))
    cos_s, sin_s = _rope_tables(jnp.full((1,), past, jnp.int32))
    pt_flat = page_table.reshape(-1)

    x = jnp.concatenate([x_prompt.reshape(mp, D_MODEL), x_sample.reshape(n_seq, D_MODEL)], axis=0)
    token_tm = _row_tile(x.shape[0], TOKEN_ROWS_TARGET)
    outs = {k: [] for k in ("k_p", "v_p", "k_s", "v_s", "gla_p", "gla_s", "wkv_p", "wkv_s", "shift_p", "shift_s")}
    for l in range(DEPTH):
        row2 = lambda v: v.reshape(1, -1)
        mu = rwkv_mu[l]
        rp = dict(
            mu_r=row2(mu[:W]), mu_k=row2(mu[W:2 * W]), mu_v=row2(mu[2 * W:3 * W]),
            mu_wa=row2(mu[3 * W:3 * W + LANES]), mu_gr=row2(mu[3 * W + LANES:]),
            w0=row2(rwkv_w0[l]), w_up=_pad_rows(rwkv_w_up[l], LANES).astype(bf16),
            a0=row2(rwkv_a0[l]),
            a_up=jnp.concatenate([jnp.zeros((RWKV_DECAY_RANK, W), f32), rwkv_a_up[l]], axis=0).astype(bf16),
            g_up=rwkv_g_up[l].astype(bf16), k_k=row2(rwkv_k_k[l]), k_a=row2(rwkv_k_a[l]),
            r_k=row2(rwkv_r_k[l]), ln_w=row2(rwkv_ln_w[l]), ln_b=row2(rwkv_ln_b[l]))
        up_pad = _pad_rows(gla_gk_up[l], LANES).astype(bf16)
        gk_bias = row2(gla_gk_bias[l])
        norm_w = row2(gla_norm_w[l])

        proj = in_proj(x, _pack_w_in(w_in[l]), _row_tile(x.shape[0], IN_PROJ_ROWS_TARGET), IN_PROJ_TN)

        y_gla_p, gla_p = gla_prompt(proj, up_pad, gk_bias, norm_w, n_batch, seq, MIXER_ROWS)
        y_rwkv_p, wkv_p = rwkv_prompt(proj, rp, n_batch, seq, MIXER_ROWS)
        k_rope, qb, kb, vb, sel = rope_prep(proj, cos_p, sin_p, n_batch, seq)
        y_moba_p = moba_prompt(qb, kb, vb, sel, n_batch, seq)

        y_gla_s, y_rwkv_s, q_s, k_s, gla_s, wkv_s = sample_step(
            proj, mp, n_seq, SAMPLE_SEQ_BLOCK, l, state_gla, state_wkv, state_shift, cos_s, sin_s,
            up_pad, gk_bias, norm_w, rp)
        v_s = proj[mp:, C_MV:C_MV + W]
        heads = (n_seq, MOBA_HEADS, MOBA_HEAD_DIM)
        y_moba_s = moba_sample(pt_flat, q_s.reshape(heads), k_s.reshape(heads), v_s.reshape(heads),
                               cache_k, cache_v, l, n_pages)

        y_gla = jnp.concatenate([y_gla_p, y_gla_s], axis=0)
        y_moba = jnp.concatenate([y_moba_p, y_moba_s.reshape(n_seq, W).astype(bf16)], axis=0)
        y_rwkv = jnp.concatenate([y_rwkv_p, y_rwkv_s], axis=0)

        x1 = merge_ln(x, proj, y_gla, y_moba, y_rwkv, w_branch[l].astype(bf16), b_gate[l], w_out[l].astype(bf16),
                      row2(ln1_g[l]), row2(ln1_b[l]), token_tm)
        x = mlp_ln(x1, w_up[l].astype(bf16), w_down[l].astype(bf16), row2(ln2_g[l]), row2(ln2_b[l]), token_tm)

        hd = (MOBA_HEADS, MOBA_HEAD_DIM)
        outs["k_p"].append(k_rope.reshape(n_batch, seq, *hd))
        outs["v_p"].append(proj[:mp, C_MV:C_MV + W].reshape(n_batch, seq, *hd))
        outs["k_s"].append(k_s.reshape(n_seq, 1, *hd))
        outs["v_s"].append(v_s.reshape(n_seq, 1, *hd))
        outs["gla_p"].append(gla_p)
        outs["gla_s"].append(gla_s)
        outs["wkv_p"].append(wkv_p)
        outs["wkv_s"].append(wkv_s)
        outs["shift_p"].append(proj[seq - 1:mp:seq, C_PR:C_PR + RWKV_PROJ])
        outs["shift_s"].append(proj[mp:, C_PR:C_PR + RWKV_PROJ])

    st = lambda k: jnp.stack(outs[k])
    return (x[:mp].reshape(n_batch, seq, D_MODEL), x[mp:].reshape(n_seq, 1, D_MODEL),
            st("k_p"), st("v_p"), st("k_s"), st("v_s"), st("gla_p"), st("gla_s"),
            st("wkv_p"), st("wkv_s"), st("shift_p"), st("shift_s"))
```

```python
import functools

import jax
import jax.numpy as jnp
from jax import lax
from jax.experimental import pallas as pl
from jax.experimental.pallas import tpu as pltpu

f32, bf16 = jnp.float32, jnp.bfloat16

D_MODEL = 1024
DEPTH = 2
PAGE_SIZE = 128
BRANCH_WIDTH = D_MODEL // 2
N_BRANCH = 3
GLA_HEADS = 4
GLA_DV = BRANCH_WIDTH // GLA_HEADS
GLA_DK = GLA_DV // 2
GLA_GATE_RANK = 16
GLA_GATE_NORMALIZER = 16.0
GLA_CHUNK = 64
GLA_NORM_EPS = 1e-5
MOBA_HEAD_DIM = 64
MOBA_HEADS = BRANCH_WIDTH // MOBA_HEAD_DIM
MOBA_BLOCK = 256
MOBA_TOPK = 3
ROPE_THETA = 10000.0
RWKV_HEAD = 64
RWKV_HEADS = BRANCH_WIDTH // RWKV_HEAD
RWKV_DECAY_RANK = 64
RWKV_A_RANK = 64
RWKV_GATE_RANK = 128
RWKV_DECAY_SCALE = 0.606531
RWKV_GN_EPS = 64e-5
RWKV_PROJ = 3 * BRANCH_WIDTH + RWKV_DECAY_RANK + RWKV_A_RANK + RWKV_GATE_RANK
RWKV_CHUNK = 64
RWKV_INV_BLOCK = 16
D_FF = 4 * D_MODEL
ALPHA = (2 * DEPTH) ** 0.25
LN_EPS = 1e-5

LANES = 128
VMEM_LIMIT_BYTES = 56 * 1024 * 1024

W = BRANCH_WIDTH
C_GQ, C_GK, C_GV, C_GO = 0, 256, 512, 1024
C_MQ, C_MK, C_MV = 1536, 2048, 2560
C_PG = 3072
C_PR = 6144
C_R, C_K7, C_V7 = C_PR, C_PR + W, C_PR + 2 * W
C_WA = C_PR + 3 * W
C_GR = C_WA + LANES
C_GLOW = C_PR + RWKV_PROJ
N_PACK = C_GLOW + LANES

_O_GQ, _O_GK, _O_GV, _O_GLOW, _O_GO = 0, 256, 512, 1024, 1040
_O_MQ, _O_MK, _O_MV, _O_PR, _O_PG = 1552, 2064, 2576, 3088, 4880

NT_DIMS = (((1,), (1,)), ((), ()))
TN_DIMS = (((0,), (0,)), ((), ()))


def _params(sem):
    return pltpu.CompilerParams(dimension_semantics=sem, vmem_limit_bytes=VMEM_LIMIT_BYTES)


def _bdot(a, b):
    return jnp.dot(a.astype(bf16), b.astype(bf16), preferred_element_type=f32)


def _bdot_nt(a, b):
    return lax.dot_general(a.astype(bf16), b.astype(bf16), NT_DIMS, preferred_element_type=f32)


def _bdot_tn(a, b):
    return lax.dot_general(a.astype(bf16), b.astype(bf16), TN_DIMS, preferred_element_type=f32)


def _split2(x):
    hi = x.astype(bf16)
    lo = (x - hi.astype(f32)).astype(bf16)
    return hi, lo


def _dot_exact_rhs(x, m_bf16):
    hi, lo = _split2(x)
    return (jnp.dot(hi, m_bf16, preferred_element_type=f32)
            + jnp.dot(lo, m_bf16, preferred_element_type=f32))


def _dot_exact_lhs(m_bf16, x):
    hi, lo = _split2(x)
    return (jnp.dot(m_bf16, hi, preferred_element_type=f32)
            + jnp.dot(m_bf16, lo, preferred_element_type=f32))


def _dot3_nt(a, b):
    ah, al = _split2(a)
    bh, bl = _split2(b)
    d = lambda u, v: lax.dot_general(u, v, NT_DIMS, preferred_element_type=f32)
    return d(ah, bh) + d(al, bh) + d(ah, bl)


def _iota(shape, dim):
    return lax.broadcasted_iota(jnp.int32, shape, dim)


def _seg_matrix(n, seg):
    return (_iota((n, n), 0) // seg == _iota((n, n), 1) // seg).astype(bf16)


def _layer_norm(z, g, b):
    mu = jnp.mean(z, -1, keepdims=True)
    d = z - mu
    var = jnp.mean(d * d, -1, keepdims=True)
    return d * lax.rsqrt(var + LN_EPS) * g + b


def _row_to_col(row, n):
    eye = _iota((n, n), 0) == _iota((n, n), 1)
    return jnp.sum(jnp.where(eye, jnp.broadcast_to(row, (n, n)), 0.0), axis=1, keepdims=True)


def _set_row(ref, idx, row):
    cur = ref[...]
    ref[...] = jnp.where(_iota(cur.shape, 0) == idx, jnp.broadcast_to(row, cur.shape), cur)


def _col_to_row(col, n):
    eye = _iota((n, n), 0) == _iota((n, n), 1)
    return jnp.sum(jnp.where(eye, jnp.broadcast_to(col, (n, n)), 0.0), axis=0, keepdims=True)


def _inproj_body(x_ref, w_ref, o_ref, xb_ref):
    @pl.when(pl.program_id(1) == 0)
    def _():
        xb_ref[...] = x_ref[...].astype(bf16)

    o_ref[...] = jnp.dot(xb_ref[...], w_ref[...], preferred_element_type=f32)


def in_proj(x, w, tm, tn):
    m, k = x.shape
    n = w.shape[1]
    return pl.pallas_call(
        _inproj_body,
        grid=(m // tm, n // tn),
        in_specs=[pl.BlockSpec((tm, k), lambda i, j: (i, 0)),
                  pl.BlockSpec((k, tn), lambda i, j: (0, j))],
        out_specs=pl.BlockSpec((tm, tn), lambda i, j: (i, j)),
        out_shape=jax.ShapeDtypeStruct((m, n), f32),
        scratch_shapes=[pltpu.VMEM((tm, k), bf16)],
        compiler_params=_params(("arbitrary", "arbitrary")),
        name="in_proj",
    )(x, w)


def _merge_body(x_ref, pg0_ref, pg1_ref, pg2_ref, y0_ref, y1_ref, y2_ref, wb_ref, bg_ref, wo_ref,
                g_ref, b_ref, o_ref):
    merged = None
    for n, (pg_ref, y_ref) in enumerate(((pg0_ref, y0_ref), (pg1_ref, y1_ref), (pg2_ref, y2_ref))):
        gate = jax.nn.sigmoid(pg_ref[...] + bg_ref[n:n + 1, :])
        term = gate * jnp.dot(y_ref[...], wb_ref[n], preferred_element_type=f32)
        merged = term if merged is None else merged + term
    out = jnp.dot(merged.astype(bf16), wo_ref[...], preferred_element_type=f32)
    o_ref[...] = _layer_norm(ALPHA * x_ref[...] + out, g_ref[...], b_ref[...])


def merge_ln(x, proj, y_gla, y_moba, y_rwkv, w_branch, b_gate, w_out, g, b, tm):
    m = x.shape[0]
    row = lambda i: (i, 0)
    const2 = lambda i: (0, 0)
    pg_spec = lambda n: pl.BlockSpec((tm, D_MODEL), lambda i, n=n: (i, C_PG // D_MODEL + n))
    y_spec = pl.BlockSpec((tm, W), row)
    return pl.pallas_call(
        _merge_body,
        grid=(m // tm,),
        in_specs=[pl.BlockSpec((tm, D_MODEL), row), pg_spec(0), pg_spec(1), pg_spec(2),
                  y_spec, y_spec, y_spec,
                  pl.BlockSpec((N_BRANCH, W, D_MODEL), lambda i: (0, 0, 0)),
                  pl.BlockSpec((N_BRANCH, D_MODEL), const2),
                  pl.BlockSpec((D_MODEL, D_MODEL), const2),
                  pl.BlockSpec((1, D_MODEL), const2), pl.BlockSpec((1, D_MODEL), const2)],
        out_specs=pl.BlockSpec((tm, D_MODEL), row),
        out_shape=jax.ShapeDtypeStruct((m, D_MODEL), f32),
        compiler_params=_params(("arbitrary",)),
        name="merge_ln",
    )(x, proj, proj, proj, y_gla, y_moba, y_rwkv, w_branch, b_gate, w_out, g, b)


FF_CHUNK = 1024


def _mlp_body(x_ref, wu_ref, wd_ref, g_ref, b_ref, o_ref):
    x = x_ref[...]
    xb = x.astype(bf16)
    acc = None
    for c in range(D_FF // FF_CHUNK):
        h = jnp.dot(xb, wu_ref[:, c * FF_CHUNK:(c + 1) * FF_CHUNK], preferred_element_type=f32)
        h = jnp.square(jnp.maximum(h, 0.0)).astype(bf16)
        part = jnp.dot(h, wd_ref[c * FF_CHUNK:(c + 1) * FF_CHUNK, :], preferred_element_type=f32)
        acc = part if acc is None else acc + part
    o_ref[...] = _layer_norm(ALPHA * x + acc, g_ref[...], b_ref[...])


def mlp_ln(x, w_up, w_down, g, b, tm):
    m = x.shape[0]
    row = lambda i: (i, 0)
    const2 = lambda i: (0, 0)
    return pl.pallas_call(
        _mlp_body,
        grid=(m // tm,),
        in_specs=[pl.BlockSpec((tm, D_MODEL), row),
                  pl.BlockSpec((D_MODEL, D_FF), const2, pipeline_mode=pl.Buffered(1)),
                  pl.BlockSpec((D_FF, D_MODEL), const2, pipeline_mode=pl.Buffered(1)),
                  pl.BlockSpec((1, D_MODEL), const2), pl.BlockSpec((1, D_MODEL), const2)],
        out_specs=pl.BlockSpec((tm, D_MODEL), row),
        out_shape=jax.ShapeDtypeStruct((m, D_MODEL), f32),
        compiler_params=_params(("arbitrary",)),
        name="mlp_ln",
    )(x, w_up, w_down, g, b)


def _gla_log_gate(glow, up_ref, bias_ref):
    z = _bdot(glow, up_ref[...]) + bias_ref[...]
    return jax.nn.log_sigmoid(z) * (1.0 / GLA_GATE_NORMALIZER)


def _gla_out(o, norm_w, g_out):
    o = o * lax.rsqrt(jnp.mean(jnp.square(o), -1, keepdims=True) + GLA_NORM_EPS) * norm_w
    return o * jax.nn.silu(g_out)


def _gla_prompt_body(q_ref, k_ref, v_ref, go_ref, glow_ref, up_ref, bias_ref, nw_ref,
                     y_ref, s_out_ref, s_ref):
    t = pl.program_id(1)
    rows = q_ref.shape[0]
    c_sz = GLA_CHUNK

    @pl.when(t == 0)
    def _():
        s_ref[...] = jnp.zeros_like(s_ref)

    lg = _gla_log_gate(glow_ref[...], up_ref, bias_ref)
    tril = (_iota((c_sz, c_sz), 1) <= _iota((c_sz, c_sz), 0))
    tril_b = tril.astype(bf16)
    n_chunks = rows // c_sz
    probs = []
    for c in range(n_chunks):
        rs = slice(c * c_sz, (c + 1) * c_sz)
        g_cum = _dot_exact_lhs(tril_b, lg[rs])
        g_last = g_cum[c_sz - 1:c_sz, :]
        q_dec = q_ref[rs, :] * (GLA_DK ** -0.5) * jnp.exp(g_cum)
        k = k_ref[rs, :]
        k_inv = k * jnp.exp(-g_cum)
        k_tail = k * jnp.exp(g_last - g_cum)
        e_last = jnp.exp(g_last)
        for h in range(GLA_HEADS):
            ks = slice(h * GLA_DK, (h + 1) * GLA_DK)
            vs = slice(h * GLA_DV, (h + 1) * GLA_DV)
            probs.append((q_dec[:, ks], k_inv[:, ks], k_tail[:, ks], e_last[:, ks], v_ref[rs, vs], rs, vs))
    a = [jnp.where(tril, _bdot_nt(q[0], q[1]), 0.0) for q in probs]
    o_intra = [_bdot(x, q[4]) for x, q in zip(a, probs)]
    kv = [_bdot_tn(q[2], q[4]) for q in probs]
    e_col = [_row_to_col(q[3], GLA_DK) for q in probs]

    state = [s_ref[h] for h in range(GLA_HEADS)]
    for c in range(n_chunks):
        sl = slice(c * GLA_HEADS, (c + 1) * GLA_HEADS)
        o = [oi + _bdot(q[0], s) for oi, q, s in zip(o_intra[sl], probs[sl], state)]
        state = [s * e + x for s, e, x in zip(state, e_col[sl], kv[sl])]
        for oo, q in zip(o, probs[sl]):
            rs, vs = q[5], q[6]
            y_ref[rs, vs] = _gla_out(oo, nw_ref[...], go_ref[rs, vs]).astype(y_ref.dtype)
    for h in range(GLA_HEADS):
        s_ref[h] = state[h]
    s_out_ref[...] = s_ref[...]


def gla_prompt(proj, up_pad, bias, norm_w, n_batch, seq, rt):
    nt = seq // rt
    rowmap = lambda width, col: pl.BlockSpec((rt, width), lambda b, t: (b * nt + t, col // width))
    const2 = lambda b, t: (0, 0)
    return pl.pallas_call(
        _gla_prompt_body,
        grid=(n_batch, nt),
        in_specs=[rowmap(GLA_HEADS * GLA_DK, C_GQ), rowmap(GLA_HEADS * GLA_DK, C_GK), rowmap(W, C_GV),
                  rowmap(W, C_GO), rowmap(LANES, C_GLOW),
                  pl.BlockSpec((LANES, GLA_HEADS * GLA_DK), const2),
                  pl.BlockSpec((1, GLA_HEADS * GLA_DK), const2),
                  pl.BlockSpec((1, GLA_DV), const2)],
        out_specs=[pl.BlockSpec((rt, W), lambda b, t: (b * nt + t, 0)),
                   pl.BlockSpec((None, GLA_HEADS, GLA_DK, GLA_DV), lambda b, t: (b, 0, 0, 0))],
        out_shape=[jax.ShapeDtypeStruct((n_batch * seq, W), bf16),
                   jax.ShapeDtypeStruct((n_batch, GLA_HEADS, GLA_DK, GLA_DV), f32)],
        scratch_shapes=[pltpu.VMEM((GLA_HEADS, GLA_DK, GLA_DV), f32)],
        compiler_params=_params(("arbitrary", "arbitrary")),
        name="gla_prompt",
    )(proj, proj, proj, proj, proj, up_pad, bias, norm_w)


def _rwkv_pre(r_s, k_s, v_s, wa_s, gr_s, p):
    seg = _seg_matrix(W, RWKV_HEAD)
    logw = -RWKV_DECAY_SCALE * jax.nn.sigmoid(p["w0"][...] + _bdot(jnp.tanh(wa_s), p["w_up"][...]))
    a = jax.nn.sigmoid(p["a0"][...] + _bdot(wa_s, p["a_up"][...]))
    g7 = _bdot(jax.nn.sigmoid(gr_s), p["g_up"][...])
    kk = k_s * p["k_k"][...]
    kk = kk * lax.rsqrt(_dot_exact_rhs(jnp.square(kk), seg) + 1e-12)
    k7 = k_s * (1.0 + (a - 1.0) * p["k_a"][...])
    bonus = _dot_exact_rhs(r_s * k7 * p["r_k"][...], seg) * v_s
    return dict(r=r_s, logw=logw, k=k7, v=v_s, kk=kk, a=a, g7=g7, bonus=bonus)


def _rwkv_post(yw, bonus, g7, p):
    seg = _seg_matrix(W, RWKV_HEAD)
    inv_n = 1.0 / RWKV_HEAD
    mu = _dot_exact_rhs(yw, seg) * inv_n
    d = yw - mu
    var = _dot_exact_rhs(jnp.square(d), seg) * inv_n
    yn = d * lax.rsqrt(var + RWKV_GN_EPS) * p["ln_w"][...] + p["ln_b"][...]
    return (yn + bonus) * g7


def _inv_unit_lower(n_mats):
    c = n_mats[0].shape[0]
    row, col = _iota((c, c), 0), _iota((c, c), 1)
    eye = (row == col).astype(f32)
    blk = (row // RWKV_INV_BLOCK) == (col // RWKV_INV_BLOCK)
    d = [jnp.where(blk, n, 0.0) for n in n_mats]
    lo = [n - x for n, x in zip(n_mats, d)]
    d2 = [_bdot(x, x) for x in d]
    d4 = [_bdot(x, x) for x in d2]
    d8 = [_bdot(x, x) for x in d4]
    pa = [_bdot(eye - x, eye + y) for x, y in zip(d, d2)]
    pb = [_bdot(eye + x, eye + y) for x, y in zip(d4, d8)]
    d_inv = [_bdot(x, y) for x, y in zip(pa, pb)]
    e = [_bdot(x, y) for x, y in zip(d_inv, lo)]
    e2 = [_bdot(x, x) for x in e]
    f = [_bdot(eye - x, eye + y) for x, y in zip(e, e2)]
    return [_bdot(x, y) for x, y in zip(f, d_inv)]


def _rwkv_chunk_coeffs(probs):
    c, kd = probs[0][0].shape
    row, col = _iota((2 * c, 2 * c), 0), _iota((2 * c, 2 * c), 1)
    mask = (col % c < row % c) | ((row >= c) & (col % c == row % c))
    eye_k = _iota((kd, kd), 0) == _iota((kd, kd), 1)
    prod = [jnp.where(mask, _bdot_nt(jnp.concatenate([a_t, r_t], 0), jnp.concatenate([b_t, k_t], 0)), 0.0)
            for r_t, a_t, b_t, k_t, _, _, _, _ in probs]
    t_inv = _inv_unit_lower([p[:c, :c] for p in prod])
    av = [_bdot(p[:c, c:], q[6]) for p, q in zip(prod, probs)]
    x = [_bdot(ti, jnp.concatenate([q[1], a], 1)) for ti, q, a in zip(t_inv, probs, av)]
    px = [_bdot(p[c:, :c], xx) for p, xx in zip(prod, x)]
    prkv = [_bdot(p[c:, c:], q[6]) for p, q in zip(prod, probs)]
    xb = [_bdot_tn(xx, q[4]) for xx, q in zip(x, probs)]
    vk = [_bdot_tn(q[6], q[5]) for q in probs]
    out = []
    for q, pxx, pv, xbb, vkk in zip(probs, px, prkv, xb, vk):
        tm_t = jnp.where(eye_k, jnp.broadcast_to(q[7], (kd, kd)), 0.0) - xbb[:kd]
        out.append((q[0] - pxx[:, :kd], pv - pxx[:, kd:], tm_t, vkk - xbb[kd:]))
    return out


_RWKV_PARAM_NAMES = ("mu_r", "mu_k", "mu_v", "mu_wa", "mu_gr", "w0", "w_up", "a0", "a_up", "g_up",
                     "k_k", "k_a", "r_k", "ln_w", "ln_b")


def _rwkv_prompt_body(r_ref, k_ref, v_ref, wa_ref, gr_ref, *rest):
    np_ = len(_RWKV_PARAM_NAMES)
    p = dict(zip(_RWKV_PARAM_NAMES, rest[:np_]))
    y_ref, s_out_ref = rest[np_:np_ + 2]
    s_ref, prev_ref = rest[np_ + 2:]
    t = pl.program_id(1)
    rows = r_ref.shape[0]
    c_sz = RWKV_CHUNK

    @pl.when(t == 0)
    def _():
        s_ref[...] = jnp.zeros_like(s_ref)
        prev_ref[...] = jnp.zeros_like(prev_ref)

    def shifted(x_ref, mu_ref, c0):
        x = x_ref[...]
        width = x.shape[1]
        prev = jnp.where(_iota(x.shape, 0) == 0, prev_ref[0:1, c0:c0 + width], pltpu.roll(x, 1, 0))
        prev_ref[0:1, c0:c0 + width] = x[rows - 1:rows, :]
        return x + mu_ref[...] * (prev - x)

    pre = _rwkv_pre(shifted(r_ref, p["mu_r"], 0), shifted(k_ref, p["mu_k"], W), shifted(v_ref, p["mu_v"], 2 * W),
                    shifted(wa_ref, p["mu_wa"], 3 * W), shifted(gr_ref, p["mu_gr"], 3 * W + LANES), p)
    tril_b = (_iota((c_sz, c_sz), 1) <= _iota((c_sz, c_sz), 0)).astype(bf16)
    n_chunks = rows // c_sz
    b_all = pre["kk"] * pre["a"]

    probs = []
    for c in range(n_chunks):
        rs = slice(c * c_sz, (c + 1) * c_sz)
        lw = pre["logw"][rs]
        g_cum = _dot_exact_lhs(tril_b, lw)
        g_last = g_cum[c_sz - 1:c_sz, :]
        e_ng = jnp.exp(-g_cum)
        e_gl = jnp.exp(g_last - g_cum)
        kk, b, k, v = pre["kk"][rs], b_all[rs], pre["k"][rs], pre["v"][rs]
        r_t = pre["r"][rs] * jnp.exp(g_cum)
        a_t = kk * jnp.exp(g_cum - lw)
        b_t, k_t, b_g, k_g = b * e_ng, k * e_ng, b * e_gl, k * e_gl
        gam = jnp.exp(g_last)
        for h in range(RWKV_HEADS):
            hs = slice(h * RWKV_HEAD, (h + 1) * RWKV_HEAD)
            probs.append((r_t[:, hs], a_t[:, hs], b_t[:, hs], k_t[:, hs], b_g[:, hs], k_g[:, hs], v[:, hs],
                          gam[:, hs]))
    coeffs = _rwkv_chunk_coeffs(probs)

    state = [s_ref[h] for h in range(RWKV_HEADS)]
    y_rows = []
    for c in range(n_chunks):
        cf = coeffs[c * RWKV_HEADS:(c + 1) * RWKV_HEADS]
        y_rows.append(jnp.concatenate([_bdot_nt(q[0], s) + q[1] for q, s in zip(cf, state)], axis=1))
        state = [_bdot(s, q[2]) + q[3] for q, s in zip(cf, state)]
    for h in range(RWKV_HEADS):
        s_ref[h] = state[h]
    yw = jnp.concatenate(y_rows, axis=0)
    y_ref[...] = _rwkv_post(yw, pre["bonus"], pre["g7"], p).astype(y_ref.dtype)
    s_out_ref[...] = s_ref[...]


def _rwkv_param_specs(const_map):
    vec = lambda width: pl.BlockSpec((1, width), const_map)
    mat = lambda r: pl.BlockSpec((r, W), const_map)
    return [vec(W), vec(W), vec(W), vec(LANES), vec(LANES), vec(W), mat(LANES), vec(W), mat(LANES),
            mat(RWKV_GATE_RANK), vec(W), vec(W), vec(W), vec(W), vec(W)]


def rwkv_prompt(proj, params, n_batch, seq, rt):
    nt = seq // rt
    rowmap = lambda width, col: pl.BlockSpec((rt, width), lambda b, t: (b * nt + t, col // width))
    const2 = lambda b, t: (0, 0)
    return pl.pallas_call(
        _rwkv_prompt_body,
        grid=(n_batch, nt),
        in_specs=[rowmap(W, C_R), rowmap(W, C_K7), rowmap(W, C_V7), rowmap(LANES, C_WA), rowmap(LANES, C_GR)]
        + _rwkv_param_specs(const2),
        out_specs=[pl.BlockSpec((rt, W), lambda b, t: (b * nt + t, 0)),
                   pl.BlockSpec((None, RWKV_HEADS, RWKV_HEAD, RWKV_HEAD), lambda b, t: (b, 0, 0, 0))],
        out_shape=[jax.ShapeDtypeStruct((n_batch * seq, W), bf16),
                   jax.ShapeDtypeStruct((n_batch, RWKV_HEADS, RWKV_HEAD, RWKV_HEAD), f32)],
        scratch_shapes=[pltpu.VMEM((RWKV_HEADS, RWKV_HEAD, RWKV_HEAD), f32), pltpu.VMEM((8, RWKV_PROJ), f32)],
        compiler_params=_params(("arbitrary", "arbitrary")),
        name="rwkv_prompt",
    )(proj, proj, proj, proj, proj, *[params[n] for n in _RWKV_PARAM_NAMES])


def _rope(x, cos, sin_signed):
    lane = _iota(x.shape, 1)
    half = MOBA_HEAD_DIM // 2
    width = x.shape[1]
    swapped = jnp.where(lane % MOBA_HEAD_DIM < half, pltpu.roll(x, width - half, 1), pltpu.roll(x, half, 1))
    return x * cos + swapped * sin_signed


def _sample_body(gq_ref, gk_ref, gv_ref, go_ref, glow_ref, r_ref, k_ref, v_ref, wa_ref, gr_ref, mq_ref, mk_ref,
                 sg_ref, sw_ref, shift_ref, cos_ref, sin_ref, up_ref, bias_ref, nw_ref, *rest):
    np_ = len(_RWKV_PARAM_NAMES)
    p = dict(zip(_RWKV_PARAM_NAMES, rest[:np_]))
    yg_ref, yr_ref, qs_ref, ks_ref, sg_out, sw_out = rest[np_:np_ + 6]
    qt_s, kt_s, egt_s, og_s, rt_s, wt_s, k7t_s, vt_s, kkt_s, bt_s, yt_s, bonus_s, g7_s = rest[np_ + 6:]
    hp = pl.program_id(0)
    n_seq = gq_ref.shape[0]

    @pl.when(hp == 0)
    def _():
        qs_ref[...] = _rope(mq_ref[...], cos_ref[...], sin_ref[...]) * (MOBA_HEAD_DIM ** -0.5)
        ks_ref[...] = _rope(mk_ref[...], cos_ref[...], sin_ref[...])
        lg = _gla_log_gate(glow_ref[...], up_ref, bias_ref)
        qt_s[...] = (gq_ref[...] * (GLA_DK ** -0.5)).T
        kt_s[...] = gk_ref[...].T
        egt_s[...] = jnp.exp(lg).T

        def shifted(x_ref, mu_ref, c0):
            x = x_ref[...]
            return x + mu_ref[...] * (shift_ref[:, c0:c0 + x.shape[1]] - x)

        pre = _rwkv_pre(shifted(r_ref, p["mu_r"], 0), shifted(k_ref, p["mu_k"], W), shifted(v_ref, p["mu_v"], 2 * W),
                        shifted(wa_ref, p["mu_wa"], 3 * W), shifted(gr_ref, p["mu_gr"], 3 * W + LANES), p)
        rt_s[...] = pre["r"].T
        wt_s[...] = jnp.exp(pre["logw"]).T
        k7t_s[...] = pre["k"].T
        vt_s[...] = pre["v"].T
        kkt_s[...] = pre["kk"].T
        bt_s[...] = (pre["kk"] * pre["a"]).T
        bonus_s[...] = pre["bonus"]
        g7_s[...] = pre["g7"]
        og_s[...] = jnp.zeros_like(og_s)

    ks = pl.ds(pl.multiple_of(hp * GLA_DK, GLA_DK), GLA_DK)
    eg_t, k_t, q_t = egt_s[ks, :], kt_s[ks, :], qt_s[ks, :]
    gv_h = gv_ref[:, pl.ds(pl.multiple_of(hp * GLA_DV, GLA_DV), GLA_DV)]
    lane_grp = _iota((n_seq, W), 1) // GLA_DV
    o_rows = []
    for s in range(n_seq):
        col = lambda x: jnp.broadcast_to(x[:, s:s + 1], (GLA_DK, GLA_DV))
        st = sg_ref[s] * col(eg_t) + col(k_t) * gv_h[s:s + 1, :]
        sg_out[s] = st
        o_rows.append(jnp.sum(col(q_t) * st, axis=0, keepdims=True))
    o_h = jnp.concatenate(o_rows, axis=0)
    og_s[...] = jnp.where(lane_grp == hp, jnp.concatenate([o_h] * GLA_HEADS, axis=1), og_s[...])

    group = RWKV_HEAD // SUBLANES
    for u in range(2):
        hs = pl.ds(pl.multiple_of((2 * hp + u) * RWKV_HEAD, RWKV_HEAD), RWKV_HEAD)
        neg_kk, w_t, b_t, k_t7, r_t = -kkt_s[hs, :], wt_s[hs, :], bt_s[hs, :], k7t_s[hs, :], rt_s[hs, :]

        def v_group(g, carry, u=u, hs=hs, neg_kk=neg_kk, w_t=w_t, b_t=b_t, k_t7=k_t7, r_t=r_t):
            base = pl.multiple_of((2 * hp + u) * RWKV_HEAD + g * SUBLANES, SUBLANES)
            v_rows = vt_s[pl.ds(base, SUBLANES), :]
            y_rows = []
            for j in range(SUBLANES):
                vi = g * SUBLANES + j
                st = sw_ref[u, vi]
                sa = jnp.sum(st * neg_kk, axis=0, keepdims=True)
                st = st * w_t + sa * b_t + v_rows[j:j + 1, :] * k_t7
                sw_out[u, vi] = st
                y_rows.append(jnp.sum(st * r_t, axis=0, keepdims=True))
            yt_s[pl.ds(base, SUBLANES), :] = jnp.concatenate(y_rows, axis=0)
            return carry

        lax.fori_loop(0, group, v_group, 0)

    @pl.when(hp == pl.num_programs(0) - 1)
    def _():
        og = og_s[...]
        for h in range(GLA_HEADS):
            vs = slice(h * GLA_DV, (h + 1) * GLA_DV)
            yg_ref[:, vs] = _gla_out(og[:, vs], nw_ref[...], go_ref[:, vs]).astype(yg_ref.dtype)
        yr_ref[...] = _rwkv_post(yt_s[...].T, bonus_s[...], g7_s[...], p).astype(yr_ref.dtype)


def sample_step(proj, row0, n_seq, layer, state_gla, state_wkv_t, state_shift, cos_s, sin_s,
                up_pad, bias, norm_w, params):
    r0 = row0 // n_seq
    rowmap = lambda width, col: pl.BlockSpec((n_seq, width), lambda h: (r0, col // width))
    const2 = lambda h: (0, 0)
    hk = GLA_HEADS * GLA_DK
    out_row = pl.BlockSpec((n_seq, W), const2)
    tr = lambda rows: pltpu.VMEM((rows, n_seq), f32)
    rw = lambda: pltpu.VMEM((n_seq, W), f32)
    return pl.pallas_call(
        _sample_body,
        grid=(GLA_HEADS,),
        in_specs=[rowmap(hk, C_GQ), rowmap(hk, C_GK), rowmap(W, C_GV), rowmap(W, C_GO), rowmap(LANES, C_GLOW),
                  rowmap(W, C_R), rowmap(W, C_K7), rowmap(W, C_V7), rowmap(LANES, C_WA), rowmap(LANES, C_GR),
                  rowmap(W, C_MQ), rowmap(W, C_MK),
                  pl.BlockSpec((None, n_seq, None, GLA_DK, GLA_DV), lambda h: (layer, 0, h, 0, 0)),
                  pl.BlockSpec((None, 2, RWKV_HEAD, RWKV_HEAD, n_seq), lambda h: (layer, h, 0, 0, 0)),
                  pl.BlockSpec((None, n_seq, RWKV_PROJ), lambda h: (layer, 0, 0)),
                  pl.BlockSpec((1, W), const2), pl.BlockSpec((1, W), const2),
                  pl.BlockSpec((LANES, hk), const2), pl.BlockSpec((1, hk), const2), pl.BlockSpec((1, GLA_DV), const2)]
        + _rwkv_param_specs(const2),
        out_specs=[out_row, out_row, out_row, out_row,
                   pl.BlockSpec((n_seq, None, GLA_DK, GLA_DV), lambda h: (0, h, 0, 0)),
                   pl.BlockSpec((2, RWKV_HEAD, RWKV_HEAD, n_seq), lambda h: (h, 0, 0, 0))],
        out_shape=[jax.ShapeDtypeStruct((n_seq, W), bf16), jax.ShapeDtypeStruct((n_seq, W), bf16),
                   jax.ShapeDtypeStruct((n_seq, W), f32), jax.ShapeDtypeStruct((n_seq, W), f32),
                   jax.ShapeDtypeStruct((n_seq, GLA_HEADS, GLA_DK, GLA_DV), f32),
                   jax.ShapeDtypeStruct((RWKV_HEADS, RWKV_HEAD, RWKV_HEAD, n_seq), f32)],
        scratch_shapes=[tr(hk), tr(hk), tr(hk), rw(), tr(W), tr(W), tr(W), tr(W), tr(W), tr(W), tr(W), rw(), rw()],
        compiler_params=_params(("arbitrary",)),
        name="sample_step",
    )(proj, proj, proj, proj, proj, proj, proj, proj, proj, proj, proj, proj,
      state_gla, state_wkv_t, state_shift, cos_s, sin_s, up_pad, bias, norm_w,
      *[params[n] for n in _RWKV_PARAM_NAMES])


def _topk_select(gate, n_valid, axis_len, axis):
    idx = _iota(gate.shape, axis)
    cnt = jnp.zeros(gate.shape, f32)
    for m in range(axis_len):
        g_m = lax.slice_in_dim(gate, m, m + 1, axis=axis)
        g_m = jnp.broadcast_to(g_m, gate.shape)
        beats = (g_m > gate) | ((g_m == gate) & (m < idx))
        cnt = cnt + jnp.where(beats, 1.0, 0.0) * jnp.where(m < n_valid, 1.0, 0.0)
    return (idx < n_valid) & (cnt < MOBA_TOPK)


def _rope_prep_body(mq_ref, mk_ref, mv_ref, cos_ref, sin_ref, k_out, qb_out, kb_out, vb_out, sel_out, km_ref):
    i = pl.program_id(1)
    blk = mq_ref.shape[0]
    nb = km_ref.shape[0]
    nh = MOBA_HEADS

    @pl.when(i == 0)
    def _():
        km_ref[...] = jnp.zeros_like(km_ref)

    q = _rope(mq_ref[...], cos_ref[...], sin_ref[...]) * (MOBA_HEAD_DIM ** -0.5)
    k = _rope(mk_ref[...], cos_ref[...], sin_ref[...])
    k_out[...] = k
    qb_out[...] = q.astype(bf16)
    kb_out[...] = k.astype(bf16)
    vb_out[...] = mv_ref[...].astype(bf16)

    km = km_ref[...]
    km_rows = jnp.concatenate([km] * nh, axis=0)
    head_of_row = _iota(km_rows.shape, 0) // nb
    km_rows = jnp.where(head_of_row == _iota(km_rows.shape, 1) // MOBA_HEAD_DIM, km_rows, 0.0)
    gate = _dot3_nt(km_rows, q).reshape(nh, nb, blk)
    sel = _topk_select(gate, i, nb, 1).astype(bf16).reshape(nh * nb, blk)
    sel = jnp.concatenate([sel, jnp.zeros((LANES - nh * nb, blk), bf16)], axis=0)
    eye = (_iota((blk, blk), 0) == _iota((blk, blk), 1)).astype(bf16)
    sel_out[...] = lax.dot_general(eye, sel, NT_DIMS, preferred_element_type=f32)

    _set_row(km_ref, i, jnp.sum(k, axis=0, keepdims=True) * (1.0 / blk))


def rope_prep(proj, cos, sin, n_batch, seq):
    blk = MOBA_BLOCK
    nb = seq // blk
    m = n_batch * seq
    rowmap = lambda col: pl.BlockSpec((blk, W), lambda b, i: (b * nb + i, col // W))
    tab = pl.BlockSpec((blk, W), lambda b, i: (i, 0))
    out = lambda width: pl.BlockSpec((blk, width), lambda b, i: (b * nb + i, 0))
    return pl.pallas_call(
        _rope_prep_body,
        grid=(n_batch, nb),
        in_specs=[rowmap(C_MQ), rowmap(C_MK), rowmap(C_MV), tab, tab],
        out_specs=[out(W), out(W), out(W), out(W), out(LANES)],
        out_shape=[jax.ShapeDtypeStruct((m, W), f32), jax.ShapeDtypeStruct((m, W), bf16),
                   jax.ShapeDtypeStruct((m, W), bf16), jax.ShapeDtypeStruct((m, W), bf16),
                   jax.ShapeDtypeStruct((m, LANES), f32)],
        scratch_shapes=[pltpu.VMEM((nb, W), f32)],
        compiler_params=_params(("arbitrary", "arbitrary")),
        name="rope_prep",
    )(proj, proj, proj, cos, sin)


def _moba_prompt_body(qb_ref, kb_ref, vb_ref, sel_ref, y_ref, m_ref, acc_ref):
    i = pl.program_id(1)
    blk = qb_ref.shape[0]
    nb = kb_ref.shape[0] // blk
    lane = _iota((blk, LANES), 1)
    low_half = lane < MOBA_HEAD_DIM
    causal = _iota((blk, blk), 1) <= _iota((blk, blk), 0)
    neg_inf = -jnp.inf

    def head_q(pair, u):
        qp = qb_ref[:, pair * LANES:(pair + 1) * LANES]
        return jnp.where(low_half if u == 0 else ~low_half, qp, jnp.zeros_like(qp))

    def kv(n, pair):
        rs = pl.ds(pl.multiple_of(n * blk, blk), blk)
        ls = slice(pair * LANES, (pair + 1) * LANES)
        return kb_ref[rs, ls], vb_ref[rs, ls]

    wide = lambda x: jnp.concatenate([x, x], axis=1)

    def head_v(v_n, u):
        return jnp.where(low_half if u == 0 else ~low_half, v_n, jnp.ones_like(v_n))

    for pair in range(MOBA_HEADS // 2):
        k_n, v_n = kv(i, pair)
        for u in range(2):
            h = 2 * pair + u
            s = jnp.where(causal, lax.dot_general(head_q(pair, u), k_n, NT_DIMS, preferred_element_type=f32), neg_inf)
            m = jnp.broadcast_to(jnp.max(s, axis=1, keepdims=True), (blk, LANES))
            p = jnp.exp(s - wide(m))
            m_ref[h] = m
            acc_ref[h] = jnp.dot(p.astype(bf16), head_v(v_n, u), preferred_element_type=f32)

    sel_b = sel_ref[...].astype(bf16)
    one_hot_row = _iota((LANES, LANES), 0)

    def past(n, carry):
        for pair in range(MOBA_HEADS // 2):
            k_n, v_n = kv(n, pair)
            for u in range(2):
                h = 2 * pair + u
                pick = (one_hot_row == h * nb + n).astype(bf16)
                chosen = jnp.dot(sel_b, pick, preferred_element_type=f32)
                s = lax.dot_general(head_q(pair, u), k_n, NT_DIMS, preferred_element_type=f32)
                s = jnp.where(wide(chosen) > 0.5, s, neg_inf)
                m_old = m_ref[h]
                m_new = jnp.maximum(m_old, jnp.broadcast_to(jnp.max(s, axis=1, keepdims=True), (blk, LANES)))
                alpha = jnp.exp(m_old - m_new)
                p = jnp.exp(s - wide(m_new))
                m_ref[h] = m_new
                acc_ref[h] = alpha * acc_ref[h] + jnp.dot(p.astype(bf16), head_v(v_n, u), preferred_element_type=f32)
        return carry

    lax.fori_loop(0, i, past, 0)

    for pair in range(MOBA_HEADS // 2):
        halves = []
        for u in range(2):
            acc = acc_ref[2 * pair + u]
            halves.append(acc / pltpu.roll(acc, MOBA_HEAD_DIM, 1))
        y_ref[:, pair * LANES:(pair + 1) * LANES] = jnp.where(low_half, halves[0], halves[1]).astype(y_ref.dtype)


def moba_prompt(qb, kb, vb, sel, n_batch, seq):
    blk = MOBA_BLOCK
    nb = seq // blk
    row = lambda width: pl.BlockSpec((blk, width), lambda b, i: (b * nb + i, 0))
    full = pl.BlockSpec((seq, W), lambda b, i: (b, 0))
    scr = lambda: pltpu.VMEM((MOBA_HEADS, blk, LANES), f32)
    return pl.pallas_call(
        _moba_prompt_body,
        grid=(n_batch, nb),
        in_specs=[row(W), full, full, row(LANES)],
        out_specs=row(W),
        out_shape=jax.ShapeDtypeStruct((n_batch * seq, W), bf16),
        scratch_shapes=[scr(), scr()],
        compiler_params=_params(("arbitrary", "arbitrary")),
        name="moba_prompt",
    )(qb, kb, vb, sel)


def _moba_sample_body(n_pages, pt_ref, qb_ref, q_ref, k_ref, v_ref, *rest):
    del pt_ref
    k_pages, v_pages = rest[:n_pages], rest[n_pages:2 * n_pages]
    y_ref = rest[2 * n_pages]
    nh, dh = MOBA_HEADS, MOBA_HEAD_DIM
    pages_per_block = MOBA_BLOCK // PAGE_SIZE
    nb = n_pages // pages_per_block
    lane = _iota((nh, LANES), 1)
    q_b = qb_ref[...]

    def head_sum(x):
        return jnp.concatenate([jnp.sum(x[h * dh:(h + 1) * dh], axis=0, keepdims=True) for h in range(nh)], axis=0)

    def head_rows(x):
        return jnp.concatenate([jnp.broadcast_to(x[h:h + 1], (dh, x.shape[1])) for h in range(nh)], axis=0)

    s_pages = [head_sum(k_pages[g][...].reshape(nh * dh, PAGE_SIZE) * q_b) for g in range(n_pages)]
    gate = jnp.zeros((nh, LANES), f32)
    for n in range(nb):
        blk_sum = sum(jnp.sum(s_pages[n * pages_per_block + t], axis=1, keepdims=True)
                      for t in range(pages_per_block))
        gate = jnp.where(lane == n, blk_sum * (1.0 / MOBA_BLOCK), gate)
    sel = _topk_select(gate, nb, nb, 1).astype(f32)
    s_own = jnp.sum(q_ref[...] * k_ref[...], axis=1, keepdims=True)
    s_pages = [jnp.where(jnp.broadcast_to(sel[:, g // pages_per_block:g // pages_per_block + 1],
                                          (nh, PAGE_SIZE)) > 0.5, s, -jnp.inf)
               for g, s in enumerate(s_pages)]
    m = s_own
    for s in s_pages:
        m = jnp.maximum(m, jnp.max(s, axis=1, keepdims=True))
    p_pages = [jnp.exp(s - m) for s in s_pages]
    w_own = jnp.exp(s_own - m)
    l_tot = w_own + sum(jnp.sum(p, axis=1, keepdims=True) for p in p_pages)

    acc = jnp.zeros((nh * dh, PAGE_SIZE), f32)
    for g in range(n_pages):
        acc = acc + head_rows(p_pages[g]) * v_pages[g][...].reshape(nh * dh, PAGE_SIZE)
    hi = acc.astype(bf16)
    mid = (acc - hi.astype(f32))
    lo = (mid - mid.astype(bf16).astype(f32)).astype(bf16)
    ones = jnp.ones((nh, PAGE_SIZE), bf16)
    nt = lambda a: lax.dot_general(ones, a, NT_DIMS, preferred_element_type=f32)
    out = nt(hi) + nt(mid.astype(bf16)) + nt(lo)
    out = (out + w_own * v_ref[...]) / l_tot
    head_lane = _iota((nh, nh * dh), 1) // dh == _iota((nh, nh * dh), 0)
    y_ref[...] = jnp.sum(jnp.where(head_lane, out, 0.0), axis=0, keepdims=True)


def moba_sample(page_table_flat, q_b, q_s, k_s, v_row, cache_kt, cache_vt, layer, n_pages):
    n_seq = q_s.shape[0]
    nh, dh = MOBA_HEADS, MOBA_HEAD_DIM
    heads = pl.BlockSpec((None, nh, dh), lambda s, pt: (s, 0, 0))
    row = pl.BlockSpec((None, 1, W), lambda s, pt: (s, 0, 0))
    page = lambda u: pl.BlockSpec((None, None, nh, dh, PAGE_SIZE),
                                  lambda s, pt, u=u: (layer, pt[s * n_pages + u], 0, 0, 0))
    grid_spec = pltpu.PrefetchScalarGridSpec(
        num_scalar_prefetch=1,
        grid=(n_seq,),
        in_specs=[pl.BlockSpec((None, nh * dh, PAGE_SIZE), lambda s, pt: (s, 0, 0)), heads, heads, row]
        + [page(u) for u in range(n_pages)] + [page(u) for u in range(n_pages)],
        out_specs=row,
    )
    return pl.pallas_call(
        functools.partial(_moba_sample_body, n_pages),
        grid_spec=grid_spec,
        out_shape=jax.ShapeDtypeStruct((n_seq, 1, W), f32),
        compiler_params=_params(("arbitrary",)),
        name="moba_sample",
    )(page_table_flat, q_b, q_s, k_s, v_row, *([cache_kt] * n_pages), *([cache_vt] * n_pages))


def _rope_tables(pos):
    half = MOBA_HEAD_DIM // 2
    inv = ROPE_THETA ** (-jnp.arange(half, dtype=f32) / half)
    ang = pos.astype(f32)[:, None] * inv[None, :]
    cos, sin = jnp.cos(ang), jnp.sin(ang)
    cos_h = jnp.concatenate([cos, cos], -1)
    sin_h = jnp.concatenate([-sin, sin], -1)
    return jnp.tile(cos_h, (1, MOBA_HEADS)), jnp.tile(sin_h, (1, MOBA_HEADS))


def _pack_w_in(w):
    cols = lambda o, n: w[:, o:o + n]
    hk = GLA_HEADS * GLA_DK
    parts = [cols(_O_GQ, hk), cols(_O_GK, hk), cols(_O_GV, W), cols(_O_GO, W),
             cols(_O_MQ, W), cols(_O_MK, W), cols(_O_MV, W), cols(_O_PG, N_BRANCH * D_MODEL),
             cols(_O_PR, RWKV_PROJ), cols(_O_GLOW, GLA_GATE_RANK),
             jnp.zeros((w.shape[0], LANES - GLA_GATE_RANK), w.dtype)]
    return jnp.concatenate(parts, axis=1).astype(bf16)


def _pad_rows(m, rows):
    return jnp.concatenate([m, jnp.zeros((rows - m.shape[0], m.shape[1]), m.dtype)], axis=0)


SUBLANES = 8
IN_PROJ_ROWS_TARGET, IN_PROJ_TN = 1100, 896
TOKEN_ROWS_TARGET = 700
MIXER_ROWS = 256


def _row_tile(m, target):
    best = SUBLANES
    for t in range(SUBLANES, target + 1, SUBLANES):
        if m % t == 0:
            best = t
    assert m % best == 0
    return best


def kernel(x_prompt, x_sample, cache_k, cache_v, page_table, state_gla, state_wkv, state_shift, w_in, b_gate, gla_gk_up, gla_gk_bias, gla_norm_w, rwkv_mu, rwkv_w0, rwkv_w_up, rwkv_a0, rwkv_a_up, rwkv_g_up, rwkv_k_k, rwkv_k_a, rwkv_r_k, rwkv_ln_w, rwkv_ln_b, w_branch, w_out, ln1_g, ln1_b, w_up, w_down, ln2_g, ln2_b):
    n_batch, seq, _ = x_prompt.shape
    n_seq = x_sample.shape[0]
    n_pages = page_table.shape[1]
    mp = n_batch * seq
    past = n_pages * PAGE_SIZE

    cos_p, sin_p = _rope_tables(jnp.arange(seq, dtype=jnp.int32))
    cos_s, sin_s = _rope_tables(jnp.full((1,), past, jnp.int32))
    pt_flat = page_table.reshape(-1)
    cache_kt = jnp.transpose(cache_k, (0, 1, 3, 4, 2))
    cache_vt = jnp.transpose(cache_v, (0, 1, 3, 4, 2))
    state_wkv_t = jnp.transpose(state_wkv, (0, 2, 3, 4, 1))

    x = jnp.concatenate([x_prompt.reshape(mp, D_MODEL), x_sample.reshape(n_seq, D_MODEL)], axis=0)
    token_tm = _row_tile(x.shape[0], TOKEN_ROWS_TARGET)
    outs = {k: [] for k in ("k_p", "v_p", "k_s", "v_s", "gla_p", "gla_s", "wkv_p", "wkv_s", "shift_p", "shift_s")}
    for l in range(DEPTH):
        row2 = lambda v: v.reshape(1, -1)
        mu = rwkv_mu[l]
        rp = dict(
            mu_r=row2(mu[:W]), mu_k=row2(mu[W:2 * W]), mu_v=row2(mu[2 * W:3 * W]),
            mu_wa=row2(mu[3 * W:3 * W + LANES]), mu_gr=row2(mu[3 * W + LANES:]),
            w0=row2(rwkv_w0[l]), w_up=_pad_rows(rwkv_w_up[l], LANES).astype(bf16),
            a0=row2(rwkv_a0[l]),
            a_up=jnp.concatenate([jnp.zeros((RWKV_DECAY_RANK, W), f32), rwkv_a_up[l]], axis=0).astype(bf16),
            g_up=rwkv_g_up[l].astype(bf16), k_k=row2(rwkv_k_k[l]), k_a=row2(rwkv_k_a[l]),
            r_k=row2(rwkv_r_k[l]), ln_w=row2(rwkv_ln_w[l]), ln_b=row2(rwkv_ln_b[l]))
        up_pad = _pad_rows(gla_gk_up[l], LANES).astype(bf16)
        gk_bias = row2(gla_gk_bias[l])
        norm_w = row2(gla_norm_w[l])

        proj = in_proj(x, _pack_w_in(w_in[l]), _row_tile(x.shape[0], IN_PROJ_ROWS_TARGET), IN_PROJ_TN)

        y_gla_p, gla_p = gla_prompt(proj, up_pad, gk_bias, norm_w, n_batch, seq, MIXER_ROWS)
        y_rwkv_p, wkv_p = rwkv_prompt(proj, rp, n_batch, seq, MIXER_ROWS)
        k_rope, qb, kb, vb, sel = rope_prep(proj, cos_p, sin_p, n_batch, seq)
        y_moba_p = moba_prompt(qb, kb, vb, sel, n_batch, seq)

        y_gla_s, y_rwkv_s, q_s, k_s, gla_s, wkv_s_t = sample_step(
            proj, mp, n_seq, l, state_gla, state_wkv_t, state_shift, cos_s, sin_s,
            up_pad, gk_bias, norm_w, rp)
        v_s = proj[mp:, C_MV:C_MV + W]
        heads = (n_seq, MOBA_HEADS, MOBA_HEAD_DIM)
        q_lanes = jnp.broadcast_to(q_s[:, :, None], (n_seq, W, PAGE_SIZE))
        y_moba_s = moba_sample(pt_flat, q_lanes, q_s.reshape(heads), k_s.reshape(heads), v_s.reshape(n_seq, 1, W),
                               cache_kt, cache_vt, l, n_pages)

        y_gla = jnp.concatenate([y_gla_p, y_gla_s], axis=0)
        y_moba = jnp.concatenate([y_moba_p, y_moba_s.reshape(n_seq, W).astype(bf16)], axis=0)
        y_rwkv = jnp.concatenate([y_rwkv_p, y_rwkv_s], axis=0)

        x1 = merge_ln(x, proj, y_gla, y_moba, y_rwkv, w_branch[l].astype(bf16), b_gate[l], w_out[l].astype(bf16),
                      row2(ln1_g[l]), row2(ln1_b[l]), token_tm)
        x = mlp_ln(x1, w_up[l].astype(bf16), w_down[l].astype(bf16), row2(ln2_g[l]), row2(ln2_b[l]), token_tm)

        hd = (MOBA_HEADS, MOBA_HEAD_DIM)
        outs["k_p"].append(k_rope.reshape(n_batch, seq, *hd))
        outs["v_p"].append(proj[:mp, C_MV:C_MV + W].reshape(n_batch, seq, *hd))
        outs["k_s"].append(k_s.reshape(n_seq, 1, *hd))
        outs["v_s"].append(v_s.reshape(n_seq, 1, *hd))
        outs["gla_p"].append(gla_p)
        outs["gla_s"].append(gla_s)
        outs["wkv_p"].append(wkv_p)
        outs["wkv_s"].append(jnp.transpose(wkv_s_t, (3, 0, 1, 2)))
        outs["shift_p"].append(proj[seq - 1:mp:seq, C_PR:C_PR + RWKV_PROJ])
        outs["shift_s"].append(proj[mp:, C_PR:C_PR + RWKV_PROJ])

    st = lambda k: jnp.stack(outs[k])
    return (x[:mp].reshape(n_batch, seq, D_MODEL), x[mp:].reshape(n_seq, 1, D_MODEL),
            st("k_p"), st("v_p"), st("k_s"), st("v_s"), st("gla_p"), st("gla_s"),
            st("wkv_p"), st("wkv_s"), st("shift_p"), st("shift_s"))
```

```python
import functools

import jax
import jax.numpy as jnp
from jax import lax
from jax.experimental import pallas as pl
from jax.experimental.pallas import tpu as pltpu

f32, bf16 = jnp.float32, jnp.bfloat16

D_MODEL = 1024
DEPTH = 2
PAGE_SIZE = 128
BRANCH_WIDTH = D_MODEL // 2
N_BRANCH = 3
GLA_HEADS = 4
GLA_DV = BRANCH_WIDTH // GLA_HEADS
GLA_DK = GLA_DV // 2
GLA_GATE_RANK = 16
GLA_GATE_NORMALIZER = 16.0
GLA_CHUNK = 64
GLA_NORM_EPS = 1e-5
MOBA_HEAD_DIM = 64
MOBA_HEADS = BRANCH_WIDTH // MOBA_HEAD_DIM
MOBA_BLOCK = 256
MOBA_TOPK = 3
ROPE_THETA = 10000.0
RWKV_HEAD = 64
RWKV_HEADS = BRANCH_WIDTH // RWKV_HEAD
RWKV_DECAY_RANK = 64
RWKV_A_RANK = 64
RWKV_GATE_RANK = 128
RWKV_DECAY_SCALE = 0.606531
RWKV_GN_EPS = 64e-5
RWKV_PROJ = 3 * BRANCH_WIDTH + RWKV_DECAY_RANK + RWKV_A_RANK + RWKV_GATE_RANK
RWKV_CHUNK = 64
RWKV_INV_BLOCK = 16
D_FF = 4 * D_MODEL
ALPHA = (2 * DEPTH) ** 0.25
LN_EPS = 1e-5

LANES = 128
VMEM_LIMIT_BYTES = 56 * 1024 * 1024

W = BRANCH_WIDTH
C_GQ, C_GK, C_GV, C_GO = 0, 256, 512, 1024
C_MQ, C_MK, C_MV = 1536, 2048, 2560
C_PG = 3072
C_PR = 6144
C_R, C_K7, C_V7 = C_PR, C_PR + W, C_PR + 2 * W
C_WA = C_PR + 3 * W
C_GR = C_WA + LANES
C_GLOW = C_PR + RWKV_PROJ
N_PACK = C_GLOW + LANES

_O_GQ, _O_GK, _O_GV, _O_GLOW, _O_GO = 0, 256, 512, 1024, 1040
_O_MQ, _O_MK, _O_MV, _O_PR, _O_PG = 1552, 2064, 2576, 3088, 4880

NT_DIMS = (((1,), (1,)), ((), ()))
TN_DIMS = (((0,), (0,)), ((), ()))


def _params(sem):
    return pltpu.CompilerParams(dimension_semantics=sem, vmem_limit_bytes=VMEM_LIMIT_BYTES)


def _bdot(a, b):
    return jnp.dot(a.astype(bf16), b.astype(bf16), preferred_element_type=f32)


def _bdot_nt(a, b):
    return lax.dot_general(a.astype(bf16), b.astype(bf16), NT_DIMS, preferred_element_type=f32)


def _bdot_tn(a, b):
    return lax.dot_general(a.astype(bf16), b.astype(bf16), TN_DIMS, preferred_element_type=f32)


def _split2(x):
    hi = x.astype(bf16)
    lo = (x - hi.astype(f32)).astype(bf16)
    return hi, lo


def _dot_exact_rhs(x, m_bf16):
    hi, lo = _split2(x)
    return (jnp.dot(hi, m_bf16, preferred_element_type=f32)
            + jnp.dot(lo, m_bf16, preferred_element_type=f32))


def _dot_exact_lhs(m_bf16, x):
    hi, lo = _split2(x)
    return (jnp.dot(m_bf16, hi, preferred_element_type=f32)
            + jnp.dot(m_bf16, lo, preferred_element_type=f32))


def _dot3_nt(a, b):
    ah, al = _split2(a)
    bh, bl = _split2(b)
    d = lambda u, v: lax.dot_general(u, v, NT_DIMS, preferred_element_type=f32)
    return d(ah, bh) + d(al, bh) + d(ah, bl)


def _iota(shape, dim):
    return lax.broadcasted_iota(jnp.int32, shape, dim)


def _seg_matrix(n, seg):
    return (_iota((n, n), 0) // seg == _iota((n, n), 1) // seg).astype(bf16)


def _layer_norm(z, g, b):
    mu = jnp.mean(z, -1, keepdims=True)
    d = z - mu
    var = jnp.mean(d * d, -1, keepdims=True)
    return d * lax.rsqrt(var + LN_EPS) * g + b


def _row_to_col(row, n):
    eye = _iota((n, n), 0) == _iota((n, n), 1)
    return jnp.sum(jnp.where(eye, jnp.broadcast_to(row, (n, n)), 0.0), axis=1, keepdims=True)


def _set_row(ref, idx, row):
    cur = ref[...]
    ref[...] = jnp.where(_iota(cur.shape, 0) == idx, jnp.broadcast_to(row, cur.shape), cur)


def _col_to_row(col, n):
    eye = _iota((n, n), 0) == _iota((n, n), 1)
    return jnp.sum(jnp.where(eye, jnp.broadcast_to(col, (n, n)), 0.0), axis=0, keepdims=True)


def _inproj_body(x_ref, w_ref, o_ref, xb_ref):
    @pl.when(pl.program_id(1) == 0)
    def _():
        xb_ref[...] = x_ref[...].astype(bf16)

    o_ref[...] = jnp.dot(xb_ref[...], w_ref[...], preferred_element_type=f32)


def in_proj(x, w, tm, tn):
    m, k = x.shape
    n = w.shape[1]
    return pl.pallas_call(
        _inproj_body,
        grid=(m // tm, n // tn),
        in_specs=[pl.BlockSpec((tm, k), lambda i, j: (i, 0)),
                  pl.BlockSpec((k, tn), lambda i, j: (0, j))],
        out_specs=pl.BlockSpec((tm, tn), lambda i, j: (i, j)),
        out_shape=jax.ShapeDtypeStruct((m, n), f32),
        scratch_shapes=[pltpu.VMEM((tm, k), bf16)],
        compiler_params=_params(("arbitrary", "arbitrary")),
        name="in_proj",
    )(x, w)


def _merge_body(x_ref, pg0_ref, pg1_ref, pg2_ref, y0_ref, y1_ref, y2_ref, wb_ref, bg_ref, wo_ref,
                g_ref, b_ref, o_ref):
    merged = None
    for n, (pg_ref, y_ref) in enumerate(((pg0_ref, y0_ref), (pg1_ref, y1_ref), (pg2_ref, y2_ref))):
        gate = jax.nn.sigmoid(pg_ref[...] + bg_ref[n:n + 1, :])
        term = gate * jnp.dot(y_ref[...], wb_ref[n], preferred_element_type=f32)
        merged = term if merged is None else merged + term
    out = jnp.dot(merged.astype(bf16), wo_ref[...], preferred_element_type=f32)
    o_ref[...] = _layer_norm(ALPHA * x_ref[...] + out, g_ref[...], b_ref[...])


def merge_ln(x, proj, y_gla, y_moba, y_rwkv, w_branch, b_gate, w_out, g, b, tm):
    m = x.shape[0]
    row = lambda i: (i, 0)
    const2 = lambda i: (0, 0)
    pg_spec = lambda n: pl.BlockSpec((tm, D_MODEL), lambda i, n=n: (i, C_PG // D_MODEL + n))
    y_spec = pl.BlockSpec((tm, W), row)
    return pl.pallas_call(
        _merge_body,
        grid=(m // tm,),
        in_specs=[pl.BlockSpec((tm, D_MODEL), row), pg_spec(0), pg_spec(1), pg_spec(2),
                  y_spec, y_spec, y_spec,
                  pl.BlockSpec((N_BRANCH, W, D_MODEL), lambda i: (0, 0, 0)),
                  pl.BlockSpec((N_BRANCH, D_MODEL), const2),
                  pl.BlockSpec((D_MODEL, D_MODEL), const2),
                  pl.BlockSpec((1, D_MODEL), const2), pl.BlockSpec((1, D_MODEL), const2)],
        out_specs=pl.BlockSpec((tm, D_MODEL), row),
        out_shape=jax.ShapeDtypeStruct((m, D_MODEL), f32),
        compiler_params=_params(("arbitrary",)),
        name="merge_ln",
    )(x, proj, proj, proj, y_gla, y_moba, y_rwkv, w_branch, b_gate, w_out, g, b)


FF_CHUNK = 1024


def _mlp_body(x_ref, wu_ref, wd_ref, g_ref, b_ref, o_ref):
    x = x_ref[...]
    xb = x.astype(bf16)
    acc = None
    for c in range(D_FF // FF_CHUNK):
        h = jnp.dot(xb, wu_ref[:, c * FF_CHUNK:(c + 1) * FF_CHUNK], preferred_element_type=f32)
        h = jnp.square(jnp.maximum(h, 0.0)).astype(bf16)
        part = jnp.dot(h, wd_ref[c * FF_CHUNK:(c + 1) * FF_CHUNK, :], preferred_element_type=f32)
        acc = part if acc is None else acc + part
    o_ref[...] = _layer_norm(ALPHA * x + acc, g_ref[...], b_ref[...])


def mlp_ln(x, w_up, w_down, g, b, tm):
    m = x.shape[0]
    row = lambda i: (i, 0)
    const2 = lambda i: (0, 0)
    return pl.pallas_call(
        _mlp_body,
        grid=(m // tm,),
        in_specs=[pl.BlockSpec((tm, D_MODEL), row),
                  pl.BlockSpec((D_MODEL, D_FF), const2, pipeline_mode=pl.Buffered(1)),
                  pl.BlockSpec((D_FF, D_MODEL), const2, pipeline_mode=pl.Buffered(1)),
                  pl.BlockSpec((1, D_MODEL), const2), pl.BlockSpec((1, D_MODEL), const2)],
        out_specs=pl.BlockSpec((tm, D_MODEL), row),
        out_shape=jax.ShapeDtypeStruct((m, D_MODEL), f32),
        compiler_params=_params(("arbitrary",)),
        name="mlp_ln",
    )(x, w_up, w_down, g, b)


def _gla_log_gate(glow, up_ref, bias_ref):
    z = _bdot(glow, up_ref[...]) + bias_ref[...]
    return jax.nn.log_sigmoid(z) * (1.0 / GLA_GATE_NORMALIZER)


def _gla_out(o, norm_w, g_out):
    o = o * lax.rsqrt(jnp.mean(jnp.square(o), -1, keepdims=True) + GLA_NORM_EPS) * norm_w
    return o * jax.nn.silu(g_out)


def _gla_prompt_body(q_ref, k_ref, v_ref, go_ref, glow_ref, up_ref, bias_ref, nw_ref,
                     y_ref, s_out_ref, s_ref):
    t = pl.program_id(1)
    rows = q_ref.shape[0]
    c_sz = GLA_CHUNK

    @pl.when(t == 0)
    def _():
        s_ref[...] = jnp.zeros_like(s_ref)

    lg = _gla_log_gate(glow_ref[...], up_ref, bias_ref)
    tril = (_iota((c_sz, c_sz), 1) <= _iota((c_sz, c_sz), 0))
    tril_b = tril.astype(bf16)
    n_chunks = rows // c_sz
    probs = []
    for c in range(n_chunks):
        rs = slice(c * c_sz, (c + 1) * c_sz)
        g_cum = _dot_exact_lhs(tril_b, lg[rs])
        g_last = g_cum[c_sz - 1:c_sz, :]
        q_dec = q_ref[rs, :] * (GLA_DK ** -0.5) * jnp.exp(g_cum)
        k = k_ref[rs, :]
        k_inv = k * jnp.exp(-g_cum)
        k_tail = k * jnp.exp(g_last - g_cum)
        e_last = jnp.exp(g_last)
        for h in range(GLA_HEADS):
            ks = slice(h * GLA_DK, (h + 1) * GLA_DK)
            vs = slice(h * GLA_DV, (h + 1) * GLA_DV)
            probs.append((q_dec[:, ks], k_inv[:, ks], k_tail[:, ks], e_last[:, ks], v_ref[rs, vs], rs, vs))
    a = [jnp.where(tril, _bdot_nt(q[0], q[1]), 0.0) for q in probs]
    o_intra = [_bdot(x, q[4]) for x, q in zip(a, probs)]
    kv = [_bdot_tn(q[2], q[4]) for q in probs]
    e_col = [_row_to_col(q[3], GLA_DK) for q in probs]

    state = [s_ref[h] for h in range(GLA_HEADS)]
    for c in range(n_chunks):
        sl = slice(c * GLA_HEADS, (c + 1) * GLA_HEADS)
        o = [oi + _bdot(q[0], s) for oi, q, s in zip(o_intra[sl], probs[sl], state)]
        state = [s * e + x for s, e, x in zip(state, e_col[sl], kv[sl])]
        for oo, q in zip(o, probs[sl]):
            rs, vs = q[5], q[6]
            y_ref[rs, vs] = _gla_out(oo, nw_ref[...], go_ref[rs, vs]).astype(y_ref.dtype)
    for h in range(GLA_HEADS):
        s_ref[h] = state[h]
    s_out_ref[...] = s_ref[...]


def gla_prompt(proj, up_pad, bias, norm_w, n_batch, seq, rt):
    nt = seq // rt
    rowmap = lambda width, col: pl.BlockSpec((rt, width), lambda b, t: (b * nt + t, col // width))
    const2 = lambda b, t: (0, 0)
    return pl.pallas_call(
        _gla_prompt_body,
        grid=(n_batch, nt),
        in_specs=[rowmap(GLA_HEADS * GLA_DK, C_GQ), rowmap(GLA_HEADS * GLA_DK, C_GK), rowmap(W, C_GV),
                  rowmap(W, C_GO), rowmap(LANES, C_GLOW),
                  pl.BlockSpec((LANES, GLA_HEADS * GLA_DK), const2),
                  pl.BlockSpec((1, GLA_HEADS * GLA_DK), const2),
                  pl.BlockSpec((1, GLA_DV), const2)],
        out_specs=[pl.BlockSpec((rt, W), lambda b, t: (b * nt + t, 0)),
                   pl.BlockSpec((None, GLA_HEADS, GLA_DK, GLA_DV), lambda b, t: (b, 0, 0, 0))],
        out_shape=[jax.ShapeDtypeStruct((n_batch * seq, W), bf16),
                   jax.ShapeDtypeStruct((n_batch, GLA_HEADS, GLA_DK, GLA_DV), f32)],
        scratch_shapes=[pltpu.VMEM((GLA_HEADS, GLA_DK, GLA_DV), f32)],
        compiler_params=_params(("arbitrary", "arbitrary")),
        name="gla_prompt",
    )(proj, proj, proj, proj, proj, up_pad, bias, norm_w)


def _rwkv_pre(r_s, k_s, v_s, wa_s, gr_s, p):
    seg = _seg_matrix(W, RWKV_HEAD)
    logw = -RWKV_DECAY_SCALE * jax.nn.sigmoid(p["w0"][...] + _bdot(jnp.tanh(wa_s), p["w_up"][...]))
    a = jax.nn.sigmoid(p["a0"][...] + _bdot(wa_s, p["a_up"][...]))
    g7 = _bdot(jax.nn.sigmoid(gr_s), p["g_up"][...])
    kk = k_s * p["k_k"][...]
    kk = kk * lax.rsqrt(_dot_exact_rhs(jnp.square(kk), seg) + 1e-12)
    k7 = k_s * (1.0 + (a - 1.0) * p["k_a"][...])
    bonus = _dot_exact_rhs(r_s * k7 * p["r_k"][...], seg) * v_s
    return dict(r=r_s, logw=logw, k=k7, v=v_s, kk=kk, a=a, g7=g7, bonus=bonus)


def _rwkv_post(yw, bonus, g7, p):
    seg = _seg_matrix(W, RWKV_HEAD)
    inv_n = 1.0 / RWKV_HEAD
    mu = _dot_exact_rhs(yw, seg) * inv_n
    d = yw - mu
    var = _dot_exact_rhs(jnp.square(d), seg) * inv_n
    yn = d * lax.rsqrt(var + RWKV_GN_EPS) * p["ln_w"][...] + p["ln_b"][...]
    return (yn + bonus) * g7


def _inv_unit_lower(n_mats):
    c = n_mats[0].shape[0]
    row, col = _iota((c, c), 0), _iota((c, c), 1)
    eye = (row == col).astype(f32)
    blk = (row // RWKV_INV_BLOCK) == (col // RWKV_INV_BLOCK)
    d = [jnp.where(blk, n, 0.0) for n in n_mats]
    lo = [n - x for n, x in zip(n_mats, d)]
    d2 = [_bdot(x, x) for x in d]
    d4 = [_bdot(x, x) for x in d2]
    d8 = [_bdot(x, x) for x in d4]
    pa = [_bdot(eye - x, eye + y) for x, y in zip(d, d2)]
    pb = [_bdot(eye + x, eye + y) for x, y in zip(d4, d8)]
    d_inv = [_bdot(x, y) for x, y in zip(pa, pb)]
    e = [_bdot(x, y) for x, y in zip(d_inv, lo)]
    e2 = [_bdot(x, x) for x in e]
    f = [_bdot(eye - x, eye + y) for x, y in zip(e, e2)]
    return [_bdot(x, y) for x, y in zip(f, d_inv)]


def _rwkv_chunk_coeffs(probs):
    c, kd = probs[0][0].shape
    row, col = _iota((2 * c, 2 * c), 0), _iota((2 * c, 2 * c), 1)
    mask = (col % c < row % c) | ((row >= c) & (col % c == row % c))
    eye_k = _iota((kd, kd), 0) == _iota((kd, kd), 1)
    prod = [jnp.where(mask, _bdot_nt(jnp.concatenate([a_t, r_t], 0), jnp.concatenate([b_t, k_t], 0)), 0.0)
            for r_t, a_t, b_t, k_t, _, _, _, _ in probs]
    t_inv = _inv_unit_lower([p[:c, :c] for p in prod])
    av = [_bdot(p[:c, c:], q[6]) for p, q in zip(prod, probs)]
    x = [_bdot(ti, jnp.concatenate([q[1], a], 1)) for ti, q, a in zip(t_inv, probs, av)]
    px = [_bdot(p[c:, :c], xx) for p, xx in zip(prod, x)]
    prkv = [_bdot(p[c:, c:], q[6]) for p, q in zip(prod, probs)]
    xb = [_bdot_tn(xx, q[4]) for xx, q in zip(x, probs)]
    vk = [_bdot_tn(q[6], q[5]) for q in probs]
    out = []
    for q, pxx, pv, xbb, vkk in zip(probs, px, prkv, xb, vk):
        tm_t = jnp.where(eye_k, jnp.broadcast_to(q[7], (kd, kd)), 0.0) - xbb[:kd]
        out.append((q[0] - pxx[:, :kd], pv - pxx[:, kd:], tm_t, vkk - xbb[kd:]))
    return out


_RWKV_PARAM_NAMES = ("mu_r", "mu_k", "mu_v", "mu_wa", "mu_gr", "w0", "w_up", "a0", "a_up", "g_up",
                     "k_k", "k_a", "r_k", "ln_w", "ln_b")


def _rwkv_prompt_body(r_ref, k_ref, v_ref, wa_ref, gr_ref, *rest):
    np_ = len(_RWKV_PARAM_NAMES)
    p = dict(zip(_RWKV_PARAM_NAMES, rest[:np_]))
    y_ref, s_out_ref, shift_out_ref = rest[np_:np_ + 3]
    s_ref, prev_ref = rest[np_ + 3:]
    t = pl.program_id(1)
    rows = r_ref.shape[0]
    c_sz = RWKV_CHUNK

    @pl.when(t == 0)
    def _():
        s_ref[...] = jnp.zeros_like(s_ref)
        prev_ref[...] = jnp.zeros_like(prev_ref)

    def shifted(x_ref, mu_ref, c0):
        x = x_ref[...]
        width = x.shape[1]
        prev = jnp.where(_iota(x.shape, 0) == 0, prev_ref[0:1, c0:c0 + width], pltpu.roll(x, 1, 0))
        prev_ref[0:1, c0:c0 + width] = x[rows - 1:rows, :]
        return x + mu_ref[...] * (prev - x)

    pre = _rwkv_pre(shifted(r_ref, p["mu_r"], 0), shifted(k_ref, p["mu_k"], W), shifted(v_ref, p["mu_v"], 2 * W),
                    shifted(wa_ref, p["mu_wa"], 3 * W), shifted(gr_ref, p["mu_gr"], 3 * W + LANES), p)
    tril_b = (_iota((c_sz, c_sz), 1) <= _iota((c_sz, c_sz), 0)).astype(bf16)
    n_chunks = rows // c_sz
    b_all = pre["kk"] * pre["a"]

    probs = []
    for c in range(n_chunks):
        rs = slice(c * c_sz, (c + 1) * c_sz)
        lw = pre["logw"][rs]
        g_cum = _dot_exact_lhs(tril_b, lw)
        g_last = g_cum[c_sz - 1:c_sz, :]
        e_ng = jnp.exp(-g_cum)
        e_gl = jnp.exp(g_last - g_cum)
        kk, b, k, v = pre["kk"][rs], b_all[rs], pre["k"][rs], pre["v"][rs]
        r_t = pre["r"][rs] * jnp.exp(g_cum)
        a_t = kk * jnp.exp(g_cum - lw)
        b_t, k_t, b_g, k_g = b * e_ng, k * e_ng, b * e_gl, k * e_gl
        gam = jnp.exp(g_last)
        for h in range(RWKV_HEADS):
            hs = slice(h * RWKV_HEAD, (h + 1) * RWKV_HEAD)
            probs.append((r_t[:, hs], a_t[:, hs], b_t[:, hs], k_t[:, hs], b_g[:, hs], k_g[:, hs], v[:, hs],
                          gam[:, hs]))
    coeffs = _rwkv_chunk_coeffs(probs)

    state = [s_ref[h] for h in range(RWKV_HEADS)]
    y_rows = []
    for c in range(n_chunks):
        cf = coeffs[c * RWKV_HEADS:(c + 1) * RWKV_HEADS]
        y_rows.append(jnp.concatenate([_bdot_nt(q[0], s) + q[1] for q, s in zip(cf, state)], axis=1))
        state = [_bdot(s, q[2]) + q[3] for q, s in zip(cf, state)]
    for h in range(RWKV_HEADS):
        s_ref[h] = state[h]
    yw = jnp.concatenate(y_rows, axis=0)
    y_ref[...] = _rwkv_post(yw, pre["bonus"], pre["g7"], p).astype(y_ref.dtype)
    s_out_ref[...] = s_ref[...]
    shift_out_ref[...] = prev_ref[...]


def _rwkv_param_specs(const_map):
    vec = lambda width: pl.BlockSpec((1, width), const_map)
    mat = lambda r: pl.BlockSpec((r, W), const_map)
    return [vec(W), vec(W), vec(W), vec(LANES), vec(LANES), vec(W), mat(LANES), vec(W), mat(LANES),
            mat(RWKV_GATE_RANK), vec(W), vec(W), vec(W), vec(W), vec(W)]


def rwkv_prompt(proj, params, n_batch, seq, rt):
    nt = seq // rt
    rowmap = lambda width, col: pl.BlockSpec((rt, width), lambda b, t: (b * nt + t, col // width))
    const2 = lambda b, t: (0, 0)
    return pl.pallas_call(
        _rwkv_prompt_body,
        grid=(n_batch, nt),
        in_specs=[rowmap(W, C_R), rowmap(W, C_K7), rowmap(W, C_V7), rowmap(LANES, C_WA), rowmap(LANES, C_GR)]
        + _rwkv_param_specs(const2),
        out_specs=[pl.BlockSpec((rt, W), lambda b, t: (b * nt + t, 0)),
                   pl.BlockSpec((None, RWKV_HEADS, RWKV_HEAD, RWKV_HEAD), lambda b, t: (b, 0, 0, 0)),
                   pl.BlockSpec((None, SUBLANES, RWKV_PROJ), lambda b, t: (b, 0, 0))],
        out_shape=[jax.ShapeDtypeStruct((n_batch * seq, W), bf16),
                   jax.ShapeDtypeStruct((n_batch, RWKV_HEADS, RWKV_HEAD, RWKV_HEAD), f32),
                   jax.ShapeDtypeStruct((n_batch, SUBLANES, RWKV_PROJ), f32)],
        scratch_shapes=[pltpu.VMEM((RWKV_HEADS, RWKV_HEAD, RWKV_HEAD), f32),
                        pltpu.VMEM((SUBLANES, RWKV_PROJ), f32)],
        compiler_params=_params(("arbitrary", "arbitrary")),
        name="rwkv_prompt",
    )(proj, proj, proj, proj, proj, *[params[n] for n in _RWKV_PARAM_NAMES])


def _rope(x, cos, sin_signed):
    lane = _iota(x.shape, 1)
    half = MOBA_HEAD_DIM // 2
    width = x.shape[1]
    swapped = jnp.where(lane % MOBA_HEAD_DIM < half, pltpu.roll(x, width - half, 1), pltpu.roll(x, half, 1))
    return x * cos + swapped * sin_signed


def _sample_body(gq_ref, gk_ref, gv_ref, go_ref, glow_ref, r_ref, k_ref, v_ref, wa_ref, gr_ref, mq_ref, mk_ref,
                 sg_ref, sw_ref, shift_ref, cos_ref, sin_ref, up_ref, bias_ref, nw_ref, *rest):
    np_ = len(_RWKV_PARAM_NAMES)
    p = dict(zip(_RWKV_PARAM_NAMES, rest[:np_]))
    yg_ref, yr_ref, qs_ref, ks_ref, sg_out, sw_out = rest[np_:np_ + 6]
    qt_s, kt_s, egt_s, og_s, rt_s, wt_s, k7t_s, vt_s, kkt_s, bt_s, yt_s, bonus_s, g7_s = rest[np_ + 6:]
    hp = pl.program_id(0)
    n_seq = gq_ref.shape[0]

    @pl.when(hp == 0)
    def _():
        qs_ref[...] = _rope(mq_ref[...], cos_ref[...], sin_ref[...]) * (MOBA_HEAD_DIM ** -0.5)
        ks_ref[...] = _rope(mk_ref[...], cos_ref[...], sin_ref[...])
        lg = _gla_log_gate(glow_ref[...], up_ref, bias_ref)
        qt_s[...] = (gq_ref[...] * (GLA_DK ** -0.5)).T
        kt_s[...] = gk_ref[...].T
        egt_s[...] = jnp.exp(lg).T

        def shifted(x_ref, mu_ref, c0):
            x = x_ref[...]
            return x + mu_ref[...] * (shift_ref[:, c0:c0 + x.shape[1]] - x)

        pre = _rwkv_pre(shifted(r_ref, p["mu_r"], 0), shifted(k_ref, p["mu_k"], W), shifted(v_ref, p["mu_v"], 2 * W),
                        shifted(wa_ref, p["mu_wa"], 3 * W), shifted(gr_ref, p["mu_gr"], 3 * W + LANES), p)
        rt_s[...] = pre["r"].T
        wt_s[...] = jnp.exp(pre["logw"]).T
        k7t_s[...] = pre["k"].T
        vt_s[...] = pre["v"].T
        kkt_s[...] = pre["kk"].T
        bt_s[...] = (pre["kk"] * pre["a"]).T
        bonus_s[...] = pre["bonus"]
        g7_s[...] = pre["g7"]
        og_s[...] = jnp.zeros_like(og_s)

    ks = pl.ds(pl.multiple_of(hp * GLA_DK, GLA_DK), GLA_DK)
    eg_t, k_t, q_t = egt_s[ks, :], kt_s[ks, :], qt_s[ks, :]
    gv_h = gv_ref[:, pl.ds(pl.multiple_of(hp * GLA_DV, GLA_DV), GLA_DV)]
    lane_grp = _iota((n_seq, W), 1) // GLA_DV
    o_rows = []
    for s in range(n_seq):
        col = lambda x: jnp.broadcast_to(x[:, s:s + 1], (GLA_DK, GLA_DV))
        st = sg_ref[s] * col(eg_t) + col(k_t) * gv_h[s:s + 1, :]
        sg_out[s] = st
        o_rows.append(jnp.sum(col(q_t) * st, axis=0, keepdims=True))
    o_h = jnp.concatenate(o_rows, axis=0)
    og_s[...] = jnp.where(lane_grp == hp, jnp.concatenate([o_h] * GLA_HEADS, axis=1), og_s[...])

    group = RWKV_HEAD // SUBLANES
    for u in range(2):
        hs = pl.ds(pl.multiple_of((2 * hp + u) * RWKV_HEAD, RWKV_HEAD), RWKV_HEAD)
        neg_kk, w_t, b_t, k_t7, r_t = -kkt_s[hs, :], wt_s[hs, :], bt_s[hs, :], k7t_s[hs, :], rt_s[hs, :]

        def v_group(g, carry, u=u, hs=hs, neg_kk=neg_kk, w_t=w_t, b_t=b_t, k_t7=k_t7, r_t=r_t):
            base = pl.multiple_of((2 * hp + u) * RWKV_HEAD + g * SUBLANES, SUBLANES)
            v_rows = vt_s[pl.ds(base, SUBLANES), :]
            y_rows = []
            for j in range(SUBLANES):
                vi = g * SUBLANES + j
                st = sw_ref[u, vi]
                sa = jnp.sum(st * neg_kk, axis=0, keepdims=True)
                st = st * w_t + sa * b_t + v_rows[j:j + 1, :] * k_t7
                sw_out[u, vi] = st
                y_rows.append(jnp.sum(st * r_t, axis=0, keepdims=True))
            yt_s[pl.ds(base, SUBLANES), :] = jnp.concatenate(y_rows, axis=0)
            return carry

        lax.fori_loop(0, group, v_group, 0)

    @pl.when(hp == pl.num_programs(0) - 1)
    def _():
        og = og_s[...]
        for h in range(GLA_HEADS):
            vs = slice(h * GLA_DV, (h + 1) * GLA_DV)
            yg_ref[:, vs] = _gla_out(og[:, vs], nw_ref[...], go_ref[:, vs]).astype(yg_ref.dtype)
        yr_ref[...] = _rwkv_post(yt_s[...].T, bonus_s[...], g7_s[...], p).astype(yr_ref.dtype)


def sample_step(proj, row0, n_seq, layer, state_gla, state_wkv_t, state_shift, cos_s, sin_s,
                up_pad, bias, norm_w, params):
    r0 = row0 // n_seq
    rowmap = lambda width, col: pl.BlockSpec((n_seq, width), lambda h: (r0, col // width))
    const2 = lambda h: (0, 0)
    hk = GLA_HEADS * GLA_DK
    out_row = pl.BlockSpec((n_seq, W), const2)
    tr = lambda rows: pltpu.VMEM((rows, n_seq), f32)
    rw = lambda: pltpu.VMEM((n_seq, W), f32)
    return pl.pallas_call(
        _sample_body,
        grid=(GLA_HEADS,),
        in_specs=[rowmap(hk, C_GQ), rowmap(hk, C_GK), rowmap(W, C_GV), rowmap(W, C_GO), rowmap(LANES, C_GLOW),
                  rowmap(W, C_R), rowmap(W, C_K7), rowmap(W, C_V7), rowmap(LANES, C_WA), rowmap(LANES, C_GR),
                  rowmap(W, C_MQ), rowmap(W, C_MK),
                  pl.BlockSpec((None, n_seq, None, GLA_DK, GLA_DV), lambda h: (layer, 0, h, 0, 0)),
                  pl.BlockSpec((None, 2, RWKV_HEAD, RWKV_HEAD, n_seq), lambda h: (layer, h, 0, 0, 0)),
                  pl.BlockSpec((None, n_seq, RWKV_PROJ), lambda h: (layer, 0, 0)),
                  pl.BlockSpec((1, W), const2), pl.BlockSpec((1, W), const2),
                  pl.BlockSpec((LANES, hk), const2), pl.BlockSpec((1, hk), const2), pl.BlockSpec((1, GLA_DV), const2)]
        + _rwkv_param_specs(const2),
        out_specs=[out_row, out_row, out_row, out_row,
                   pl.BlockSpec((n_seq, None, GLA_DK, GLA_DV), lambda h: (0, h, 0, 0)),
                   pl.BlockSpec((2, RWKV_HEAD, RWKV_HEAD, n_seq), lambda h: (h, 0, 0, 0))],
        out_shape=[jax.ShapeDtypeStruct((n_seq, W), bf16), jax.ShapeDtypeStruct((n_seq, W), bf16),
                   jax.ShapeDtypeStruct((n_seq, W), f32), jax.ShapeDtypeStruct((n_seq, W), f32),
                   jax.ShapeDtypeStruct((n_seq, GLA_HEADS, GLA_DK, GLA_DV), f32),
                   jax.ShapeDtypeStruct((RWKV_HEADS, RWKV_HEAD, RWKV_HEAD, n_seq), f32)],
        scratch_shapes=[tr(hk), tr(hk), tr(hk), rw(), tr(W), tr(W), tr(W), tr(W), tr(W), tr(W), tr(W), rw(), rw()],
        compiler_params=_params(("arbitrary",)),
        name="sample_step",
    )(proj, proj, proj, proj, proj, proj, proj, proj, proj, proj, proj, proj,
      state_gla, state_wkv_t, state_shift, cos_s, sin_s, up_pad, bias, norm_w,
      *[params[n] for n in _RWKV_PARAM_NAMES])


def _topk_select(gate, n_valid, axis_len, axis):
    idx = _iota(gate.shape, axis)
    cnt = jnp.zeros(gate.shape, f32)
    for m in range(axis_len):
        g_m = lax.slice_in_dim(gate, m, m + 1, axis=axis)
        g_m = jnp.broadcast_to(g_m, gate.shape)
        beats = (g_m > gate) | ((g_m == gate) & (m < idx))
        cnt = cnt + jnp.where(beats, 1.0, 0.0) * jnp.where(m < n_valid, 1.0, 0.0)
    return (idx < n_valid) & (cnt < MOBA_TOPK)


def _rope_prep_body(mq_ref, mk_ref, mv_ref, cos_ref, sin_ref, k_out, qb_out, kb_out, vb_out, sel_out, km_ref):
    i = pl.program_id(1)
    blk = mq_ref.shape[0]
    nb = km_ref.shape[0]
    nh = MOBA_HEADS

    @pl.when(i == 0)
    def _():
        km_ref[...] = jnp.zeros_like(km_ref)

    q = _rope(mq_ref[...], cos_ref[...], sin_ref[...]) * (MOBA_HEAD_DIM ** -0.5)
    k = _rope(mk_ref[...], cos_ref[...], sin_ref[...])
    k_out[...] = k
    qb_out[...] = q.astype(bf16)
    kb_out[...] = k.astype(bf16)
    vb_out[...] = mv_ref[...].astype(bf16)

    km = km_ref[...]
    km_rows = jnp.concatenate([km] * nh, axis=0)
    head_of_row = _iota(km_rows.shape, 0) // nb
    km_rows = jnp.where(head_of_row == _iota(km_rows.shape, 1) // MOBA_HEAD_DIM, km_rows, 0.0)
    gate = _dot3_nt(km_rows, q).reshape(nh, nb, blk)
    sel = _topk_select(gate, i, nb, 1).astype(bf16).reshape(nh * nb, blk)
    sel = jnp.concatenate([sel, jnp.zeros((LANES - nh * nb, blk), bf16)], axis=0)
    eye = (_iota((blk, blk), 0) == _iota((blk, blk), 1)).astype(bf16)
    sel_out[...] = lax.dot_general(eye, sel, NT_DIMS, preferred_element_type=f32)

    _set_row(km_ref, i, jnp.sum(k, axis=0, keepdims=True) * (1.0 / blk))


def rope_prep(proj, cos, sin, n_batch, seq):
    blk = MOBA_BLOCK
    nb = seq // blk
    m = n_batch * seq
    rowmap = lambda col: pl.BlockSpec((blk, W), lambda b, i: (b * nb + i, col // W))
    tab = pl.BlockSpec((blk, W), lambda b, i: (i, 0))
    out = lambda width: pl.BlockSpec((blk, width), lambda b, i: (b * nb + i, 0))
    return pl.pallas_call(
        _rope_prep_body,
        grid=(n_batch, nb),
        in_specs=[rowmap(C_MQ), rowmap(C_MK), rowmap(C_MV), tab, tab],
        out_specs=[out(W), out(W), out(W), out(W), out(LANES)],
        out_shape=[jax.ShapeDtypeStruct((m, W), f32), jax.ShapeDtypeStruct((m, W), bf16),
                   jax.ShapeDtypeStruct((m, W), bf16), jax.ShapeDtypeStruct((m, W), bf16),
                   jax.ShapeDtypeStruct((m, LANES), f32)],
        scratch_shapes=[pltpu.VMEM((nb, W), f32)],
        compiler_params=_params(("arbitrary", "arbitrary")),
        name="rope_prep",
    )(proj, proj, proj, cos, sin)


def _moba_prompt_body(qb_ref, kb_ref, vb_ref, sel_ref, y_ref, m_ref, acc_ref):
    i = pl.program_id(1)
    blk = qb_ref.shape[0]
    nb = kb_ref.shape[0] // blk
    lane = _iota((blk, LANES), 1)
    low_half = lane < MOBA_HEAD_DIM
    causal = _iota((blk, blk), 1) <= _iota((blk, blk), 0)
    neg_inf = -jnp.inf

    def head_q(pair, u):
        qp = qb_ref[:, pair * LANES:(pair + 1) * LANES]
        return jnp.where(low_half if u == 0 else ~low_half, qp, jnp.zeros_like(qp))

    def kv(n, pair):
        rs = pl.ds(pl.multiple_of(n * blk, blk), blk)
        ls = slice(pair * LANES, (pair + 1) * LANES)
        return kb_ref[rs, ls], vb_ref[rs, ls]

    wide = lambda x: jnp.concatenate([x, x], axis=1)

    def head_v(v_n, u):
        return jnp.where(low_half if u == 0 else ~low_half, v_n, jnp.ones_like(v_n))

    for pair in range(MOBA_HEADS // 2):
        k_n, v_n = kv(i, pair)
        for u in range(2):
            h = 2 * pair + u
            s = jnp.where(causal, lax.dot_general(head_q(pair, u), k_n, NT_DIMS, preferred_element_type=f32), neg_inf)
            m = jnp.broadcast_to(jnp.max(s, axis=1, keepdims=True), (blk, LANES))
            p = jnp.exp(s - wide(m))
            m_ref[h] = m
            acc_ref[h] = jnp.dot(p.astype(bf16), head_v(v_n, u), preferred_element_type=f32)

    sel_b = sel_ref[...].astype(bf16)
    one_hot_row = _iota((LANES, LANES), 0)

    def past(n, carry):
        heads = [(pair, u) for pair in range(MOBA_HEADS // 2) for u in range(2)]
        kvs = [kv(n, pair) for pair in range(MOBA_HEADS // 2)]
        chosen = [jnp.dot(sel_b, (one_hot_row == (2 * pair + u) * nb + n).astype(bf16), preferred_element_type=f32)
                  for pair, u in heads]
        s = [lax.dot_general(head_q(pair, u), kvs[pair][0], NT_DIMS, preferred_element_type=f32)
             for pair, u in heads]
        s = [jnp.where(wide(c) > 0.5, x, neg_inf) for c, x in zip(chosen, s)]
        m_old = [m_ref[h] for h in range(MOBA_HEADS)]
        m_new = [jnp.maximum(mo, jnp.broadcast_to(jnp.max(x, axis=1, keepdims=True), (blk, LANES)))
                 for mo, x in zip(m_old, s)]
        p = [jnp.exp(x - wide(mn)).astype(bf16) for x, mn in zip(s, m_new)]
        pv = [jnp.dot(pp, head_v(kvs[pair][1], u), preferred_element_type=f32) for pp, (pair, u) in zip(p, heads)]
        for h in range(MOBA_HEADS):
            m_ref[h] = m_new[h]
            acc_ref[h] = jnp.exp(m_old[h] - m_new[h]) * acc_ref[h] + pv[h]
        return carry

    lax.fori_loop(0, i, past, 0)

    for pair in range(MOBA_HEADS // 2):
        halves = []
        for u in range(2):
            acc = acc_ref[2 * pair + u]
            halves.append(acc / pltpu.roll(acc, MOBA_HEAD_DIM, 1))
        y_ref[:, pair * LANES:(pair + 1) * LANES] = jnp.where(low_half, halves[0], halves[1]).astype(y_ref.dtype)


def moba_prompt(qb, kb, vb, sel, n_batch, seq):
    blk = MOBA_BLOCK
    nb = seq // blk
    row = lambda width: pl.BlockSpec((blk, width), lambda b, i: (b * nb + i, 0))
    full = pl.BlockSpec((seq, W), lambda b, i: (b, 0))
    scr = lambda: pltpu.VMEM((MOBA_HEADS, blk, LANES), f32)
    return pl.pallas_call(
        _moba_prompt_body,
        grid=(n_batch, nb),
        in_specs=[row(W), full, full, row(LANES)],
        out_specs=row(W),
        out_shape=jax.ShapeDtypeStruct((n_batch * seq, W), bf16),
        scratch_shapes=[scr(), scr()],
        compiler_params=_params(("arbitrary", "arbitrary")),
        name="moba_prompt",
    )(qb, kb, vb, sel)


def _moba_sample_body(n_pages, pt_ref, qb_ref, q_ref, k_ref, v_ref, *rest):
    del pt_ref
    k_pages, v_pages = rest[:n_pages], rest[n_pages:2 * n_pages]
    y_ref = rest[2 * n_pages]
    nh, dh = MOBA_HEADS, MOBA_HEAD_DIM
    pages_per_block = MOBA_BLOCK // PAGE_SIZE
    nb = n_pages // pages_per_block
    lane = _iota((nh, LANES), 1)
    q_b = qb_ref[...]

    def head_sum(x):
        return jnp.concatenate([jnp.sum(x[h * dh:(h + 1) * dh], axis=0, keepdims=True) for h in range(nh)], axis=0)

    def head_rows(x):
        return jnp.concatenate([jnp.broadcast_to(x[h:h + 1], (dh, x.shape[1])) for h in range(nh)], axis=0)

    s_pages = [head_sum(k_pages[g][...].reshape(nh * dh, PAGE_SIZE) * q_b) for g in range(n_pages)]
    gate = jnp.zeros((nh, LANES), f32)
    for n in range(nb):
        blk_sum = sum(jnp.sum(s_pages[n * pages_per_block + t], axis=1, keepdims=True)
                      for t in range(pages_per_block))
        gate = jnp.where(lane == n, blk_sum * (1.0 / MOBA_BLOCK), gate)
    sel = _topk_select(gate, nb, nb, 1).astype(f32)
    s_own = jnp.sum(q_ref[...] * k_ref[...], axis=1, keepdims=True)
    s_pages = [jnp.where(jnp.broadcast_to(sel[:, g // pages_per_block:g // pages_per_block + 1],
                                          (nh, PAGE_SIZE)) > 0.5, s, -jnp.inf)
               for g, s in enumerate(s_pages)]
    m = s_own
    for s in s_pages:
        m = jnp.maximum(m, jnp.max(s, axis=1, keepdims=True))
    p_pages = [jnp.exp(s - m) for s in s_pages]
    w_own = jnp.exp(s_own - m)
    l_tot = w_own + sum(jnp.sum(p, axis=1, keepdims=True) for p in p_pages)

    acc = jnp.zeros((nh * dh, PAGE_SIZE), f32)
    for g in range(n_pages):
        acc = acc + head_rows(p_pages[g]) * v_pages[g][...].reshape(nh * dh, PAGE_SIZE)
    hi = acc.astype(bf16)
    mid = (acc - hi.astype(f32))
    lo = (mid - mid.astype(bf16).astype(f32)).astype(bf16)
    ones = jnp.ones((nh, PAGE_SIZE), bf16)
    nt = lambda a: lax.dot_general(ones, a, NT_DIMS, preferred_element_type=f32)
    out = nt(hi) + nt(mid.astype(bf16)) + nt(lo)
    out = (out + w_own * v_ref[...]) / l_tot
    head_lane = _iota((nh, nh * dh), 1) // dh == _iota((nh, nh * dh), 0)
    y_ref[...] = jnp.sum(jnp.where(head_lane, out, 0.0), axis=0, keepdims=True)


def moba_sample(page_table_flat, q_b, q_s, k_s, v_row, cache_kt, cache_vt, layer, n_pages):
    n_seq = q_s.shape[0]
    nh, dh = MOBA_HEADS, MOBA_HEAD_DIM
    heads = pl.BlockSpec((None, nh, dh), lambda s, pt: (s, 0, 0))
    row = pl.BlockSpec((None, 1, W), lambda s, pt: (s, 0, 0))
    page = lambda u: pl.BlockSpec((None, None, nh, dh, PAGE_SIZE),
                                  lambda s, pt, u=u: (layer, pt[s * n_pages + u], 0, 0, 0))
    grid_spec = pltpu.PrefetchScalarGridSpec(
        num_scalar_prefetch=1,
        grid=(n_seq,),
        in_specs=[pl.BlockSpec((None, nh * dh, PAGE_SIZE), lambda s, pt: (s, 0, 0)), heads, heads, row]
        + [page(u) for u in range(n_pages)] + [page(u) for u in range(n_pages)],
        out_specs=row,
    )
    return pl.pallas_call(
        functools.partial(_moba_sample_body, n_pages),
        grid_spec=grid_spec,
        out_shape=jax.ShapeDtypeStruct((n_seq, 1, W), f32),
        compiler_params=_params(("arbitrary",)),
        name="moba_sample",
    )(page_table_flat, q_b, q_s, k_s, v_row, *([cache_kt] * n_pages), *([cache_vt] * n_pages))


def _rope_tables(pos):
    half = MOBA_HEAD_DIM // 2
    inv = ROPE_THETA ** (-jnp.arange(half, dtype=f32) / half)
    ang = pos.astype(f32)[:, None] * inv[None, :]
    cos, sin = jnp.cos(ang), jnp.sin(ang)
    cos_h = jnp.concatenate([cos, cos], -1)
    sin_h = jnp.concatenate([-sin, sin], -1)
    return jnp.tile(cos_h, (1, MOBA_HEADS)), jnp.tile(sin_h, (1, MOBA_HEADS))


def _pack_w_in(w):
    cols = lambda o, n: w[:, o:o + n]
    hk = GLA_HEADS * GLA_DK
    parts = [cols(_O_GQ, hk), cols(_O_GK, hk), cols(_O_GV, W), cols(_O_GO, W),
             cols(_O_MQ, W), cols(_O_MK, W), cols(_O_MV, W), cols(_O_PG, N_BRANCH * D_MODEL),
             cols(_O_PR, RWKV_PROJ), cols(_O_GLOW, GLA_GATE_RANK),
             jnp.zeros((w.shape[0], LANES - GLA_GATE_RANK), w.dtype)]
    return jnp.concatenate(parts, axis=1).astype(bf16)


def _pad_rows(m, rows):
    return jnp.concatenate([m, jnp.zeros((rows - m.shape[0], m.shape[1]), m.dtype)], axis=0)


SUBLANES = 8
IN_PROJ_ROWS_TARGET, IN_PROJ_TN = 2048, 896
TOKEN_ROWS_TARGET = 1024
MERGE_ROWS_TARGET = 512
MIXER_ROWS = 256


def _row_tile(m, target):
    best = SUBLANES
    for t in range(SUBLANES, target + 1, SUBLANES):
        if m % t == 0:
            best = t
    assert m % best == 0
    return best


def kernel(x_prompt, x_sample, cache_k, cache_v, page_table, state_gla, state_wkv, state_shift, w_in, b_gate, gla_gk_up, gla_gk_bias, gla_norm_w, rwkv_mu, rwkv_w0, rwkv_w_up, rwkv_a0, rwkv_a_up, rwkv_g_up, rwkv_k_k, rwkv_k_a, rwkv_r_k, rwkv_ln_w, rwkv_ln_b, w_branch, w_out, ln1_g, ln1_b, w_up, w_down, ln2_g, ln2_b):
    n_batch, seq, _ = x_prompt.shape
    n_seq = x_sample.shape[0]
    n_pages = page_table.shape[1]
    mp = n_batch * seq
    past = n_pages * PAGE_SIZE

    cos_p, sin_p = _rope_tables(jnp.arange(seq, dtype=jnp.int32))
    cos_s, sin_s = _rope_tables(jnp.full((1,), past, jnp.int32))
    pt_flat = page_table.reshape(-1)
    cache_kt = jnp.transpose(cache_k, (0, 1, 3, 4, 2))
    cache_vt = jnp.transpose(cache_v, (0, 1, 3, 4, 2))
    state_wkv_t = jnp.transpose(state_wkv, (0, 2, 3, 4, 1))

    x_p, x_s = x_prompt.reshape(mp, D_MODEL), x_sample.reshape(n_seq, D_MODEL)
    outs = {k: [] for k in ("k_p", "v_p", "k_s", "v_s", "gla_p", "gla_s", "wkv_p", "wkv_s", "shift_p", "shift_s")}
    for l in range(DEPTH):
        row2 = lambda v: v.reshape(1, -1)
        mu = rwkv_mu[l]
        rp = dict(
            mu_r=row2(mu[:W]), mu_k=row2(mu[W:2 * W]), mu_v=row2(mu[2 * W:3 * W]),
            mu_wa=row2(mu[3 * W:3 * W + LANES]), mu_gr=row2(mu[3 * W + LANES:]),
            w0=row2(rwkv_w0[l]), w_up=_pad_rows(rwkv_w_up[l], LANES).astype(bf16),
            a0=row2(rwkv_a0[l]),
            a_up=jnp.concatenate([jnp.zeros((RWKV_DECAY_RANK, W), f32), rwkv_a_up[l]], axis=0).astype(bf16),
            g_up=rwkv_g_up[l].astype(bf16), k_k=row2(rwkv_k_k[l]), k_a=row2(rwkv_k_a[l]),
            r_k=row2(rwkv_r_k[l]), ln_w=row2(rwkv_ln_w[l]), ln_b=row2(rwkv_ln_b[l]))
        up_pad = _pad_rows(gla_gk_up[l], LANES).astype(bf16)
        gk_bias = row2(gla_gk_bias[l])
        norm_w = row2(gla_norm_w[l])

        w_packed = _pack_w_in(w_in[l])
        wb, wo = w_branch[l].astype(bf16), w_out[l].astype(bf16)
        wu, wd = w_up[l].astype(bf16), w_down[l].astype(bf16)
        ln1, ln2 = (row2(ln1_g[l]), row2(ln1_b[l])), (row2(ln2_g[l]), row2(ln2_b[l]))

        def token_tail(x_rows, proj_rows, y_gla, y_moba, y_rwkv):
            tm = _row_tile(x_rows.shape[0], TOKEN_ROWS_TARGET)
            x1 = merge_ln(x_rows, proj_rows, y_gla, y_moba, y_rwkv, wb, b_gate[l], wo, *ln1,
                          _row_tile(x_rows.shape[0], MERGE_ROWS_TARGET))
            return mlp_ln(x1, wu, wd, *ln2, tm)

        proj_p = in_proj(x_p, w_packed, _row_tile(mp, IN_PROJ_ROWS_TARGET), IN_PROJ_TN)
        y_gla_p, gla_p = gla_prompt(proj_p, up_pad, gk_bias, norm_w, n_batch, seq, MIXER_ROWS)
        y_rwkv_p, wkv_p, shift_p = rwkv_prompt(proj_p, rp, n_batch, seq, MIXER_ROWS)
        k_rope, qb, kb, vb, sel = rope_prep(proj_p, cos_p, sin_p, n_batch, seq)
        y_moba_p = moba_prompt(qb, kb, vb, sel, n_batch, seq)
        x_p_next = token_tail(x_p, proj_p, y_gla_p, y_moba_p, y_rwkv_p)

        proj_s = in_proj(x_s, w_packed, n_seq, IN_PROJ_TN)
        y_gla_s, y_rwkv_s, q_s, k_s, gla_s, wkv_s_t = sample_step(
            proj_s, 0, n_seq, l, state_gla, state_wkv_t, state_shift, cos_s, sin_s, up_pad, gk_bias, norm_w, rp)
        v_s = proj_s[:, C_MV:C_MV + W]
        heads = (n_seq, MOBA_HEADS, MOBA_HEAD_DIM)
        q_lanes = jnp.broadcast_to(q_s[:, :, None], (n_seq, W, PAGE_SIZE))
        y_moba_s = moba_sample(pt_flat, q_lanes, q_s.reshape(heads), k_s.reshape(heads), v_s.reshape(n_seq, 1, W),
                               cache_kt, cache_vt, l, n_pages)
        x_s_next = token_tail(x_s, proj_s, y_gla_s, y_moba_s.reshape(n_seq, W).astype(bf16), y_rwkv_s)

        x_p, x_s = x_p_next, x_s_next
        hd = (MOBA_HEADS, MOBA_HEAD_DIM)
        outs["k_p"].append(k_rope.reshape(n_batch, seq, *hd))
        outs["v_p"].append(proj_p[:, C_MV:C_MV + W].reshape(n_batch, seq, *hd))
        outs["k_s"].append(k_s.reshape(n_seq, 1, *hd))
        outs["v_s"].append(v_s.reshape(n_seq, 1, *hd))
        outs["gla_p"].append(gla_p)
        outs["gla_s"].append(gla_s)
        outs["wkv_p"].append(wkv_p)
        outs["wkv_s"].append(jnp.transpose(wkv_s_t, (3, 0, 1, 2)))
        outs["shift_p"].append(shift_p[:, 0, :])
        outs["shift_s"].append(proj_s[:, C_PR:C_PR + RWKV_PROJ])

    st = lambda k: jnp.stack(outs[k])
    return (x_p.reshape(n_batch, seq, D_MODEL), x_s.reshape(n_seq, 1, D_MODEL),
            st("k_p"), st("v_p"), st("k_s"), st("v_s"), st("gla_p"), st("gla_s"),
            st("wkv_p"), st("wkv_s"), st("shift_p"), st("shift_s"))
```

```python
import functools

import jax
import jax.numpy as jnp
from jax import lax
from jax.experimental import pallas as pl
from jax.experimental.pallas import tpu as pltpu

f32, bf16 = jnp.float32, jnp.bfloat16

D_MODEL = 1024
DEPTH = 2
PAGE_SIZE = 128
BRANCH_WIDTH = D_MODEL // 2
N_BRANCH = 3
GLA_HEADS = 4
GLA_DV = BRANCH_WIDTH // GLA_HEADS
GLA_DK = GLA_DV // 2
GLA_GATE_RANK = 16
GLA_GATE_NORMALIZER = 16.0
GLA_CHUNK = 64
GLA_NORM_EPS = 1e-5
MOBA_HEAD_DIM = 64
MOBA_HEADS = BRANCH_WIDTH // MOBA_HEAD_DIM
MOBA_BLOCK = 256
MOBA_TOPK = 3
ROPE_THETA = 10000.0
RWKV_HEAD = 64
RWKV_HEADS = BRANCH_WIDTH // RWKV_HEAD
RWKV_DECAY_RANK = 64
RWKV_A_RANK = 64
RWKV_GATE_RANK = 128
RWKV_DECAY_SCALE = 0.606531
RWKV_GN_EPS = 64e-5
RWKV_PROJ = 3 * BRANCH_WIDTH + RWKV_DECAY_RANK + RWKV_A_RANK + RWKV_GATE_RANK
RWKV_CHUNK = 64
RWKV_INV_BLOCK = 16
D_FF = 4 * D_MODEL
ALPHA = (2 * DEPTH) ** 0.25
LN_EPS = 1e-5
LOG2_E = 1.4426950408889634
MASK_BIG = 2.0 ** 100

LANES = 128
VMEM_LIMIT_BYTES = 56 * 1024 * 1024

W = BRANCH_WIDTH
C_GQ, C_GK, C_GV, C_GO = 0, 256, 512, 1024
C_MQ, C_MK, C_MV = 1536, 2048, 2560
C_PG = 3072
C_PR = 6144
C_R, C_K7, C_V7 = C_PR, C_PR + W, C_PR + 2 * W
C_WA = C_PR + 3 * W
C_GR = C_WA + LANES
C_GLOW = C_PR + RWKV_PROJ
N_PACK = 8192

_O_GQ, _O_GK, _O_GV, _O_GLOW, _O_GO = 0, 256, 512, 1024, 1040
_O_MQ, _O_MK, _O_MV, _O_PR, _O_PG = 1552, 2064, 2576, 3088, 4880

NT_DIMS = (((1,), (1,)), ((), ()))
TN_DIMS = (((0,), (0,)), ((), ()))


def _params(sem):
    return pltpu.CompilerParams(dimension_semantics=sem, vmem_limit_bytes=VMEM_LIMIT_BYTES)


def _bdot(a, b):
    return jnp.dot(a.astype(bf16), b.astype(bf16), preferred_element_type=f32)


def _bdot_nt(a, b):
    return lax.dot_general(a.astype(bf16), b.astype(bf16), NT_DIMS, preferred_element_type=f32)


def _bdot_tn(a, b):
    return lax.dot_general(a.astype(bf16), b.astype(bf16), TN_DIMS, preferred_element_type=f32)


def _split2(x):
    hi = x.astype(bf16)
    lo = (x - hi.astype(f32)).astype(bf16)
    return hi, lo


def _dot_exact_rhs(x, m_bf16):
    hi, lo = _split2(x)
    return (jnp.dot(hi, m_bf16, preferred_element_type=f32)
            + jnp.dot(lo, m_bf16, preferred_element_type=f32))


def _dot_exact_lhs(m_bf16, x):
    hi, lo = _split2(x)
    return (jnp.dot(m_bf16, hi, preferred_element_type=f32)
            + jnp.dot(m_bf16, lo, preferred_element_type=f32))


def _dot3_nt(a, b):
    ah, al = _split2(a)
    bh, bl = _split2(b)
    d = lambda u, v: lax.dot_general(u, v, NT_DIMS, preferred_element_type=f32)
    return d(ah, bh) + d(al, bh) + d(ah, bl)


def _iota(shape, dim):
    return lax.broadcasted_iota(jnp.int32, shape, dim)


def _seg_matrix(n, seg):
    return (_iota((n, n), 0) // seg == _iota((n, n), 1) // seg).astype(bf16)


def _layer_norm(z, g, b):
    mu = jnp.mean(z, -1, keepdims=True)
    d = z - mu
    var = jnp.mean(d * d, -1, keepdims=True)
    return d * lax.rsqrt(var + LN_EPS) * g + b


def _row_to_col(row, n):
    eye = _iota((n, n), 0) == _iota((n, n), 1)
    return jnp.sum(jnp.where(eye, jnp.broadcast_to(row, (n, n)), 0.0), axis=1, keepdims=True)


def _set_row(ref, idx, row):
    cur = ref[...]
    ref[...] = jnp.where(_iota(cur.shape, 0) == idx, jnp.broadcast_to(row, cur.shape), cur)


def _col_to_row(col, n):
    eye = _iota((n, n), 0) == _iota((n, n), 1)
    return jnp.sum(jnp.where(eye, jnp.broadcast_to(col, (n, n)), 0.0), axis=0, keepdims=True)


def _inproj_body(x_ref, w_ref, o_ref, xb_ref):
    @pl.when(pl.program_id(1) == 0)
    def _():
        xb_ref[...] = x_ref[...].astype(bf16)

    o_ref[...] = jnp.dot(xb_ref[...], w_ref[...], preferred_element_type=f32)


def in_proj(x, w, tm, tn):
    m, k = x.shape
    n = w.shape[1]
    return pl.pallas_call(
        _inproj_body,
        grid=(m // tm, n // tn),
        in_specs=[pl.BlockSpec((tm, k), lambda i, j: (i, 0)),
                  pl.BlockSpec((k, tn), lambda i, j: (0, j))],
        out_specs=pl.BlockSpec((tm, tn), lambda i, j: (i, j)),
        out_shape=jax.ShapeDtypeStruct((m, n), f32),
        scratch_shapes=[pltpu.VMEM((tm, k), bf16)],
        compiler_params=_params(("arbitrary", "arbitrary")),
        name="in_proj",
    )(x, w)


def _merge_body(x_ref, pg0_ref, pg1_ref, pg2_ref, y0_ref, y1_ref, y2_ref, wb_ref, bg_ref, wo_ref,
                g_ref, b_ref, o_ref):
    merged = None
    for n, (pg_ref, y_ref) in enumerate(((pg0_ref, y0_ref), (pg1_ref, y1_ref), (pg2_ref, y2_ref))):
        gate = jax.nn.sigmoid(pg_ref[...] + bg_ref[n:n + 1, :])
        term = gate * jnp.dot(y_ref[...], wb_ref[n], preferred_element_type=f32)
        merged = term if merged is None else merged + term
    out = jnp.dot(merged.astype(bf16), wo_ref[...], preferred_element_type=f32)
    o_ref[...] = _layer_norm(ALPHA * x_ref[...] + out, g_ref[...], b_ref[...])


def merge_ln(x, proj, y_gla, y_moba, y_rwkv, w_branch, b_gate, w_out, g, b, tm):
    m = x.shape[0]
    row = lambda i: (i, 0)
    const2 = lambda i: (0, 0)
    pg_spec = lambda n: pl.BlockSpec((tm, D_MODEL), lambda i, n=n: (i, C_PG // D_MODEL + n))
    y_spec = pl.BlockSpec((tm, W), row)
    return pl.pallas_call(
        _merge_body,
        grid=(m // tm,),
        in_specs=[pl.BlockSpec((tm, D_MODEL), row), pg_spec(0), pg_spec(1), pg_spec(2),
                  y_spec, y_spec, y_spec,
                  pl.BlockSpec((N_BRANCH, W, D_MODEL), lambda i: (0, 0, 0)),
                  pl.BlockSpec((N_BRANCH, D_MODEL), const2),
                  pl.BlockSpec((D_MODEL, D_MODEL), const2),
                  pl.BlockSpec((1, D_MODEL), const2), pl.BlockSpec((1, D_MODEL), const2)],
        out_specs=pl.BlockSpec((tm, D_MODEL), row),
        out_shape=jax.ShapeDtypeStruct((m, D_MODEL), f32),
        compiler_params=_params(("arbitrary",)),
        name="merge_ln",
    )(x, proj, proj, proj, y_gla, y_moba, y_rwkv, w_branch, b_gate, w_out, g, b)


FF_CHUNK = 1024


def _mlp_body(x_ref, wu_ref, wd_ref, g_ref, b_ref, o_ref):
    x = x_ref[...]
    xb = x.astype(bf16)
    acc = None
    for c in range(D_FF // FF_CHUNK):
        h = jnp.dot(xb, wu_ref[:, c * FF_CHUNK:(c + 1) * FF_CHUNK], preferred_element_type=f32)
        h = jnp.square(jnp.maximum(h, 0.0)).astype(bf16)
        part = jnp.dot(h, wd_ref[c * FF_CHUNK:(c + 1) * FF_CHUNK, :], preferred_element_type=f32)
        acc = part if acc is None else acc + part
    o_ref[...] = _layer_norm(ALPHA * x + acc, g_ref[...], b_ref[...])


def mlp_ln(x, w_up, w_down, g, b, tm):
    m = x.shape[0]
    row = lambda i: (i, 0)
    const2 = lambda i: (0, 0)
    return pl.pallas_call(
        _mlp_body,
        grid=(m // tm,),
        in_specs=[pl.BlockSpec((tm, D_MODEL), row),
                  pl.BlockSpec((D_MODEL, D_FF), const2, pipeline_mode=pl.Buffered(1)),
                  pl.BlockSpec((D_FF, D_MODEL), const2, pipeline_mode=pl.Buffered(1)),
                  pl.BlockSpec((1, D_MODEL), const2), pl.BlockSpec((1, D_MODEL), const2)],
        out_specs=pl.BlockSpec((tm, D_MODEL), row),
        out_shape=jax.ShapeDtypeStruct((m, D_MODEL), f32),
        compiler_params=_params(("arbitrary",)),
        name="mlp_ln",
    )(x, w_up, w_down, g, b)


def _gla_log_gate(glow, up_ref, bias_ref):
    z = _bdot(glow, up_ref[...]) + bias_ref[...]
    return jax.nn.log_sigmoid(z) * (1.0 / GLA_GATE_NORMALIZER)


def _gla_out(o, norm_w, g_out):
    o = o * lax.rsqrt(jnp.mean(jnp.square(o), -1, keepdims=True) + GLA_NORM_EPS) * norm_w
    return o * jax.nn.silu(g_out)


def _gla_prompt_body(q_ref, k_ref, v_ref, go_ref, glow_ref, up_ref, bias_ref, nw_ref,
                     y_ref, s_out_ref, s_ref):
    t = pl.program_id(1)
    rows = q_ref.shape[0]
    c_sz = GLA_CHUNK

    @pl.when(t == 0)
    def _():
        s_ref[...] = jnp.zeros_like(s_ref)

    lg = _gla_log_gate(glow_ref[...], up_ref, bias_ref)
    tril = (_iota((c_sz, c_sz), 1) <= _iota((c_sz, c_sz), 0))
    tril_b = tril.astype(bf16)
    n_chunks = rows // c_sz
    probs = []
    for c in range(n_chunks):
        rs = slice(c * c_sz, (c + 1) * c_sz)
        g_cum = _dot_exact_lhs(tril_b, lg[rs])
        g_last = g_cum[c_sz - 1:c_sz, :]
        q_dec = q_ref[rs, :] * (GLA_DK ** -0.5) * jnp.exp(g_cum)
        k = k_ref[rs, :]
        k_inv = k * jnp.exp(-g_cum)
        k_tail = k * jnp.exp(g_last - g_cum)
        e_last = jnp.exp(g_last)
        for h in range(GLA_HEADS):
            ks = slice(h * GLA_DK, (h + 1) * GLA_DK)
            vs = slice(h * GLA_DV, (h + 1) * GLA_DV)
            probs.append((q_dec[:, ks], k_inv[:, ks], k_tail[:, ks], e_last[:, ks], v_ref[rs, vs], rs, vs))
    a = [jnp.where(tril, _bdot_nt(q[0], q[1]), 0.0) for q in probs]
    o_intra = [_bdot(x, q[4]) for x, q in zip(a, probs)]
    kv = [_bdot_tn(q[2], q[4]) for q in probs]
    e_col = [_row_to_col(q[3], GLA_DK) for q in probs]

    state = [s_ref[h] for h in range(GLA_HEADS)]
    for c in range(n_chunks):
        sl = slice(c * GLA_HEADS, (c + 1) * GLA_HEADS)
        o = [oi + _bdot(q[0], s) for oi, q, s in zip(o_intra[sl], probs[sl], state)]
        state = [s * e + x for s, e, x in zip(state, e_col[sl], kv[sl])]
        for oo, q in zip(o, probs[sl]):
            rs, vs = q[5], q[6]
            y_ref[rs, vs] = _gla_out(oo, nw_ref[...], go_ref[rs, vs]).astype(y_ref.dtype)
    for h in range(GLA_HEADS):
        s_ref[h] = state[h]
    s_out_ref[...] = s_ref[...]


def gla_prompt(proj, up_pad, bias, norm_w, n_batch, seq, rt):
    nt = seq // rt
    rowmap = lambda width, col: pl.BlockSpec((rt, width), lambda b, t: (b * nt + t, col // width))
    const2 = lambda b, t: (0, 0)
    return pl.pallas_call(
        _gla_prompt_body,
        grid=(n_batch, nt),
        in_specs=[rowmap(GLA_HEADS * GLA_DK, C_GQ), rowmap(GLA_HEADS * GLA_DK, C_GK), rowmap(W, C_GV),
                  rowmap(W, C_GO), rowmap(LANES, C_GLOW),
                  pl.BlockSpec((LANES, GLA_HEADS * GLA_DK), const2),
                  pl.BlockSpec((1, GLA_HEADS * GLA_DK), const2),
                  pl.BlockSpec((1, GLA_DV), const2)],
        out_specs=[pl.BlockSpec((rt, W), lambda b, t: (b * nt + t, 0)),
                   pl.BlockSpec((None, GLA_HEADS, GLA_DK, GLA_DV), lambda b, t: (b, 0, 0, 0))],
        out_shape=[jax.ShapeDtypeStruct((n_batch * seq, W), bf16),
                   jax.ShapeDtypeStruct((n_batch, GLA_HEADS, GLA_DK, GLA_DV), f32)],
        scratch_shapes=[pltpu.VMEM((GLA_HEADS, GLA_DK, GLA_DV), f32)],
        compiler_params=_params(("arbitrary", "arbitrary")),
        name="gla_prompt",
    )(proj, proj, proj, proj, proj, up_pad, bias, norm_w)


def _rwkv_pre(r_s, k_s, v_s, wa_s, gr_s, p):
    seg = _seg_matrix(W, RWKV_HEAD)
    logw = -RWKV_DECAY_SCALE * jax.nn.sigmoid(p["w0"][...] + _bdot(jnp.tanh(wa_s), p["w_up"][...]))
    a = jax.nn.sigmoid(p["a0"][...] + _bdot(wa_s, p["a_up"][...]))
    g7 = _bdot(jax.nn.sigmoid(gr_s), p["g_up"][...])
    kk = k_s * p["k_k"][...]
    kk = kk * lax.rsqrt(_dot_exact_rhs(jnp.square(kk), seg) + 1e-12)
    k7 = k_s * (1.0 + (a - 1.0) * p["k_a"][...])
    bonus = _dot_exact_rhs(r_s * k7 * p["r_k"][...], seg) * v_s
    return dict(r=r_s, logw=logw, k=k7, v=v_s, kk=kk, a=a, g7=g7, bonus=bonus)


def _rwkv_post(yw, bonus, g7, p):
    seg = _seg_matrix(W, RWKV_HEAD)
    inv_n = 1.0 / RWKV_HEAD
    mu = _dot_exact_rhs(yw, seg) * inv_n
    d = yw - mu
    var = _dot_exact_rhs(jnp.square(d), seg) * inv_n
    yn = d * lax.rsqrt(var + RWKV_GN_EPS) * p["ln_w"][...] + p["ln_b"][...]
    return (yn + bonus) * g7


def _pair_bd(x):
    low = _iota(x.shape, 1) < x.shape[1] // 2
    zero = jnp.zeros_like(x)
    return jnp.concatenate([jnp.where(low, x, zero), jnp.where(low, zero, x)], axis=0)


def _pair_mm(a, b):
    return jnp.dot(a.astype(bf16), _pair_bd(b.astype(bf16)), preferred_element_type=f32)


def _pair_nt(a, bd_b):
    return lax.dot_general(a.astype(bf16), bd_b, NT_DIMS, preferred_element_type=f32)


def _pair_tn(a, b):
    full = lax.dot_general(a.astype(bf16), b.astype(bf16), TN_DIMS, preferred_element_type=f32)
    p = b.shape[1] // 2
    low = _iota((p, b.shape[1]), 1) < p
    groups = full.shape[0] // (2 * p)
    return jnp.concatenate([jnp.where(low, full[2 * g * p:(2 * g + 1) * p], full[(2 * g + 1) * p:(2 * g + 2) * p])
                            for g in range(groups)], axis=0)


def _inv_unit_lower(n_mats):
    c = n_mats[0].shape[0]
    row, col = _iota((c, 2 * c), 0), _iota((c, 2 * c), 1) % c
    eye = (row == col).astype(f32)
    blk = (row // RWKV_INV_BLOCK) == (col // RWKV_INV_BLOCK)
    d = [jnp.where(blk, n, 0.0) for n in n_mats]
    lo = [n - x for n, x in zip(n_mats, d)]
    d2 = [_pair_mm(x, x) for x in d]
    d4 = [_pair_mm(x, x) for x in d2]
    d8 = [_pair_mm(x, x) for x in d4]
    pa = [_pair_mm(eye - x, eye + y) for x, y in zip(d, d2)]
    pb = [_pair_mm(eye + x, eye + y) for x, y in zip(d4, d8)]
    d_inv = [_pair_mm(x, y) for x, y in zip(pa, pb)]
    e = [_pair_mm(x, y) for x, y in zip(d_inv, lo)]
    e2 = [_pair_mm(x, x) for x in e]
    f = [_pair_mm(eye - x, eye + y) for x, y in zip(e, e2)]
    return [_pair_mm(x, y) for x, y in zip(f, d_inv)]


def _rwkv_chunk_coeffs(probs):
    c, width = probs[0][0].shape
    row, col = _iota((c, width), 0), _iota((c, width), 1) % (width // 2)
    strict, incl, eye = col < row, col <= row, col == row
    lhs = [jnp.concatenate([q[1], q[0]], 0).astype(bf16) for q in probs]
    with_b = [_pair_nt(x, _pair_bd(q[2].astype(bf16))) for x, q in zip(lhs, probs)]
    with_k = [_pair_nt(x, _pair_bd(q[3].astype(bf16))) for x, q in zip(lhs, probs)]
    n_ab = [jnp.where(strict, x[:c], 0.0) for x in with_b]
    p_rb = [jnp.where(incl, x[c:], 0.0) for x in with_b]
    a_ak = [jnp.where(strict, x[:c], 0.0) for x in with_k]
    p_rk = [jnp.where(incl, x[c:], 0.0) for x in with_k]
    t_inv = _inv_unit_lower(n_ab)
    av = [_pair_mm(x, q[6]) for x, q in zip(a_ak, probs)]
    two = lambda u, w: jnp.concatenate([_pair_bd(u.astype(bf16)), _pair_bd(w.astype(bf16))], axis=1)
    x = [jnp.dot(ti.astype(bf16), two(q[1], a), preferred_element_type=f32)
         for ti, q, a in zip(t_inv, probs, av)]
    px = [jnp.dot(pr.astype(bf16), two(xx[:, :width], xx[:, width:]), preferred_element_type=f32)
          for pr, xx in zip(p_rb, x)]
    prkv = [_pair_mm(pr, q[6]) for pr, q in zip(p_rk, probs)]
    xb = [_pair_tn(xx, q[4]) for xx, q in zip(x, probs)]
    vk = [_pair_tn(q[6], q[5]) for q in probs]
    out = []
    for q, pxx, pv, xbb, vkk in zip(probs, px, prkv, xb, vk):
        tm_t = jnp.where(eye, jnp.broadcast_to(q[7], (c, width)), 0.0) - xbb[:c]
        out.append((q[0] - pxx[:, :width], pv - pxx[:, width:], tm_t, vkk - xbb[c:]))
    return out


_RWKV_PARAM_NAMES = ("mu_r", "mu_k", "mu_v", "mu_wa", "mu_gr", "w0", "w_up", "a0", "a_up", "g_up",
                     "k_k", "k_a", "r_k", "ln_w", "ln_b")


def _rwkv_prompt_body(r_ref, k_ref, v_ref, wa_ref, gr_ref, *rest):
    np_ = len(_RWKV_PARAM_NAMES)
    p = dict(zip(_RWKV_PARAM_NAMES, rest[:np_]))
    y_ref, s_out_ref, shift_out_ref = rest[np_:np_ + 3]
    s_ref, prev_ref = rest[np_ + 3:]
    t = pl.program_id(1)
    rows = r_ref.shape[0]
    c_sz = RWKV_CHUNK

    @pl.when(t == 0)
    def _():
        s_ref[...] = jnp.zeros_like(s_ref)
        prev_ref[...] = jnp.zeros_like(prev_ref)

    def shifted(x_ref, mu_ref, c0):
        x = x_ref[...]
        width = x.shape[1]
        prev = jnp.where(_iota(x.shape, 0) == 0, prev_ref[0:1, c0:c0 + width], pltpu.roll(x, 1, 0))
        prev_ref[0:1, c0:c0 + width] = x[rows - 1:rows, :]
        return x + mu_ref[...] * (prev - x)

    pre = _rwkv_pre(shifted(r_ref, p["mu_r"], 0), shifted(k_ref, p["mu_k"], W), shifted(v_ref, p["mu_v"], 2 * W),
                    shifted(wa_ref, p["mu_wa"], 3 * W), shifted(gr_ref, p["mu_gr"], 3 * W + LANES), p)
    tril_b = (_iota((c_sz, c_sz), 1) <= _iota((c_sz, c_sz), 0)).astype(bf16)
    n_chunks = rows // c_sz
    n_pairs = RWKV_HEADS // 2
    b_all = pre["kk"] * pre["a"]

    probs = []
    for c in range(n_chunks):
        rs = slice(c * c_sz, (c + 1) * c_sz)
        lw = pre["logw"][rs]
        g_cum = _dot_exact_lhs(tril_b, lw)
        g_last = g_cum[c_sz - 1:c_sz, :]
        e_ng = jnp.exp(-g_cum)
        e_gl = jnp.exp(g_last - g_cum)
        kk, b, k, v = pre["kk"][rs], b_all[rs], pre["k"][rs], pre["v"][rs]
        r_t = pre["r"][rs] * jnp.exp(g_cum)
        a_t = kk * jnp.exp(g_cum - lw)
        b_t, k_t, b_g, k_g = b * e_ng, k * e_ng, b * e_gl, k * e_gl
        gam = jnp.exp(g_last)
        for hp in range(n_pairs):
            hs = slice(hp * LANES, (hp + 1) * LANES)
            probs.append((r_t[:, hs], a_t[:, hs], b_t[:, hs], k_t[:, hs], b_g[:, hs], k_g[:, hs], v[:, hs],
                          gam[:, hs]))
    coeffs = _rwkv_chunk_coeffs(probs)

    state = [s_ref[hp] for hp in range(n_pairs)]
    y_rows = []
    for c in range(n_chunks):
        cf = coeffs[c * n_pairs:(c + 1) * n_pairs]
        bd_s = [_pair_bd(s.astype(bf16)) for s in state]
        y_rows.append(jnp.concatenate([_pair_nt(q[0], bd) + q[1] for q, bd in zip(cf, bd_s)], axis=1))
        state = [_pair_mm(s, q[2]) + q[3] for q, s in zip(cf, state)]
    for hp in range(n_pairs):
        s_ref[hp] = state[hp]
    yw = jnp.concatenate(y_rows, axis=0)
    y_ref[...] = _rwkv_post(yw, pre["bonus"], pre["g7"], p).astype(y_ref.dtype)
    s_out_ref[...] = s_ref[...]
    shift_out_ref[...] = prev_ref[...]


def _rwkv_param_specs(const_map):
    vec = lambda width: pl.BlockSpec((1, width), const_map)
    mat = lambda r: pl.BlockSpec((r, W), const_map)
    return [vec(W), vec(W), vec(W), vec(LANES), vec(LANES), vec(W), mat(LANES), vec(W), mat(LANES),
            mat(RWKV_GATE_RANK), vec(W), vec(W), vec(W), vec(W), vec(W)]


def rwkv_prompt(proj, params, n_batch, seq, rt):
    nt = seq // rt
    rowmap = lambda width, col: pl.BlockSpec((rt, width), lambda b, t: (b * nt + t, col // width))
    const2 = lambda b, t: (0, 0)
    return pl.pallas_call(
        _rwkv_prompt_body,
        grid=(n_batch, nt),
        in_specs=[rowmap(W, C_R), rowmap(W, C_K7), rowmap(W, C_V7), rowmap(LANES, C_WA), rowmap(LANES, C_GR)]
        + _rwkv_param_specs(const2),
        out_specs=[pl.BlockSpec((rt, W), lambda b, t: (b * nt + t, 0)),
                   pl.BlockSpec((None, RWKV_HEADS // 2, RWKV_HEAD, 2 * RWKV_HEAD), lambda b, t: (b, 0, 0, 0)),
                   pl.BlockSpec((None, SUBLANES, RWKV_PROJ), lambda b, t: (b, 0, 0))],
        out_shape=[jax.ShapeDtypeStruct((n_batch * seq, W), bf16),
                   jax.ShapeDtypeStruct((n_batch, RWKV_HEADS // 2, RWKV_HEAD, 2 * RWKV_HEAD), f32),
                   jax.ShapeDtypeStruct((n_batch, SUBLANES, RWKV_PROJ), f32)],
        scratch_shapes=[pltpu.VMEM((RWKV_HEADS // 2, RWKV_HEAD, 2 * RWKV_HEAD), f32),
                        pltpu.VMEM((SUBLANES, RWKV_PROJ), f32)],
        compiler_params=_params(("arbitrary", "arbitrary")),
        name="rwkv_prompt",
    )(proj, proj, proj, proj, proj, *[params[n] for n in _RWKV_PARAM_NAMES])


def _rope(x, cos, sin_signed):
    lane = _iota(x.shape, 1)
    half = MOBA_HEAD_DIM // 2
    width = x.shape[1]
    swapped = jnp.where(lane % MOBA_HEAD_DIM < half, pltpu.roll(x, width - half, 1), pltpu.roll(x, half, 1))
    return x * cos + swapped * sin_signed


def _sample_body(gq_ref, gk_ref, gv_ref, go_ref, glow_ref, r_ref, k_ref, v_ref, wa_ref, gr_ref, mq_ref, mk_ref,
                 sg_ref, sw_ref, shift_ref, cos_ref, sin_ref, up_ref, bias_ref, nw_ref, *rest):
    np_ = len(_RWKV_PARAM_NAMES)
    p = dict(zip(_RWKV_PARAM_NAMES, rest[:np_]))
    yg_ref, yr_ref, qs_ref, ks_ref, sg_out, sw_out = rest[np_:np_ + 6]
    qt_s, kt_s, egt_s, og_s, rt_s, wt_s, k7t_s, vt_s, kkt_s, bt_s, yt_s, bonus_s, g7_s = rest[np_ + 6:]
    hp = pl.program_id(0)
    n_seq = gq_ref.shape[0]

    @pl.when(hp == 0)
    def _():
        qs_ref[...] = _rope(mq_ref[...], cos_ref[...], sin_ref[...]) * (MOBA_HEAD_DIM ** -0.5)
        ks_ref[...] = _rope(mk_ref[...], cos_ref[...], sin_ref[...])
        lg = _gla_log_gate(glow_ref[...], up_ref, bias_ref)
        qt_s[...] = (gq_ref[...] * (GLA_DK ** -0.5)).T
        kt_s[...] = gk_ref[...].T
        egt_s[...] = jnp.exp(lg).T

        def shifted(x_ref, mu_ref, c0):
            x = x_ref[...]
            return x + mu_ref[...] * (shift_ref[:, c0:c0 + x.shape[1]] - x)

        pre = _rwkv_pre(shifted(r_ref, p["mu_r"], 0), shifted(k_ref, p["mu_k"], W), shifted(v_ref, p["mu_v"], 2 * W),
                        shifted(wa_ref, p["mu_wa"], 3 * W), shifted(gr_ref, p["mu_gr"], 3 * W + LANES), p)
        rt_s[...] = pre["r"].T
        wt_s[...] = jnp.exp(pre["logw"]).T
        k7t_s[...] = pre["k"].T
        vt_s[...] = pre["v"].T
        kkt_s[...] = pre["kk"].T
        bt_s[...] = (pre["kk"] * pre["a"]).T
        bonus_s[...] = pre["bonus"]
        g7_s[...] = pre["g7"]
        og_s[...] = jnp.zeros_like(og_s)

    ks = pl.ds(pl.multiple_of(hp * GLA_DK, GLA_DK), GLA_DK)
    eg_t, k_t, q_t = egt_s[ks, :], kt_s[ks, :], qt_s[ks, :]
    gv_h = gv_ref[:, pl.ds(pl.multiple_of(hp * GLA_DV, GLA_DV), GLA_DV)]
    lane_grp = _iota((n_seq, W), 1) // GLA_DV
    o_rows = []
    for s in range(n_seq):
        col = lambda x: jnp.broadcast_to(x[:, s:s + 1], (GLA_DK, GLA_DV))
        st = sg_ref[s] * col(eg_t) + col(k_t) * gv_h[s:s + 1, :]
        sg_out[s] = st
        o_rows.append(jnp.sum(col(q_t) * st, axis=0, keepdims=True))
    o_h = jnp.concatenate(o_rows, axis=0)
    og_s[...] = jnp.where(lane_grp == hp, jnp.concatenate([o_h] * GLA_HEADS, axis=1), og_s[...])

    group = RWKV_HEAD // SUBLANES
    for u in range(2):
        hs = pl.ds(pl.multiple_of((2 * hp + u) * RWKV_HEAD, RWKV_HEAD), RWKV_HEAD)
        neg_kk, w_t, b_t, k_t7, r_t = -kkt_s[hs, :], wt_s[hs, :], bt_s[hs, :], k7t_s[hs, :], rt_s[hs, :]

        def v_group(g, carry, u=u, hs=hs, neg_kk=neg_kk, w_t=w_t, b_t=b_t, k_t7=k_t7, r_t=r_t):
            base = pl.multiple_of((2 * hp + u) * RWKV_HEAD + g * SUBLANES, SUBLANES)
            v_rows = vt_s[pl.ds(base, SUBLANES), :]
            y_rows = []
            for j in range(SUBLANES):
                vi = g * SUBLANES + j
                st = sw_ref[u, vi]
                sa = jnp.sum(st * neg_kk, axis=0, keepdims=True)
                st = st * w_t + sa * b_t + v_rows[j:j + 1, :] * k_t7
                sw_out[u, vi] = st
                y_rows.append(jnp.sum(st * r_t, axis=0, keepdims=True))
            yt_s[pl.ds(base, SUBLANES), :] = jnp.concatenate(y_rows, axis=0)
            return carry

        lax.fori_loop(0, group, v_group, 0)

    @pl.when(hp == pl.num_programs(0) - 1)
    def _():
        og = og_s[...]
        for h in range(GLA_HEADS):
            vs = slice(h * GLA_DV, (h + 1) * GLA_DV)
            yg_ref[:, vs] = _gla_out(og[:, vs], nw_ref[...], go_ref[:, vs]).astype(yg_ref.dtype)
        yr_ref[...] = _rwkv_post(yt_s[...].T, bonus_s[...], g7_s[...], p).astype(yr_ref.dtype)


def sample_step(proj, row0, n_seq, layer, state_gla, state_wkv_t, state_shift, cos_s, sin_s,
                up_pad, bias, norm_w, params):
    r0 = row0 // n_seq
    rowmap = lambda width, col: pl.BlockSpec((n_seq, width), lambda h: (r0, col // width))
    const2 = lambda h: (0, 0)
    hk = GLA_HEADS * GLA_DK
    out_row = pl.BlockSpec((n_seq, W), const2)
    tr = lambda rows: pltpu.VMEM((rows, n_seq), f32)
    rw = lambda: pltpu.VMEM((n_seq, W), f32)
    return pl.pallas_call(
        _sample_body,
        grid=(GLA_HEADS,),
        in_specs=[rowmap(hk, C_GQ), rowmap(hk, C_GK), rowmap(W, C_GV), rowmap(W, C_GO), rowmap(LANES, C_GLOW),
                  rowmap(W, C_R), rowmap(W, C_K7), rowmap(W, C_V7), rowmap(LANES, C_WA), rowmap(LANES, C_GR),
                  rowmap(W, C_MQ), rowmap(W, C_MK),
                  pl.BlockSpec((None, n_seq, None, GLA_DK, GLA_DV), lambda h: (layer, 0, h, 0, 0)),
                  pl.BlockSpec((None, 2, RWKV_HEAD, RWKV_HEAD, n_seq), lambda h: (layer, h, 0, 0, 0)),
                  pl.BlockSpec((None, n_seq, RWKV_PROJ), lambda h: (layer, 0, 0)),
                  pl.BlockSpec((1, W), const2), pl.BlockSpec((1, W), const2),
                  pl.BlockSpec((LANES, hk), const2), pl.BlockSpec((1, hk), const2), pl.BlockSpec((1, GLA_DV), const2)]
        + _rwkv_param_specs(const2),
        out_specs=[out_row, out_row, out_row, out_row,
                   pl.BlockSpec((n_seq, None, GLA_DK, GLA_DV), lambda h: (0, h, 0, 0)),
                   pl.BlockSpec((2, RWKV_HEAD, RWKV_HEAD, n_seq), lambda h: (h, 0, 0, 0))],
        out_shape=[jax.ShapeDtypeStruct((n_seq, W), bf16), jax.ShapeDtypeStruct((n_seq, W), bf16),
                   jax.ShapeDtypeStruct((n_seq, W), f32), jax.ShapeDtypeStruct((n_seq, W), f32),
                   jax.ShapeDtypeStruct((n_seq, GLA_HEADS, GLA_DK, GLA_DV), f32),
                   jax.ShapeDtypeStruct((RWKV_HEADS, RWKV_HEAD, RWKV_HEAD, n_seq), f32)],
        scratch_shapes=[tr(hk), tr(hk), tr(hk), rw(), tr(W), tr(W), tr(W), tr(W), tr(W), tr(W), tr(W), rw(), rw()],
        compiler_params=_params(("arbitrary",)),
        name="sample_step",
    )(proj, proj, proj, proj, proj, proj, proj, proj, proj, proj, proj, proj,
      state_gla, state_wkv_t, state_shift, cos_s, sin_s, up_pad, bias, norm_w,
      *[params[n] for n in _RWKV_PARAM_NAMES])


def _topk_select(gate, n_valid, axis_len, axis):
    idx = _iota(gate.shape, axis)
    cnt = jnp.zeros(gate.shape, f32)
    for m in range(axis_len):
        g_m = lax.slice_in_dim(gate, m, m + 1, axis=axis)
        g_m = jnp.broadcast_to(g_m, gate.shape)
        beats = (g_m > gate) | ((g_m == gate) & (m < idx))
        cnt = cnt + jnp.where(beats, 1.0, 0.0) * jnp.where(m < n_valid, 1.0, 0.0)
    return (idx < n_valid) & (cnt < MOBA_TOPK)


def _rope_prep_body(mq_ref, mk_ref, mv_ref, cos_ref, sin_ref, k_out, qb_out, kb_out, vb_out, sel_out, km_ref):
    i = pl.program_id(1)
    blk = mq_ref.shape[0]
    nb = km_ref.shape[0]
    nh = MOBA_HEADS

    @pl.when(i == 0)
    def _():
        km_ref[...] = jnp.zeros_like(km_ref)

    q = _rope(mq_ref[...], cos_ref[...], sin_ref[...]) * (MOBA_HEAD_DIM ** -0.5)
    k = _rope(mk_ref[...], cos_ref[...], sin_ref[...])
    k_out[...] = k
    qb_out[...] = (q * LOG2_E).astype(bf16)
    kb_out[...] = k.astype(bf16)
    vb_out[...] = mv_ref[...].astype(bf16)

    km = km_ref[...]
    km_rows = jnp.concatenate([km] * nh, axis=0)
    head_of_row = _iota(km_rows.shape, 0) // nb
    km_rows = jnp.where(head_of_row == _iota(km_rows.shape, 1) // MOBA_HEAD_DIM, km_rows, 0.0)
    gate = _dot3_nt(km_rows, q).reshape(nh, nb, blk)
    sel = _topk_select(gate, i, nb, 1).astype(bf16).reshape(nh * nb, blk)
    sel = jnp.concatenate([sel, jnp.zeros((LANES - nh * nb, blk), bf16)], axis=0)
    eye = (_iota((blk, blk), 0) == _iota((blk, blk), 1)).astype(bf16)
    sel_out[...] = (lax.dot_general(eye, sel, NT_DIMS, preferred_element_type=f32) - 1.0) * MASK_BIG

    _set_row(km_ref, i, jnp.sum(k, axis=0, keepdims=True) * (1.0 / blk))


def rope_prep(proj, cos, sin, n_batch, seq):
    blk = MOBA_BLOCK
    nb = seq // blk
    m = n_batch * seq
    rowmap = lambda col: pl.BlockSpec((blk, W), lambda b, i: (b * nb + i, col // W))
    tab = pl.BlockSpec((blk, W), lambda b, i: (i, 0))
    out = lambda width: pl.BlockSpec((blk, width), lambda b, i: (b * nb + i, 0))
    return pl.pallas_call(
        _rope_prep_body,
        grid=(n_batch, nb),
        in_specs=[rowmap(C_MQ), rowmap(C_MK), rowmap(C_MV), tab, tab],
        out_specs=[out(W), out(W), out(W), out(W), out(LANES)],
        out_shape=[jax.ShapeDtypeStruct((m, W), f32), jax.ShapeDtypeStruct((m, W), bf16),
                   jax.ShapeDtypeStruct((m, W), bf16), jax.ShapeDtypeStruct((m, W), bf16),
                   jax.ShapeDtypeStruct((m, LANES), f32)],
        scratch_shapes=[pltpu.VMEM((nb, W), f32)],
        compiler_params=_params(("arbitrary", "arbitrary")),
        name="rope_prep",
    )(proj, proj, proj, cos, sin)


def _moba_prompt_body(qb_ref, kb_ref, vb_ref, sel_ref, y_ref, m_ref, acc_ref):
    i = pl.program_id(1)
    blk = qb_ref.shape[0]
    nb = kb_ref.shape[0] // blk
    lane = _iota((blk, LANES), 1)
    low_half = lane < MOBA_HEAD_DIM
    causal = _iota((blk, blk), 1) <= _iota((blk, blk), 0)
    neg_inf = -jnp.inf

    def head_q(pair, u):
        qp = qb_ref[:, pair * LANES:(pair + 1) * LANES]
        return jnp.where(low_half if u == 0 else ~low_half, qp, jnp.zeros_like(qp))

    def kv(n, pair):
        rs = pl.ds(pl.multiple_of(n * blk, blk), blk)
        ls = slice(pair * LANES, (pair + 1) * LANES)
        return kb_ref[rs, ls], vb_ref[rs, ls]

    wide = lambda x: jnp.concatenate([x, x], axis=1)

    def head_v(v_n, u):
        return jnp.where(low_half if u == 0 else ~low_half, v_n, jnp.ones_like(v_n))

    all_heads = [(pair, u) for pair in range(MOBA_HEADS // 2) for u in range(2)]
    kv_own = [kv(i, pair) for pair in range(MOBA_HEADS // 2)]
    s_own = [lax.dot_general(head_q(pair, u), kv_own[pair][0], NT_DIMS, preferred_element_type=f32)
             for pair, u in all_heads]
    s_own = [jnp.where(causal, x, neg_inf) for x in s_own]
    m_own = [jnp.broadcast_to(jnp.max(x, axis=1, keepdims=True), (blk, LANES)) for x in s_own]
    p_own = [jnp.exp2(x - wide(m)).astype(bf16) for x, m in zip(s_own, m_own)]
    for h, (pair, u) in enumerate(all_heads):
        m_ref[h] = m_own[h]
        acc_ref[h] = jnp.dot(p_own[h], head_v(kv_own[pair][1], u), preferred_element_type=f32)

    mask_b = sel_ref[...].astype(bf16)
    n_heads = MOBA_HEADS
    mask_lane = lambda u: MOBA_HEAD_DIM if u == 0 else 0
    hot_row = _iota((LANES, n_heads * LANES), 0)
    hot_col = _iota((LANES, n_heads * LANES), 1)
    hot_head = hot_col // LANES
    hot_lane_ok = hot_col % LANES == jnp.where(hot_head % 2 == 0, mask_lane(0), mask_lane(1))

    def past(n, carry):
        heads = [(pair, u) for pair in range(n_heads // 2) for u in range(2)]
        kvs = [kv(n, pair) for pair in range(n_heads // 2)]
        one_hot = (hot_lane_ok & (hot_row == hot_head * nb + n)).astype(bf16)
        bias = jnp.dot(mask_b, one_hot, preferred_element_type=f32).astype(bf16)
        q_m = [head_q(pair, u) + bias[:, (2 * pair + u) * LANES:(2 * pair + u + 1) * LANES] for pair, u in heads]
        k_e = [jnp.where(lane == mask_lane(u), jnp.ones_like(kvs[pair][0]), kvs[pair][0]) for pair, u in heads]
        s = [lax.dot_general(q, k, NT_DIMS, preferred_element_type=f32) for q, k in zip(q_m, k_e)]
        m_old = [m_ref[h] for h in range(n_heads)]
        m_new = [jnp.maximum(mo, jnp.broadcast_to(jnp.max(x, axis=1, keepdims=True), (blk, LANES)))
                 for mo, x in zip(m_old, s)]
        p = [jnp.exp2(x - wide(mn)).astype(bf16) for x, mn in zip(s, m_new)]
        pv = [jnp.dot(pp, head_v(kvs[pair][1], u), preferred_element_type=f32) for pp, (pair, u) in zip(p, heads)]
        for h in range(n_heads):
            m_ref[h] = m_new[h]
            acc_ref[h] = jnp.exp2(m_old[h] - m_new[h]) * acc_ref[h] + pv[h]
        return carry

    lax.fori_loop(0, i, past, 0)

    for pair in range(MOBA_HEADS // 2):
        halves = []
        for u in range(2):
            acc = acc_ref[2 * pair + u]
            halves.append(acc / pltpu.roll(acc, MOBA_HEAD_DIM, 1))
        y_ref[:, pair * LANES:(pair + 1) * LANES] = jnp.where(low_half, halves[0], halves[1]).astype(y_ref.dtype)


def moba_prompt(qb, kb, vb, sel, n_batch, seq):
    blk = MOBA_BLOCK
    nb = seq // blk
    row = lambda width: pl.BlockSpec((blk, width), lambda b, i: (b * nb + i, 0))
    full = pl.BlockSpec((seq, W), lambda b, i: (b, 0))
    scr = lambda: pltpu.VMEM((MOBA_HEADS, blk, LANES), f32)
    return pl.pallas_call(
        _moba_prompt_body,
        grid=(n_batch, nb),
        in_specs=[row(W), full, full, row(LANES)],
        out_specs=row(W),
        out_shape=jax.ShapeDtypeStruct((n_batch * seq, W), bf16),
        scratch_shapes=[scr(), scr()],
        compiler_params=_params(("arbitrary", "arbitrary")),
        name="moba_prompt",
    )(qb, kb, vb, sel)


def _moba_sample_body(n_pages, pt_ref, qb_ref, q_ref, k_ref, v_ref, *rest):
    del pt_ref
    k_pages, v_pages = rest[:n_pages], rest[n_pages:2 * n_pages]
    y_ref = rest[2 * n_pages]
    nh, dh = MOBA_HEADS, MOBA_HEAD_DIM
    pages_per_block = MOBA_BLOCK // PAGE_SIZE
    nb = n_pages // pages_per_block
    lane = _iota((nh, LANES), 1)
    q_b = qb_ref[...]

    def head_sum(x):
        return jnp.concatenate([jnp.sum(x[h * dh:(h + 1) * dh], axis=0, keepdims=True) for h in range(nh)], axis=0)

    def head_rows(x):
        return jnp.concatenate([jnp.broadcast_to(x[h:h + 1], (dh, x.shape[1])) for h in range(nh)], axis=0)

    s_pages = [head_sum(k_pages[g][...].reshape(nh * dh, PAGE_SIZE) * q_b) for g in range(n_pages)]
    gate = jnp.zeros((nh, LANES), f32)
    for n in range(nb):
        blk_sum = sum(jnp.sum(s_pages[n * pages_per_block + t], axis=1, keepdims=True)
                      for t in range(pages_per_block))
        gate = jnp.where(lane == n, blk_sum * (1.0 / MOBA_BLOCK), gate)
    sel = _topk_select(gate, nb, nb, 1).astype(f32)
    s_own = jnp.sum(q_ref[...] * k_ref[...], axis=1, keepdims=True)
    s_pages = [jnp.where(jnp.broadcast_to(sel[:, g // pages_per_block:g // pages_per_block + 1],
                                          (nh, PAGE_SIZE)) > 0.5, s, -jnp.inf)
               for g, s in enumerate(s_pages)]
    m = s_own
    for s in s_pages:
        m = jnp.maximum(m, jnp.max(s, axis=1, keepdims=True))
    p_pages = [jnp.exp(s - m) for s in s_pages]
    w_own = jnp.exp(s_own - m)
    l_tot = w_own + sum(jnp.sum(p, axis=1, keepdims=True) for p in p_pages)

    acc = jnp.zeros((nh * dh, PAGE_SIZE), f32)
    for g in range(n_pages):
        acc = acc + head_rows(p_pages[g]) * v_pages[g][...].reshape(nh * dh, PAGE_SIZE)
    hi = acc.astype(bf16)
    mid = (acc - hi.astype(f32))
    lo = (mid - mid.astype(bf16).astype(f32)).astype(bf16)
    ones = jnp.ones((nh, PAGE_SIZE), bf16)
    nt = lambda a: lax.dot_general(ones, a, NT_DIMS, preferred_element_type=f32)
    out = nt(hi) + nt(mid.astype(bf16)) + nt(lo)
    out = (out + w_own * v_ref[...]) / l_tot
    head_lane = _iota((nh, nh * dh), 1) // dh == _iota((nh, nh * dh), 0)
    y_ref[...] = jnp.sum(jnp.where(head_lane, out, 0.0), axis=0, keepdims=True)


def moba_sample(page_table_flat, q_b, q_s, k_s, v_row, cache_kt, cache_vt, layer, n_pages):
    n_seq = q_s.shape[0]
    nh, dh = MOBA_HEADS, MOBA_HEAD_DIM
    heads = pl.BlockSpec((None, nh, dh), lambda s, pt: (s, 0, 0))
    row = pl.BlockSpec((None, 1, W), lambda s, pt: (s, 0, 0))
    page = lambda u: pl.BlockSpec((None, None, nh, dh, PAGE_SIZE),
                                  lambda s, pt, u=u: (layer, pt[s * n_pages + u], 0, 0, 0))
    grid_spec = pltpu.PrefetchScalarGridSpec(
        num_scalar_prefetch=1,
        grid=(n_seq,),
        in_specs=[pl.BlockSpec((None, nh * dh, PAGE_SIZE), lambda s, pt: (s, 0, 0)), heads, heads, row]
        + [page(u) for u in range(n_pages)] + [page(u) for u in range(n_pages)],
        out_specs=row,
    )
    return pl.pallas_call(
        functools.partial(_moba_sample_body, n_pages),
        grid_spec=grid_spec,
        out_shape=jax.ShapeDtypeStruct((n_seq, 1, W), f32),
        compiler_params=_params(("arbitrary",)),
        name="moba_sample",
    )(page_table_flat, q_b, q_s, k_s, v_row, *([cache_kt] * n_pages), *([cache_vt] * n_pages))


def _rope_tables(pos):
    half = MOBA_HEAD_DIM // 2
    inv = ROPE_THETA ** (-jnp.arange(half, dtype=f32) / half)
    ang = pos.astype(f32)[:, None] * inv[None, :]
    cos, sin = jnp.cos(ang), jnp.sin(ang)
    cos_h = jnp.concatenate([cos, cos], -1)
    sin_h = jnp.concatenate([-sin, sin], -1)
    return jnp.tile(cos_h, (1, MOBA_HEADS)), jnp.tile(sin_h, (1, MOBA_HEADS))


def _pack_w_in(w):
    cols = lambda o, n: w[:, o:o + n]
    hk = GLA_HEADS * GLA_DK
    parts = [cols(_O_GQ, hk), cols(_O_GK, hk), cols(_O_GV, W), cols(_O_GO, W),
             cols(_O_MQ, W), cols(_O_MK, W), cols(_O_MV, W), cols(_O_PG, N_BRANCH * D_MODEL),
             cols(_O_PR, RWKV_PROJ), cols(_O_GLOW, GLA_GATE_RANK),
             jnp.zeros((w.shape[0], N_PACK - C_GLOW - GLA_GATE_RANK), w.dtype)]
    return jnp.concatenate(parts, axis=1).astype(bf16)


def _pad_rows(m, rows):
    return jnp.concatenate([m, jnp.zeros((rows - m.shape[0], m.shape[1]), m.dtype)], axis=0)


SUBLANES = 8
IN_PROJ_ROWS_TARGET, IN_PROJ_TN = 2048, 1024
TOKEN_ROWS_TARGET = 1024
MERGE_ROWS_TARGET = 512
MIXER_ROWS = 256


def _row_tile(m, target):
    best = SUBLANES
    for t in range(SUBLANES, target + 1, SUBLANES):
        if m % t == 0:
            best = t
    assert m % best == 0
    return best


def kernel(x_prompt, x_sample, cache_k, cache_v, page_table, state_gla, state_wkv, state_shift, w_in, b_gate, gla_gk_up, gla_gk_bias, gla_norm_w, rwkv_mu, rwkv_w0, rwkv_w_up, rwkv_a0, rwkv_a_up, rwkv_g_up, rwkv_k_k, rwkv_k_a, rwkv_r_k, rwkv_ln_w, rwkv_ln_b, w_branch, w_out, ln1_g, ln1_b, w_up, w_down, ln2_g, ln2_b):
    n_batch, seq, _ = x_prompt.shape
    n_seq = x_sample.shape[0]
    n_pages = page_table.shape[1]
    mp = n_batch * seq
    past = n_pages * PAGE_SIZE

    cos_p, sin_p = _rope_tables(jnp.arange(seq, dtype=jnp.int32))
    cos_s, sin_s = _rope_tables(jnp.full((1,), past, jnp.int32))
    pt_flat = page_table.reshape(-1)
    cache_kt = jnp.transpose(cache_k, (0, 1, 3, 4, 2))
    cache_vt = jnp.transpose(cache_v, (0, 1, 3, 4, 2))
    state_wkv_t = jnp.transpose(state_wkv, (0, 2, 3, 4, 1))

    x_p, x_s = x_prompt.reshape(mp, D_MODEL), x_sample.reshape(n_seq, D_MODEL)
    outs = {k: [] for k in ("k_p", "v_p", "k_s", "v_s", "gla_p", "gla_s", "wkv_p", "wkv_s", "shift_p", "shift_s")}
    for l in range(DEPTH):
        row2 = lambda v: v.reshape(1, -1)
        mu = rwkv_mu[l]
        rp = dict(
            mu_r=row2(mu[:W]), mu_k=row2(mu[W:2 * W]), mu_v=row2(mu[2 * W:3 * W]),
            mu_wa=row2(mu[3 * W:3 * W + LANES]), mu_gr=row2(mu[3 * W + LANES:]),
            w0=row2(rwkv_w0[l]), w_up=_pad_rows(rwkv_w_up[l], LANES).astype(bf16),
            a0=row2(rwkv_a0[l]),
            a_up=jnp.concatenate([jnp.zeros((RWKV_DECAY_RANK, W), f32), rwkv_a_up[l]], axis=0).astype(bf16),
            g_up=rwkv_g_up[l].astype(bf16), k_k=row2(rwkv_k_k[l]), k_a=row2(rwkv_k_a[l]),
            r_k=row2(rwkv_r_k[l]), ln_w=row2(rwkv_ln_w[l]), ln_b=row2(rwkv_ln_b[l]))
        up_pad = _pad_rows(gla_gk_up[l], LANES).astype(bf16)
        gk_bias = row2(gla_gk_bias[l])
        norm_w = row2(gla_norm_w[l])

        w_packed = _pack_w_in(w_in[l])
        wb, wo = w_branch[l].astype(bf16), w_out[l].astype(bf16)
        wu, wd = w_up[l].astype(bf16), w_down[l].astype(bf16)
        ln1, ln2 = (row2(ln1_g[l]), row2(ln1_b[l])), (row2(ln2_g[l]), row2(ln2_b[l]))

        def token_tail(x_rows, proj_rows, y_gla, y_moba, y_rwkv):
            tm = _row_tile(x_rows.shape[0], TOKEN_ROWS_TARGET)
            x1 = merge_ln(x_rows, proj_rows, y_gla, y_moba, y_rwkv, wb, b_gate[l], wo, *ln1,
                          _row_tile(x_rows.shape[0], MERGE_ROWS_TARGET))
            return mlp_ln(x1, wu, wd, *ln2, tm)

        proj_p = in_proj(x_p, w_packed, _row_tile(mp, IN_PROJ_ROWS_TARGET), IN_PROJ_TN)
        y_gla_p, gla_p = gla_prompt(proj_p, up_pad, gk_bias, norm_w, n_batch, seq, MIXER_ROWS)
        y_rwkv_p, wkv_p, shift_p = rwkv_prompt(proj_p, rp, n_batch, seq, MIXER_ROWS)
        k_rope, qb, kb, vb, sel = rope_prep(proj_p, cos_p, sin_p, n_batch, seq)
        y_moba_p = moba_prompt(qb, kb, vb, sel, n_batch, seq)
        x_p_next = token_tail(x_p, proj_p, y_gla_p, y_moba_p, y_rwkv_p)

        proj_s = in_proj(x_s, w_packed, n_seq, IN_PROJ_TN)
        y_gla_s, y_rwkv_s, q_s, k_s, gla_s, wkv_s_t = sample_step(
            proj_s, 0, n_seq, l, state_gla, state_wkv_t, state_shift, cos_s, sin_s, up_pad, gk_bias, norm_w, rp)
        v_s = proj_s[:, C_MV:C_MV + W]
        heads = (n_seq, MOBA_HEADS, MOBA_HEAD_DIM)
        q_lanes = jnp.broadcast_to(q_s[:, :, None], (n_seq, W, PAGE_SIZE))
        y_moba_s = moba_sample(pt_flat, q_lanes, q_s.reshape(heads), k_s.reshape(heads), v_s.reshape(n_seq, 1, W),
                               cache_kt, cache_vt, l, n_pages)
        x_s_next = token_tail(x_s, proj_s, y_gla_s, y_moba_s.reshape(n_seq, W).astype(bf16), y_rwkv_s)

        x_p, x_s = x_p_next, x_s_next
        hd = (MOBA_HEADS, MOBA_HEAD_DIM)
        outs["k_p"].append(k_rope.reshape(n_batch, seq, *hd))
        outs["v_p"].append(proj_p[:, C_MV:C_MV + W].reshape(n_batch, seq, *hd))
        outs["k_s"].append(k_s.reshape(n_seq, 1, *hd))
        outs["v_s"].append(v_s.reshape(n_seq, 1, *hd))
        outs["gla_p"].append(gla_p)
        outs["gla_s"].append(gla_s)
        wkv_p = wkv_p.reshape(n_batch, RWKV_HEADS // 2, RWKV_HEAD, 2, RWKV_HEAD)
        outs["wkv_p"].append(wkv_p.transpose(0, 1, 3, 2, 4).reshape(n_batch, RWKV_HEADS, RWKV_HEAD, RWKV_HEAD))
        outs["wkv_s"].append(jnp.transpose(wkv_s_t, (3, 0, 1, 2)))
        outs["shift_p"].append(shift_p[:, 0, :])
        outs["shift_s"].append(proj_s[:, C_PR:C_PR + RWKV_PROJ])

    st = lambda k: jnp.stack(outs[k])
    return (x_p.reshape(n_batch, seq, D_MODEL), x_s.reshape(n_seq, 1, D_MODEL),
            st("k_p"), st("v_p"), st("k_s"), st("v_s"), st("gla_p"), st("gla_s"),
            st("wkv_p"), st("wkv_s"), st("shift_p"), st("shift_s"))
```

```python
import functools

import jax
import jax.numpy as jnp
from jax import lax
from jax.experimental import pallas as pl
from jax.experimental.pallas import tpu as pltpu

f32, bf16 = jnp.float32, jnp.bfloat16

D_MODEL = 1024
DEPTH = 2
PAGE_SIZE = 128
BRANCH_WIDTH = D_MODEL // 2
N_BRANCH = 3
GLA_HEADS = 4
GLA_DV = BRANCH_WIDTH // GLA_HEADS
GLA_DK = GLA_DV // 2
GLA_GATE_RANK = 16
GLA_GATE_NORMALIZER = 16.0
GLA_CHUNK = 64
GLA_NORM_EPS = 1e-5
MOBA_HEAD_DIM = 64
MOBA_HEADS = BRANCH_WIDTH // MOBA_HEAD_DIM
MOBA_BLOCK = 256
MOBA_TOPK = 3
ROPE_THETA = 10000.0
RWKV_HEAD = 64
RWKV_HEADS = BRANCH_WIDTH // RWKV_HEAD
RWKV_DECAY_RANK = 64
RWKV_A_RANK = 64
RWKV_GATE_RANK = 128
RWKV_DECAY_SCALE = 0.606531
RWKV_GN_EPS = 64e-5
RWKV_PROJ = 3 * BRANCH_WIDTH + RWKV_DECAY_RANK + RWKV_A_RANK + RWKV_GATE_RANK
RWKV_CHUNK = 64
RWKV_INV_BLOCK = 16
D_FF = 4 * D_MODEL
ALPHA = (2 * DEPTH) ** 0.25
LN_EPS = 1e-5
LOG2_E = 1.4426950408889634
MASK_BIG = 2.0 ** 100

LANES = 128
VMEM_LIMIT_BYTES = 56 * 1024 * 1024

W = BRANCH_WIDTH
C_GQ, C_GK, C_GV, C_GO = 0, 256, 512, 1024
C_MQ, C_MK, C_MV = 1536, 2048, 2560
C_PG = 3072
C_PR = 6144
C_R, C_K7, C_V7 = C_PR, C_PR + W, C_PR + 2 * W
C_WA = C_PR + 3 * W
C_GR = C_WA + LANES
C_GLOW = C_PR + RWKV_PROJ
N_PACK = 8192

_O_GQ, _O_GK, _O_GV, _O_GLOW, _O_GO = 0, 256, 512, 1024, 1040
_O_MQ, _O_MK, _O_MV, _O_PR, _O_PG = 1552, 2064, 2576, 3088, 4880

NT_DIMS = (((1,), (1,)), ((), ()))
TN_DIMS = (((0,), (0,)), ((), ()))


def _params(sem):
    return pltpu.CompilerParams(dimension_semantics=sem, vmem_limit_bytes=VMEM_LIMIT_BYTES)


def _bdot(a, b):
    return jnp.dot(a.astype(bf16), b.astype(bf16), preferred_element_type=f32)


def _bdot_nt(a, b):
    return lax.dot_general(a.astype(bf16), b.astype(bf16), NT_DIMS, preferred_element_type=f32)


def _bdot_tn(a, b):
    return lax.dot_general(a.astype(bf16), b.astype(bf16), TN_DIMS, preferred_element_type=f32)


def _split2(x):
    hi = x.astype(bf16)
    lo = (x - hi.astype(f32)).astype(bf16)
    return hi, lo


def _dot_exact_rhs(x, m_bf16):
    hi, lo = _split2(x)
    return (jnp.dot(hi, m_bf16, preferred_element_type=f32)
            + jnp.dot(lo, m_bf16, preferred_element_type=f32))


def _dot_exact_lhs(m_bf16, x):
    hi, lo = _split2(x)
    return (jnp.dot(m_bf16, hi, preferred_element_type=f32)
            + jnp.dot(m_bf16, lo, preferred_element_type=f32))


def _dot3_nt(a, b):
    ah, al = _split2(a)
    bh, bl = _split2(b)
    d = lambda u, v: lax.dot_general(u, v, NT_DIMS, preferred_element_type=f32)
    return d(ah, bh) + d(al, bh) + d(ah, bl)


def _iota(shape, dim):
    return lax.broadcasted_iota(jnp.int32, shape, dim)


def _seg_matrix(n, seg):
    return (_iota((n, n), 0) // seg == _iota((n, n), 1) // seg).astype(bf16)


def _layer_norm(z, g, b):
    mu = jnp.mean(z, -1, keepdims=True)
    d = z - mu
    var = jnp.mean(d * d, -1, keepdims=True)
    return d * lax.rsqrt(var + LN_EPS) * g + b


def _row_to_col(row, n):
    eye = _iota((n, n), 0) == _iota((n, n), 1)
    return jnp.sum(jnp.where(eye, jnp.broadcast_to(row, (n, n)), 0.0), axis=1, keepdims=True)


def _set_row(ref, idx, row):
    cur = ref[...]
    ref[...] = jnp.where(_iota(cur.shape, 0) == idx, jnp.broadcast_to(row, cur.shape), cur)


def _col_to_row(col, n):
    eye = _iota((n, n), 0) == _iota((n, n), 1)
    return jnp.sum(jnp.where(eye, jnp.broadcast_to(col, (n, n)), 0.0), axis=0, keepdims=True)


def _inproj_body(x_ref, w_ref, o_ref, xb_ref):
    @pl.when(pl.program_id(1) == 0)
    def _():
        xb_ref[...] = x_ref[...].astype(bf16)

    o_ref[...] = jnp.dot(xb_ref[...], w_ref[...], preferred_element_type=f32)


def in_proj(x, w, tm, tn):
    m, k = x.shape
    n = w.shape[1]
    return pl.pallas_call(
        _inproj_body,
        grid=(m // tm, n // tn),
        in_specs=[pl.BlockSpec((tm, k), lambda i, j: (i, 0)),
                  pl.BlockSpec((k, tn), lambda i, j: (0, j))],
        out_specs=pl.BlockSpec((tm, tn), lambda i, j: (i, j)),
        out_shape=jax.ShapeDtypeStruct((m, n), f32),
        scratch_shapes=[pltpu.VMEM((tm, k), bf16)],
        compiler_params=_params(("arbitrary", "arbitrary")),
        name="in_proj",
    )(x, w)


def _merge_body(x_ref, pg0_ref, pg1_ref, pg2_ref, y0_ref, y1_ref, y2_ref, wb_ref, bg_ref, wo_ref,
                g_ref, b_ref, o_ref):
    merged = None
    for n, (pg_ref, y_ref) in enumerate(((pg0_ref, y0_ref), (pg1_ref, y1_ref), (pg2_ref, y2_ref))):
        gate = jax.nn.sigmoid(pg_ref[...] + bg_ref[n:n + 1, :])
        term = gate * jnp.dot(y_ref[...], wb_ref[n], preferred_element_type=f32)
        merged = term if merged is None else merged + term
    out = jnp.dot(merged.astype(bf16), wo_ref[...], preferred_element_type=f32)
    o_ref[...] = _layer_norm(ALPHA * x_ref[...] + out, g_ref[...], b_ref[...])


def merge_ln(x, proj, y_gla, y_moba, y_rwkv, w_branch, b_gate, w_out, g, b, tm):
    m = x.shape[0]
    row = lambda i: (i, 0)
    const2 = lambda i: (0, 0)
    pg_spec = lambda n: pl.BlockSpec((tm, D_MODEL), lambda i, n=n: (i, C_PG // D_MODEL + n))
    y_spec = pl.BlockSpec((tm, W), row)
    return pl.pallas_call(
        _merge_body,
        grid=(m // tm,),
        in_specs=[pl.BlockSpec((tm, D_MODEL), row), pg_spec(0), pg_spec(1), pg_spec(2),
                  y_spec, y_spec, y_spec,
                  pl.BlockSpec((N_BRANCH, W, D_MODEL), lambda i: (0, 0, 0)),
                  pl.BlockSpec((N_BRANCH, D_MODEL), const2),
                  pl.BlockSpec((D_MODEL, D_MODEL), const2),
                  pl.BlockSpec((1, D_MODEL), const2), pl.BlockSpec((1, D_MODEL), const2)],
        out_specs=pl.BlockSpec((tm, D_MODEL), row),
        out_shape=jax.ShapeDtypeStruct((m, D_MODEL), f32),
        compiler_params=_params(("arbitrary",)),
        name="merge_ln",
    )(x, proj, proj, proj, y_gla, y_moba, y_rwkv, w_branch, b_gate, w_out, g, b)


FF_CHUNK = 1024


def _mlp_body(x_ref, wu_ref, wd_ref, g_ref, b_ref, o_ref):
    x = x_ref[...]
    xb = x.astype(bf16)
    acc = None
    for c in range(D_FF // FF_CHUNK):
        h = jnp.dot(xb, wu_ref[:, c * FF_CHUNK:(c + 1) * FF_CHUNK], preferred_element_type=f32)
        h = jnp.square(jnp.maximum(h, 0.0)).astype(bf16)
        part = jnp.dot(h, wd_ref[c * FF_CHUNK:(c + 1) * FF_CHUNK, :], preferred_element_type=f32)
        acc = part if acc is None else acc + part
    o_ref[...] = _layer_norm(ALPHA * x + acc, g_ref[...], b_ref[...])


def mlp_ln(x, w_up, w_down, g, b, tm):
    m = x.shape[0]
    row = lambda i: (i, 0)
    const2 = lambda i: (0, 0)
    return pl.pallas_call(
        _mlp_body,
        grid=(m // tm,),
        in_specs=[pl.BlockSpec((tm, D_MODEL), row),
                  pl.BlockSpec((D_MODEL, D_FF), const2, pipeline_mode=pl.Buffered(1)),
                  pl.BlockSpec((D_FF, D_MODEL), const2, pipeline_mode=pl.Buffered(1)),
                  pl.BlockSpec((1, D_MODEL), const2), pl.BlockSpec((1, D_MODEL), const2)],
        out_specs=pl.BlockSpec((tm, D_MODEL), row),
        out_shape=jax.ShapeDtypeStruct((m, D_MODEL), f32),
        compiler_params=_params(("arbitrary",)),
        name="mlp_ln",
    )(x, w_up, w_down, g, b)


def _gla_log_gate(glow, up_ref, bias_ref):
    z = _bdot(glow, up_ref[...]) + bias_ref[...]
    return jax.nn.log_sigmoid(z) * (1.0 / GLA_GATE_NORMALIZER)


def _gla_out(o, norm_w, g_out):
    o = o * lax.rsqrt(jnp.mean(jnp.square(o), -1, keepdims=True) + GLA_NORM_EPS) * norm_w
    return o * jax.nn.silu(g_out)


def _gla_prompt_body(q_ref, k_ref, v_ref, go_ref, glow_ref, up_ref, bias_ref, nw_ref,
                     y_ref, s_out_ref, s_ref):
    t = pl.program_id(1)
    rows = q_ref.shape[0]
    c_sz = GLA_CHUNK

    @pl.when(t == 0)
    def _():
        s_ref[...] = jnp.zeros_like(s_ref)

    lg = _gla_log_gate(glow_ref[...], up_ref, bias_ref)
    tril = (_iota((c_sz, c_sz), 1) <= _iota((c_sz, c_sz), 0))
    tril_b = tril.astype(bf16)
    n_chunks = rows // c_sz
    probs = []
    for c in range(n_chunks):
        rs = slice(c * c_sz, (c + 1) * c_sz)
        g_cum = _dot_exact_lhs(tril_b, lg[rs])
        g_last = g_cum[c_sz - 1:c_sz, :]
        q_dec = q_ref[rs, :] * (GLA_DK ** -0.5) * jnp.exp(g_cum)
        k = k_ref[rs, :]
        k_inv = k * jnp.exp(-g_cum)
        k_tail = k * jnp.exp(g_last - g_cum)
        e_last = jnp.exp(g_last)
        for h in range(GLA_HEADS):
            ks = slice(h * GLA_DK, (h + 1) * GLA_DK)
            vs = slice(h * GLA_DV, (h + 1) * GLA_DV)
            probs.append((q_dec[:, ks], k_inv[:, ks], k_tail[:, ks], e_last[:, ks], v_ref[rs, vs], rs, vs))
    a = [jnp.where(tril, _bdot_nt(q[0], q[1]), 0.0) for q in probs]
    o_intra = [_bdot(x, q[4]) for x, q in zip(a, probs)]
    kv = [_bdot_tn(q[2], q[4]) for q in probs]
    e_col = [_row_to_col(q[3], GLA_DK) for q in probs]

    state = [s_ref[h] for h in range(GLA_HEADS)]
    for c in range(n_chunks):
        sl = slice(c * GLA_HEADS, (c + 1) * GLA_HEADS)
        o = [oi + _bdot(q[0], s) for oi, q, s in zip(o_intra[sl], probs[sl], state)]
        state = [s * e + x for s, e, x in zip(state, e_col[sl], kv[sl])]
        for oo, q in zip(o, probs[sl]):
            rs, vs = q[5], q[6]
            y_ref[rs, vs] = _gla_out(oo, nw_ref[...], go_ref[rs, vs]).astype(y_ref.dtype)
    for h in range(GLA_HEADS):
        s_ref[h] = state[h]
    s_out_ref[...] = s_ref[...]


def gla_prompt(proj, up_pad, bias, norm_w, n_batch, seq, rt):
    nt = seq // rt
    rowmap = lambda width, col: pl.BlockSpec((rt, width), lambda b, t: (b * nt + t, col // width))
    const2 = lambda b, t: (0, 0)
    return pl.pallas_call(
        _gla_prompt_body,
        grid=(n_batch, nt),
        in_specs=[rowmap(GLA_HEADS * GLA_DK, C_GQ), rowmap(GLA_HEADS * GLA_DK, C_GK), rowmap(W, C_GV),
                  rowmap(W, C_GO), rowmap(LANES, C_GLOW),
                  pl.BlockSpec((LANES, GLA_HEADS * GLA_DK), const2),
                  pl.BlockSpec((1, GLA_HEADS * GLA_DK), const2),
                  pl.BlockSpec((1, GLA_DV), const2)],
        out_specs=[pl.BlockSpec((rt, W), lambda b, t: (b * nt + t, 0)),
                   pl.BlockSpec((None, GLA_HEADS, GLA_DK, GLA_DV), lambda b, t: (b, 0, 0, 0))],
        out_shape=[jax.ShapeDtypeStruct((n_batch * seq, W), bf16),
                   jax.ShapeDtypeStruct((n_batch, GLA_HEADS, GLA_DK, GLA_DV), f32)],
        scratch_shapes=[pltpu.VMEM((GLA_HEADS, GLA_DK, GLA_DV), f32)],
        compiler_params=_params(("arbitrary", "arbitrary")),
        name="gla_prompt",
    )(proj, proj, proj, proj, proj, up_pad, bias, norm_w)


def _rwkv_pre(r_s, k_s, v_s, wa_s, gr_s, p):
    seg = _seg_matrix(W, RWKV_HEAD)
    logw = -RWKV_DECAY_SCALE * jax.nn.sigmoid(p["w0"][...] + _bdot(jnp.tanh(wa_s), p["w_up"][...]))
    a = jax.nn.sigmoid(p["a0"][...] + _bdot(wa_s, p["a_up"][...]))
    g7 = _bdot(jax.nn.sigmoid(gr_s), p["g_up"][...])
    kk = k_s * p["k_k"][...]
    kk = kk * lax.rsqrt(_dot_exact_rhs(jnp.square(kk), seg) + 1e-12)
    k7 = k_s * (1.0 + (a - 1.0) * p["k_a"][...])
    bonus = _dot_exact_rhs(r_s * k7 * p["r_k"][...], seg) * v_s
    return dict(r=r_s, logw=logw, k=k7, v=v_s, kk=kk, a=a, g7=g7, bonus=bonus)


def _rwkv_post(yw, bonus, g7, p):
    seg = _seg_matrix(W, RWKV_HEAD)
    inv_n = 1.0 / RWKV_HEAD
    mu = _dot_exact_rhs(yw, seg) * inv_n
    d = yw - mu
    var = _dot_exact_rhs(jnp.square(d), seg) * inv_n
    yn = d * lax.rsqrt(var + RWKV_GN_EPS) * p["ln_w"][...] + p["ln_b"][...]
    return (yn + bonus) * g7


def _pair_bd(x):
    low = _iota(x.shape, 1) < x.shape[1] // 2
    zero = jnp.zeros_like(x)
    return jnp.concatenate([jnp.where(low, x, zero), jnp.where(low, zero, x)], axis=0)


def _pair_mm(a, b):
    return jnp.dot(a.astype(bf16), _pair_bd(b.astype(bf16)), preferred_element_type=f32)


def _pair_nt(a, bd_b):
    return lax.dot_general(a.astype(bf16), bd_b, NT_DIMS, preferred_element_type=f32)


def _pair_tn(a, b):
    full = lax.dot_general(a.astype(bf16), b.astype(bf16), TN_DIMS, preferred_element_type=f32)
    p = b.shape[1] // 2
    low = _iota((p, b.shape[1]), 1) < p
    groups = full.shape[0] // (2 * p)
    return jnp.concatenate([jnp.where(low, full[2 * g * p:(2 * g + 1) * p], full[(2 * g + 1) * p:(2 * g + 2) * p])
                            for g in range(groups)], axis=0)


def _inv_unit_lower(n_mats):
    c = n_mats[0].shape[0]
    row, col = _iota((c, 2 * c), 0), _iota((c, 2 * c), 1) % c
    eye = (row == col).astype(f32)
    blk = (row // RWKV_INV_BLOCK) == (col // RWKV_INV_BLOCK)
    d = [jnp.where(blk, n, 0.0) for n in n_mats]
    lo = [n - x for n, x in zip(n_mats, d)]
    d2 = [_pair_mm(x, x) for x in d]
    d4 = [_pair_mm(x, x) for x in d2]
    d8 = [_pair_mm(x, x) for x in d4]
    pa = [_pair_mm(eye - x, eye + y) for x, y in zip(d, d2)]
    pb = [_pair_mm(eye + x, eye + y) for x, y in zip(d4, d8)]
    d_inv = [_pair_mm(x, y) for x, y in zip(pa, pb)]
    e = [_pair_mm(x, y) for x, y in zip(d_inv, lo)]
    e2 = [_pair_mm(x, x) for x in e]
    f = [_pair_mm(eye - x, eye + y) for x, y in zip(e, e2)]
    return [_pair_mm(x, y) for x, y in zip(f, d_inv)]


def _rwkv_chunk_coeffs(probs):
    c, width = probs[0][0].shape
    row, col = _iota((c, width), 0), _iota((c, width), 1) % (width // 2)
    strict, incl, eye = col < row, col <= row, col == row
    lhs = [jnp.concatenate([q[1], q[0]], 0).astype(bf16) for q in probs]
    with_b = [_pair_nt(x, _pair_bd(q[2].astype(bf16))) for x, q in zip(lhs, probs)]
    with_k = [_pair_nt(x, _pair_bd(q[3].astype(bf16))) for x, q in zip(lhs, probs)]
    n_ab = [jnp.where(strict, x[:c], 0.0) for x in with_b]
    p_rb = [jnp.where(incl, x[c:], 0.0) for x in with_b]
    a_ak = [jnp.where(strict, x[:c], 0.0) for x in with_k]
    p_rk = [jnp.where(incl, x[c:], 0.0) for x in with_k]
    t_inv = _inv_unit_lower(n_ab)
    av = [_pair_mm(x, q[6]) for x, q in zip(a_ak, probs)]
    two = lambda u, w: jnp.concatenate([_pair_bd(u.astype(bf16)), _pair_bd(w.astype(bf16))], axis=1)
    x = [jnp.dot(ti.astype(bf16), two(q[1], a), preferred_element_type=f32)
         for ti, q, a in zip(t_inv, probs, av)]
    px = [jnp.dot(pr.astype(bf16), two(xx[:, :width], xx[:, width:]), preferred_element_type=f32)
          for pr, xx in zip(p_rb, x)]
    prkv = [_pair_mm(pr, q[6]) for pr, q in zip(p_rk, probs)]
    xb = [_pair_tn(xx, q[4]) for xx, q in zip(x, probs)]
    vk = [_pair_tn(q[6], q[5]) for q in probs]
    out = []
    for q, pxx, pv, xbb, vkk in zip(probs, px, prkv, xb, vk):
        tm_t = jnp.where(eye, jnp.broadcast_to(q[7], (c, width)), 0.0) - xbb[:c]
        out.append((q[0] - pxx[:, :width], pv - pxx[:, width:], tm_t, vkk - xbb[c:]))
    return out


_RWKV_PARAM_NAMES = ("mu_r", "mu_k", "mu_v", "mu_wa", "mu_gr", "w0", "w_up", "a0", "a_up", "g_up",
                     "k_k", "k_a", "r_k", "ln_w", "ln_b")


def _rwkv_prompt_body(layer, n_pages, seqs_per_step, pt_ref, r_ref, k_ref, v_ref, wa_ref, gr_ref, *rest):
    np_ = len(_RWKV_PARAM_NAMES)
    p = dict(zip(_RWKV_PARAM_NAMES, rest[:np_]))
    qb_ref, qh_ref, kh_ref, vrow_ref, cache_k_hbm, cache_v_hbm = rest[np_:np_ + 6]
    y_ref, s_out_ref, shift_out_ref, y_moba_ref = rest[np_ + 6:np_ + 10]
    s_ref, prev_ref, k_buf, v_buf, sem = rest[np_ + 10:]
    t = pl.program_id(1)
    step = pl.program_id(0) * pl.num_programs(1) + t
    n_steps = pl.num_programs(0) * pl.num_programs(1)
    pages_per_step = seqs_per_step * n_pages
    rows = r_ref.shape[0]
    c_sz = RWKV_CHUNK

    def page_copies(of_step, slot):
        out = []
        for j in range(pages_per_step):
            page = pt_ref[of_step * pages_per_step + j]
            out.append(pltpu.make_async_copy(cache_k_hbm.at[layer, page], k_buf.at[slot, j], sem.at[0, slot, j]))
            out.append(pltpu.make_async_copy(cache_v_hbm.at[layer, page], v_buf.at[slot, j], sem.at[1, slot, j]))
        return out

    @pl.when(step == 0)
    def _():
        for c in page_copies(0, 0):
            c.start()

    slot = step % 2
    for c in page_copies(step, slot):
        c.wait()

    @pl.when(step + 1 < n_steps)
    def _():
        for c in page_copies(step + 1, (step + 1) % 2):
            c.start()

    @pl.when(t == 0)
    def _():
        s_ref[...] = jnp.zeros_like(s_ref)
        prev_ref[...] = jnp.zeros_like(prev_ref)

    def shifted(x_ref, mu_ref, c0):
        x = x_ref[...]
        width = x.shape[1]
        prev = jnp.where(_iota(x.shape, 0) == 0, prev_ref[0:1, c0:c0 + width], pltpu.roll(x, 1, 0))
        prev_ref[0:1, c0:c0 + width] = x[rows - 1:rows, :]
        return x + mu_ref[...] * (prev - x)

    pre = _rwkv_pre(shifted(r_ref, p["mu_r"], 0), shifted(k_ref, p["mu_k"], W), shifted(v_ref, p["mu_v"], 2 * W),
                    shifted(wa_ref, p["mu_wa"], 3 * W), shifted(gr_ref, p["mu_gr"], 3 * W + LANES), p)
    tril_b = (_iota((c_sz, c_sz), 1) <= _iota((c_sz, c_sz), 0)).astype(bf16)
    n_chunks = rows // c_sz
    n_pairs = RWKV_HEADS // 2
    b_all = pre["kk"] * pre["a"]

    probs = []
    for c in range(n_chunks):
        rs = slice(c * c_sz, (c + 1) * c_sz)
        lw = pre["logw"][rs]
        g_cum = _dot_exact_lhs(tril_b, lw)
        g_last = g_cum[c_sz - 1:c_sz, :]
        e_ng = jnp.exp(-g_cum)
        e_gl = jnp.exp(g_last - g_cum)
        kk, b, k, v = pre["kk"][rs], b_all[rs], pre["k"][rs], pre["v"][rs]
        r_t = pre["r"][rs] * jnp.exp(g_cum)
        a_t = kk * jnp.exp(g_cum - lw)
        b_t, k_t, b_g, k_g = b * e_ng, k * e_ng, b * e_gl, k * e_gl
        gam = jnp.exp(g_last)
        for hp in range(n_pairs):
            hs = slice(hp * LANES, (hp + 1) * LANES)
            probs.append((r_t[:, hs], a_t[:, hs], b_t[:, hs], k_t[:, hs], b_g[:, hs], k_g[:, hs], v[:, hs],
                          gam[:, hs]))
    coeffs = _rwkv_chunk_coeffs(probs)

    state = [s_ref[hp] for hp in range(n_pairs)]
    y_rows = []
    for c in range(n_chunks):
        cf = coeffs[c * n_pairs:(c + 1) * n_pairs]
        bd_s = [_pair_bd(s.astype(bf16)) for s in state]
        y_rows.append(jnp.concatenate([_pair_nt(q[0], bd) + q[1] for q, bd in zip(cf, bd_s)], axis=1))
        state = [_pair_mm(s, q[2]) + q[3] for q, s in zip(cf, state)]
    for hp in range(n_pairs):
        s_ref[hp] = state[hp]
    yw = jnp.concatenate(y_rows, axis=0)
    y_ref[...] = _rwkv_post(yw, pre["bonus"], pre["g7"], p).astype(y_ref.dtype)
    s_out_ref[...] = s_ref[...]
    shift_out_ref[...] = prev_ref[...]

    for u in range(seqs_per_step):
        y_moba_ref[u] = _moba_sample_seq(
            qb_ref[u], qh_ref[u], kh_ref[u], vrow_ref[u],
            lambda g, u=u: k_buf[slot, u * n_pages + g], lambda g, u=u: v_buf[slot, u * n_pages + g], n_pages)


def _rwkv_param_specs(const_map):
    vec = lambda width: pl.BlockSpec((1, width), const_map)
    mat = lambda r: pl.BlockSpec((r, W), const_map)
    return [vec(W), vec(W), vec(W), vec(LANES), vec(LANES), vec(W), mat(LANES), vec(W), mat(LANES),
            mat(RWKV_GATE_RANK), vec(W), vec(W), vec(W), vec(W), vec(W)]


def rwkv_prompt_moba_sample(proj, params, n_batch, seq, rt, layer, page_table_flat, q_lanes, q_heads, k_heads,
                            v_row, cache_kt, cache_vt, n_pages):
    nt = seq // rt
    n_seq = q_heads.shape[0]
    n_steps = n_batch * nt
    assert n_seq % n_steps == 0, "the sample sequences are spread evenly over the prompt row tiles"
    sps = n_seq // n_steps
    nh, dh = MOBA_HEADS, MOBA_HEAD_DIM
    rowmap = lambda width, col: pl.BlockSpec((rt, width), lambda b, t, pt: (b * nt + t, col // width))
    const2 = lambda b, t, pt: (0, 0)
    per_step = lambda *tail: pl.BlockSpec((sps,) + tail, lambda b, t, pt: (b * nt + t,) + (0,) * len(tail))
    grid_spec = pltpu.PrefetchScalarGridSpec(
        num_scalar_prefetch=1,
        grid=(n_batch, nt),
        in_specs=[rowmap(W, C_R), rowmap(W, C_K7), rowmap(W, C_V7), rowmap(LANES, C_WA), rowmap(LANES, C_GR)]
        + _rwkv_param_specs(const2)
        + [per_step(nh * dh, PAGE_SIZE), per_step(nh, dh), per_step(nh, dh), per_step(1, W),
           pl.BlockSpec(memory_space=pl.ANY), pl.BlockSpec(memory_space=pl.ANY)],
        out_specs=[pl.BlockSpec((rt, W), lambda b, t, pt: (b * nt + t, 0)),
                   pl.BlockSpec((None, RWKV_HEADS // 2, RWKV_HEAD, 2 * RWKV_HEAD), lambda b, t, pt: (b, 0, 0, 0)),
                   pl.BlockSpec((None, SUBLANES, RWKV_PROJ), lambda b, t, pt: (b, 0, 0)),
                   per_step(1, W)],
        scratch_shapes=[pltpu.VMEM((RWKV_HEADS // 2, RWKV_HEAD, 2 * RWKV_HEAD), f32),
                        pltpu.VMEM((SUBLANES, RWKV_PROJ), f32),
                        pltpu.VMEM((2, sps * n_pages, nh, dh, PAGE_SIZE), f32),
                        pltpu.VMEM((2, sps * n_pages, nh, dh, PAGE_SIZE), f32),
                        pltpu.SemaphoreType.DMA((2, 2, sps * n_pages))],
    )
    return pl.pallas_call(
        functools.partial(_rwkv_prompt_body, layer, n_pages, sps),
        grid_spec=grid_spec,
        out_shape=[jax.ShapeDtypeStruct((n_batch * seq, W), bf16),
                   jax.ShapeDtypeStruct((n_batch, RWKV_HEADS // 2, RWKV_HEAD, 2 * RWKV_HEAD), f32),
                   jax.ShapeDtypeStruct((n_batch, SUBLANES, RWKV_PROJ), f32),
                   jax.ShapeDtypeStruct((n_seq, 1, W), f32)],
        compiler_params=_params(("arbitrary", "arbitrary")),
        name="rwkv_prompt_moba_sample",
    )(page_table_flat, proj, proj, proj, proj, proj, *[params[n] for n in _RWKV_PARAM_NAMES],
      q_lanes, q_heads, k_heads, v_row, cache_kt, cache_vt)


def _rope(x, cos, sin_signed):
    lane = _iota(x.shape, 1)
    half = MOBA_HEAD_DIM // 2
    width = x.shape[1]
    swapped = jnp.where(lane % MOBA_HEAD_DIM < half, pltpu.roll(x, width - half, 1), pltpu.roll(x, half, 1))
    return x * cos + swapped * sin_signed


def _sample_body(gq_ref, gk_ref, gv_ref, go_ref, glow_ref, r_ref, k_ref, v_ref, wa_ref, gr_ref, mq_ref, mk_ref,
                 sg_ref, sw_ref, shift_ref, cos_ref, sin_ref, up_ref, bias_ref, nw_ref, *rest):
    np_ = len(_RWKV_PARAM_NAMES)
    p = dict(zip(_RWKV_PARAM_NAMES, rest[:np_]))
    yg_ref, yr_ref, qs_ref, ks_ref, sg_out, sw_out = rest[np_:np_ + 6]
    qt_s, kt_s, egt_s, og_s, rt_s, wt_s, k7t_s, vt_s, kkt_s, bt_s, yt_s, bonus_s, g7_s = rest[np_ + 6:]
    hp = pl.program_id(0)
    n_seq = gq_ref.shape[0]

    @pl.when(hp == 0)
    def _():
        qs_ref[...] = _rope(mq_ref[...], cos_ref[...], sin_ref[...]) * (MOBA_HEAD_DIM ** -0.5)
        ks_ref[...] = _rope(mk_ref[...], cos_ref[...], sin_ref[...])
        lg = _gla_log_gate(glow_ref[...], up_ref, bias_ref)
        qt_s[...] = (gq_ref[...] * (GLA_DK ** -0.5)).T
        kt_s[...] = gk_ref[...].T
        egt_s[...] = jnp.exp(lg).T

        def shifted(x_ref, mu_ref, c0):
            x = x_ref[...]
            return x + mu_ref[...] * (shift_ref[:, c0:c0 + x.shape[1]] - x)

        pre = _rwkv_pre(shifted(r_ref, p["mu_r"], 0), shifted(k_ref, p["mu_k"], W), shifted(v_ref, p["mu_v"], 2 * W),
                        shifted(wa_ref, p["mu_wa"], 3 * W), shifted(gr_ref, p["mu_gr"], 3 * W + LANES), p)
        rt_s[...] = pre["r"].T
        wt_s[...] = jnp.exp(pre["logw"]).T
        k7t_s[...] = pre["k"].T
        vt_s[...] = pre["v"].T
        kkt_s[...] = pre["kk"].T
        bt_s[...] = (pre["kk"] * pre["a"]).T
        bonus_s[...] = pre["bonus"]
        g7_s[...] = pre["g7"]
        og_s[...] = jnp.zeros_like(og_s)

    ks = pl.ds(pl.multiple_of(hp * GLA_DK, GLA_DK), GLA_DK)
    eg_t, k_t, q_t = egt_s[ks, :], kt_s[ks, :], qt_s[ks, :]
    gv_h = gv_ref[:, pl.ds(pl.multiple_of(hp * GLA_DV, GLA_DV), GLA_DV)]
    lane_grp = _iota((n_seq, W), 1) // GLA_DV
    o_rows = []
    for s in range(n_seq):
        col = lambda x: jnp.broadcast_to(x[:, s:s + 1], (GLA_DK, GLA_DV))
        st = sg_ref[s] * col(eg_t) + col(k_t) * gv_h[s:s + 1, :]
        sg_out[s] = st
        o_rows.append(jnp.sum(col(q_t) * st, axis=0, keepdims=True))
    o_h = jnp.concatenate(o_rows, axis=0)
    og_s[...] = jnp.where(lane_grp == hp, jnp.concatenate([o_h] * GLA_HEADS, axis=1), og_s[...])

    group = RWKV_HEAD // SUBLANES
    for u in range(2):
        hs = pl.ds(pl.multiple_of((2 * hp + u) * RWKV_HEAD, RWKV_HEAD), RWKV_HEAD)
        neg_kk, w_t, b_t, k_t7, r_t = -kkt_s[hs, :], wt_s[hs, :], bt_s[hs, :], k7t_s[hs, :], rt_s[hs, :]

        def v_group(g, carry, u=u, hs=hs, neg_kk=neg_kk, w_t=w_t, b_t=b_t, k_t7=k_t7, r_t=r_t):
            base = pl.multiple_of((2 * hp + u) * RWKV_HEAD + g * SUBLANES, SUBLANES)
            v_rows = vt_s[pl.ds(base, SUBLANES), :]
            y_rows = []
            for j in range(SUBLANES):
                vi = g * SUBLANES + j
                st = sw_ref[u, vi]
                sa = jnp.sum(st * neg_kk, axis=0, keepdims=True)
                st = st * w_t + sa * b_t + v_rows[j:j + 1, :] * k_t7
                sw_out[u, vi] = st
                y_rows.append(jnp.sum(st * r_t, axis=0, keepdims=True))
            yt_s[pl.ds(base, SUBLANES), :] = jnp.concatenate(y_rows, axis=0)
            return carry

        lax.fori_loop(0, group, v_group, 0)

    @pl.when(hp == pl.num_programs(0) - 1)
    def _():
        og = og_s[...]
        for h in range(GLA_HEADS):
            vs = slice(h * GLA_DV, (h + 1) * GLA_DV)
            yg_ref[:, vs] = _gla_out(og[:, vs], nw_ref[...], go_ref[:, vs]).astype(yg_ref.dtype)
        yr_ref[...] = _rwkv_post(yt_s[...].T, bonus_s[...], g7_s[...], p).astype(yr_ref.dtype)


def sample_step(proj, row0, n_seq, layer, state_gla, state_wkv_t, state_shift, cos_s, sin_s,
                up_pad, bias, norm_w, params):
    r0 = row0 // n_seq
    rowmap = lambda width, col: pl.BlockSpec((n_seq, width), lambda h: (r0, col // width))
    const2 = lambda h: (0, 0)
    hk = GLA_HEADS * GLA_DK
    out_row = pl.BlockSpec((n_seq, W), const2)
    tr = lambda rows: pltpu.VMEM((rows, n_seq), f32)
    rw = lambda: pltpu.VMEM((n_seq, W), f32)
    return pl.pallas_call(
        _sample_body,
        grid=(GLA_HEADS,),
        in_specs=[rowmap(hk, C_GQ), rowmap(hk, C_GK), rowmap(W, C_GV), rowmap(W, C_GO), rowmap(LANES, C_GLOW),
                  rowmap(W, C_R), rowmap(W, C_K7), rowmap(W, C_V7), rowmap(LANES, C_WA), rowmap(LANES, C_GR),
                  rowmap(W, C_MQ), rowmap(W, C_MK),
                  pl.BlockSpec((None, n_seq, None, GLA_DK, GLA_DV), lambda h: (layer, 0, h, 0, 0)),
                  pl.BlockSpec((None, 2, RWKV_HEAD, RWKV_HEAD, n_seq), lambda h: (layer, h, 0, 0, 0)),
                  pl.BlockSpec((None, n_seq, RWKV_PROJ), lambda h: (layer, 0, 0)),
                  pl.BlockSpec((1, W), const2), pl.BlockSpec((1, W), const2),
                  pl.BlockSpec((LANES, hk), const2), pl.BlockSpec((1, hk), const2), pl.BlockSpec((1, GLA_DV), const2)]
        + _rwkv_param_specs(const2),
        out_specs=[out_row, out_row, out_row, out_row,
                   pl.BlockSpec((n_seq, None, GLA_DK, GLA_DV), lambda h: (0, h, 0, 0)),
                   pl.BlockSpec((2, RWKV_HEAD, RWKV_HEAD, n_seq), lambda h: (h, 0, 0, 0))],
        out_shape=[jax.ShapeDtypeStruct((n_seq, W), bf16), jax.ShapeDtypeStruct((n_seq, W), bf16),
                   jax.ShapeDtypeStruct((n_seq, W), f32), jax.ShapeDtypeStruct((n_seq, W), f32),
                   jax.ShapeDtypeStruct((n_seq, GLA_HEADS, GLA_DK, GLA_DV), f32),
                   jax.ShapeDtypeStruct((RWKV_HEADS, RWKV_HEAD, RWKV_HEAD, n_seq), f32)],
        scratch_shapes=[tr(hk), tr(hk), tr(hk), rw(), tr(W), tr(W), tr(W), tr(W), tr(W), tr(W), tr(W), rw(), rw()],
        compiler_params=_params(("arbitrary",)),
        name="sample_step",
    )(proj, proj, proj, proj, proj, proj, proj, proj, proj, proj, proj, proj,
      state_gla, state_wkv_t, state_shift, cos_s, sin_s, up_pad, bias, norm_w,
      *[params[n] for n in _RWKV_PARAM_NAMES])


def _topk_select(gate, n_valid, axis_len, axis):
    idx = _iota(gate.shape, axis)
    cnt = jnp.zeros(gate.shape, f32)
    for m in range(axis_len):
        g_m = lax.slice_in_dim(gate, m, m + 1, axis=axis)
        g_m = jnp.broadcast_to(g_m, gate.shape)
        beats = (g_m > gate) | ((g_m == gate) & (m < idx))
        cnt = cnt + jnp.where(beats, 1.0, 0.0) * jnp.where(m < n_valid, 1.0, 0.0)
    return (idx < n_valid) & (cnt < MOBA_TOPK)


def _rope_prep_body(mq_ref, mk_ref, mv_ref, cos_ref, sin_ref, k_out, qb_out, kb_out, vb_out, sel_out, km_ref):
    i = pl.program_id(1)
    blk = mq_ref.shape[0]
    nb = km_ref.shape[0]
    nh = MOBA_HEADS

    @pl.when(i == 0)
    def _():
        km_ref[...] = jnp.zeros_like(km_ref)

    q = _rope(mq_ref[...], cos_ref[...], sin_ref[...]) * (MOBA_HEAD_DIM ** -0.5)
    k = _rope(mk_ref[...], cos_ref[...], sin_ref[...])
    k_out[...] = k
    qb_out[...] = (q * LOG2_E).astype(bf16)
    kb_out[...] = k.astype(bf16)
    vb_out[...] = mv_ref[...].astype(bf16)

    km = km_ref[...]
    km_rows = jnp.concatenate([km] * nh, axis=0)
    head_of_row = _iota(km_rows.shape, 0) // nb
    km_rows = jnp.where(head_of_row == _iota(km_rows.shape, 1) // MOBA_HEAD_DIM, km_rows, 0.0)
    gate = _dot3_nt(km_rows, q).reshape(nh, nb, blk)
    sel = _topk_select(gate, i, nb, 1).astype(bf16).reshape(nh * nb, blk)
    sel = jnp.concatenate([sel, jnp.zeros((LANES - nh * nb, blk), bf16)], axis=0)
    eye = (_iota((blk, blk), 0) == _iota((blk, blk), 1)).astype(bf16)
    sel_out[...] = (lax.dot_general(eye, sel, NT_DIMS, preferred_element_type=f32) - 1.0) * MASK_BIG

    _set_row(km_ref, i, jnp.sum(k, axis=0, keepdims=True) * (1.0 / blk))


def rope_prep(proj, cos, sin, n_batch, seq):
    blk = MOBA_BLOCK
    nb = seq // blk
    m = n_batch * seq
    rowmap = lambda col: pl.BlockSpec((blk, W), lambda b, i: (b * nb + i, col // W))
    tab = pl.BlockSpec((blk, W), lambda b, i: (i, 0))
    out = lambda width: pl.BlockSpec((blk, width), lambda b, i: (b * nb + i, 0))
    return pl.pallas_call(
        _rope_prep_body,
        grid=(n_batch, nb),
        in_specs=[rowmap(C_MQ), rowmap(C_MK), rowmap(C_MV), tab, tab],
        out_specs=[out(W), out(W), out(W), out(W), out(LANES)],
        out_shape=[jax.ShapeDtypeStruct((m, W), f32), jax.ShapeDtypeStruct((m, W), bf16),
                   jax.ShapeDtypeStruct((m, W), bf16), jax.ShapeDtypeStruct((m, W), bf16),
                   jax.ShapeDtypeStruct((m, LANES), f32)],
        scratch_shapes=[pltpu.VMEM((nb, W), f32)],
        compiler_params=_params(("arbitrary", "arbitrary")),
        name="rope_prep",
    )(proj, proj, proj, cos, sin)


def _moba_prompt_body(qb_ref, kb_ref, vb_ref, sel_ref, y_ref, m_ref, acc_ref):
    i = pl.program_id(1)
    blk = qb_ref.shape[0]
    nb = kb_ref.shape[0] // blk
    lane = _iota((blk, LANES), 1)
    low_half = lane < MOBA_HEAD_DIM
    causal = _iota((blk, blk), 1) <= _iota((blk, blk), 0)
    neg_inf = -jnp.inf

    def head_q(pair, u):
        qp = qb_ref[:, pair * LANES:(pair + 1) * LANES]
        return jnp.where(low_half if u == 0 else ~low_half, qp, jnp.zeros_like(qp))

    def kv(n, pair):
        rs = pl.ds(pl.multiple_of(n * blk, blk), blk)
        ls = slice(pair * LANES, (pair + 1) * LANES)
        return kb_ref[rs, ls], vb_ref[rs, ls]

    wide = lambda x: jnp.concatenate([x, x], axis=1)

    def head_v(v_n, u):
        return jnp.where(low_half if u == 0 else ~low_half, v_n, jnp.ones_like(v_n))

    all_heads = [(pair, u) for pair in range(MOBA_HEADS // 2) for u in range(2)]
    kv_own = [kv(i, pair) for pair in range(MOBA_HEADS // 2)]
    s_own = [lax.dot_general(head_q(pair, u), kv_own[pair][0], NT_DIMS, preferred_element_type=f32)
             for pair, u in all_heads]
    s_own = [jnp.where(causal, x, neg_inf) for x in s_own]
    m_own = [jnp.broadcast_to(jnp.max(x, axis=1, keepdims=True), (blk, LANES)) for x in s_own]
    p_own = [jnp.exp2(x - wide(m)).astype(bf16) for x, m in zip(s_own, m_own)]
    for h, (pair, u) in enumerate(all_heads):
        m_ref[h] = m_own[h]
        acc_ref[h] = jnp.dot(p_own[h], head_v(kv_own[pair][1], u), preferred_element_type=f32)

    mask_b = sel_ref[...].astype(bf16)
    n_heads = MOBA_HEADS
    mask_lane = lambda u: MOBA_HEAD_DIM if u == 0 else 0
    hot_row = _iota((LANES, n_heads * LANES), 0)
    hot_col = _iota((LANES, n_heads * LANES), 1)
    hot_head = hot_col // LANES
    hot_lane_ok = hot_col % LANES == jnp.where(hot_head % 2 == 0, mask_lane(0), mask_lane(1))

    def past(n, carry):
        heads = [(pair, u) for pair in range(n_heads // 2) for u in range(2)]
        kvs = [kv(n, pair) for pair in range(n_heads // 2)]
        one_hot = (hot_lane_ok & (hot_row == hot_head * nb + n)).astype(bf16)
        bias = jnp.dot(mask_b, one_hot, preferred_element_type=f32).astype(bf16)
        q_m = [head_q(pair, u) + bias[:, (2 * pair + u) * LANES:(2 * pair + u + 1) * LANES] for pair, u in heads]
        k_e = [jnp.where(lane == mask_lane(u), jnp.ones_like(kvs[pair][0]), kvs[pair][0]) for pair, u in heads]
        s = [lax.dot_general(q, k, NT_DIMS, preferred_element_type=f32) for q, k in zip(q_m, k_e)]
        m_old = [m_ref[h] for h in range(n_heads)]
        m_new = [jnp.maximum(mo, jnp.broadcast_to(jnp.max(x, axis=1, keepdims=True), (blk, LANES)))
                 for mo, x in zip(m_old, s)]
        p = [jnp.exp2(x - wide(mn)).astype(bf16) for x, mn in zip(s, m_new)]
        pv = [jnp.dot(pp, head_v(kvs[pair][1], u), preferred_element_type=f32) for pp, (pair, u) in zip(p, heads)]
        for h in range(n_heads):
            m_ref[h] = m_new[h]
            acc_ref[h] = jnp.exp2(m_old[h] - m_new[h]) * acc_ref[h] + pv[h]
        return carry

    lax.fori_loop(0, i, past, 0)

    for pair in range(MOBA_HEADS // 2):
        halves = []
        for u in range(2):
            acc = acc_ref[2 * pair + u]
            halves.append(acc / pltpu.roll(acc, MOBA_HEAD_DIM, 1))
        y_ref[:, pair * LANES:(pair + 1) * LANES] = jnp.where(low_half, halves[0], halves[1]).astype(y_ref.dtype)


def moba_prompt(qb, kb, vb, sel, n_batch, seq):
    blk = MOBA_BLOCK
    nb = seq // blk
    row = lambda width: pl.BlockSpec((blk, width), lambda b, i: (b * nb + i, 0))
    full = pl.BlockSpec((seq, W), lambda b, i: (b, 0))
    scr = lambda: pltpu.VMEM((MOBA_HEADS, blk, LANES), f32)
    return pl.pallas_call(
        _moba_prompt_body,
        grid=(n_batch, nb),
        in_specs=[row(W), full, full, row(LANES)],
        out_specs=row(W),
        out_shape=jax.ShapeDtypeStruct((n_batch * seq, W), bf16),
        scratch_shapes=[scr(), scr()],
        compiler_params=_params(("arbitrary", "arbitrary")),
        name="moba_prompt",
    )(qb, kb, vb, sel)


def _moba_sample_seq(q_b, q_heads, k_heads, v_row, k_page, v_page, n_pages):
    nh, dh = MOBA_HEADS, MOBA_HEAD_DIM
    pages_per_block = MOBA_BLOCK // PAGE_SIZE
    nb = n_pages // pages_per_block
    lane = _iota((nh, LANES), 1)

    def head_sum(x):
        return jnp.concatenate([jnp.sum(x[h * dh:(h + 1) * dh], axis=0, keepdims=True) for h in range(nh)], axis=0)

    def head_rows(x):
        return jnp.concatenate([jnp.broadcast_to(x[h:h + 1], (dh, x.shape[1])) for h in range(nh)], axis=0)

    s_pages = [head_sum(k_page(g).reshape(nh * dh, PAGE_SIZE) * q_b) for g in range(n_pages)]
    gate = jnp.zeros((nh, LANES), f32)
    for n in range(nb):
        blk_sum = sum(jnp.sum(s_pages[n * pages_per_block + t], axis=1, keepdims=True)
                      for t in range(pages_per_block))
        gate = jnp.where(lane == n, blk_sum * (1.0 / MOBA_BLOCK), gate)
    sel = _topk_select(gate, nb, nb, 1).astype(f32)
    s_own = jnp.sum(q_heads * k_heads, axis=1, keepdims=True)
    s_pages = [jnp.where(jnp.broadcast_to(sel[:, g // pages_per_block:g // pages_per_block + 1],
                                          (nh, PAGE_SIZE)) > 0.5, s, -jnp.inf)
               for g, s in enumerate(s_pages)]
    m = s_own
    for s in s_pages:
        m = jnp.maximum(m, jnp.max(s, axis=1, keepdims=True))
    p_pages = [jnp.exp(s - m) for s in s_pages]
    w_own = jnp.exp(s_own - m)
    l_tot = w_own + sum(jnp.sum(p, axis=1, keepdims=True) for p in p_pages)

    acc = jnp.zeros((nh * dh, PAGE_SIZE), f32)
    for g in range(n_pages):
        acc = acc + head_rows(p_pages[g]) * v_page(g).reshape(nh * dh, PAGE_SIZE)
    hi = acc.astype(bf16)
    mid = (acc - hi.astype(f32))
    lo = (mid - mid.astype(bf16).astype(f32)).astype(bf16)
    ones = jnp.ones((nh, PAGE_SIZE), bf16)
    nt = lambda a: lax.dot_general(ones, a, NT_DIMS, preferred_element_type=f32)
    out = nt(hi) + nt(mid.astype(bf16)) + nt(lo)
    out = (out + w_own * v_row) / l_tot
    head_lane = _iota((nh, nh * dh), 1) // dh == _iota((nh, nh * dh), 0)
    return jnp.sum(jnp.where(head_lane, out, 0.0), axis=0, keepdims=True)


def _rope_tables(pos):
    half = MOBA_HEAD_DIM // 2
    inv = ROPE_THETA ** (-jnp.arange(half, dtype=f32) / half)
    ang = pos.astype(f32)[:, None] * inv[None, :]
    cos, sin = jnp.cos(ang), jnp.sin(ang)
    cos_h = jnp.concatenate([cos, cos], -1)
    sin_h = jnp.concatenate([-sin, sin], -1)
    return jnp.tile(cos_h, (1, MOBA_HEADS)), jnp.tile(sin_h, (1, MOBA_HEADS))


def _pack_w_in(w):
    cols = lambda o, n: w[:, o:o + n]
    hk = GLA_HEADS * GLA_DK
    parts = [cols(_O_GQ, hk), cols(_O_GK, hk), cols(_O_GV, W), cols(_O_GO, W),
             cols(_O_MQ, W), cols(_O_MK, W), cols(_O_MV, W), cols(_O_PG, N_BRANCH * D_MODEL),
             cols(_O_PR, RWKV_PROJ), cols(_O_GLOW, GLA_GATE_RANK),
             jnp.zeros((w.shape[0], N_PACK - C_GLOW - GLA_GATE_RANK), w.dtype)]
    return jnp.concatenate(parts, axis=1).astype(bf16)


def _pad_rows(m, rows):
    return jnp.concatenate([m, jnp.zeros((rows - m.shape[0], m.shape[1]), m.dtype)], axis=0)


SUBLANES = 8
IN_PROJ_ROWS_TARGET, IN_PROJ_TN = 2048, 1024
TOKEN_ROWS_TARGET = 1024
MERGE_ROWS_TARGET = 512
MIXER_ROWS = 256


def _row_tile(m, target):
    best = SUBLANES
    for t in range(SUBLANES, target + 1, SUBLANES):
        if m % t == 0:
            best = t
    assert m % best == 0
    return best


def kernel(x_prompt, x_sample, cache_k, cache_v, page_table, state_gla, state_wkv, state_shift, w_in, b_gate, gla_gk_up, gla_gk_bias, gla_norm_w, rwkv_mu, rwkv_w0, rwkv_w_up, rwkv_a0, rwkv_a_up, rwkv_g_up, rwkv_k_k, rwkv_k_a, rwkv_r_k, rwkv_ln_w, rwkv_ln_b, w_branch, w_out, ln1_g, ln1_b, w_up, w_down, ln2_g, ln2_b):
    n_batch, seq, _ = x_prompt.shape
    n_seq = x_sample.shape[0]
    n_pages = page_table.shape[1]
    mp = n_batch * seq
    past = n_pages * PAGE_SIZE

    cos_p, sin_p = _rope_tables(jnp.arange(seq, dtype=jnp.int32))
    cos_s, sin_s = _rope_tables(jnp.full((1,), past, jnp.int32))
    pt_flat = page_table.reshape(-1)
    cache_kt = jnp.transpose(cache_k, (0, 1, 3, 4, 2))
    cache_vt = jnp.transpose(cache_v, (0, 1, 3, 4, 2))
    state_wkv_t = jnp.transpose(state_wkv, (0, 2, 3, 4, 1))

    x_p, x_s = x_prompt.reshape(mp, D_MODEL), x_sample.reshape(n_seq, D_MODEL)
    outs = {k: [] for k in ("k_p", "v_p", "k_s", "v_s", "gla_p", "gla_s", "wkv_p", "wkv_s", "shift_p", "shift_s")}
    for l in range(DEPTH):
        row2 = lambda v: v.reshape(1, -1)
        mu = rwkv_mu[l]
        rp = dict(
            mu_r=row2(mu[:W]), mu_k=row2(mu[W:2 * W]), mu_v=row2(mu[2 * W:3 * W]),
            mu_wa=row2(mu[3 * W:3 * W + LANES]), mu_gr=row2(mu[3 * W + LANES:]),
            w0=row2(rwkv_w0[l]), w_up=_pad_rows(rwkv_w_up[l], LANES).astype(bf16),
            a0=row2(rwkv_a0[l]),
            a_up=jnp.concatenate([jnp.zeros((RWKV_DECAY_RANK, W), f32), rwkv_a_up[l]], axis=0).astype(bf16),
            g_up=rwkv_g_up[l].astype(bf16), k_k=row2(rwkv_k_k[l]), k_a=row2(rwkv_k_a[l]),
            r_k=row2(rwkv_r_k[l]), ln_w=row2(rwkv_ln_w[l]), ln_b=row2(rwkv_ln_b[l]))
        up_pad = _pad_rows(gla_gk_up[l], LANES).astype(bf16)
        gk_bias = row2(gla_gk_bias[l])
        norm_w = row2(gla_norm_w[l])

        w_packed = _pack_w_in(w_in[l])
        wb, wo = w_branch[l].astype(bf16), w_out[l].astype(bf16)
        wu, wd = w_up[l].astype(bf16), w_down[l].astype(bf16)
        ln1, ln2 = (row2(ln1_g[l]), row2(ln1_b[l])), (row2(ln2_g[l]), row2(ln2_b[l]))

        def token_tail(x_rows, proj_rows, y_gla, y_moba, y_rwkv):
            tm = _row_tile(x_rows.shape[0], TOKEN_ROWS_TARGET)
            x1 = merge_ln(x_rows, proj_rows, y_gla, y_moba, y_rwkv, wb, b_gate[l], wo, *ln1,
                          _row_tile(x_rows.shape[0], MERGE_ROWS_TARGET))
            return mlp_ln(x1, wu, wd, *ln2, tm)

        proj_p = in_proj(x_p, w_packed, _row_tile(mp, IN_PROJ_ROWS_TARGET), IN_PROJ_TN)
        proj_s = in_proj(x_s, w_packed, n_seq, IN_PROJ_TN)

        y_gla_s, y_rwkv_s, q_s, k_s, gla_s, wkv_s_t = sample_step(
            proj_s, 0, n_seq, l, state_gla, state_wkv_t, state_shift, cos_s, sin_s, up_pad, gk_bias, norm_w, rp)
        v_s = proj_s[:, C_MV:C_MV + W]
        heads = (n_seq, MOBA_HEADS, MOBA_HEAD_DIM)
        q_lanes = jnp.broadcast_to(q_s[:, :, None], (n_seq, W, PAGE_SIZE))

        y_gla_p, gla_p = gla_prompt(proj_p, up_pad, gk_bias, norm_w, n_batch, seq, MIXER_ROWS)
        y_rwkv_p, wkv_p, shift_p, y_moba_s = rwkv_prompt_moba_sample(
            proj_p, rp, n_batch, seq, MIXER_ROWS, l, pt_flat, q_lanes, q_s.reshape(heads), k_s.reshape(heads),
            v_s.reshape(n_seq, 1, W), cache_kt, cache_vt, n_pages)
        k_rope, qb, kb, vb, sel = rope_prep(proj_p, cos_p, sin_p, n_batch, seq)
        y_moba_p = moba_prompt(qb, kb, vb, sel, n_batch, seq)
        x_p_next = token_tail(x_p, proj_p, y_gla_p, y_moba_p, y_rwkv_p)
        x_s_next = token_tail(x_s, proj_s, y_gla_s, y_moba_s.reshape(n_seq, W).astype(bf16), y_rwkv_s)

        x_p, x_s = x_p_next, x_s_next
        hd = (MOBA_HEADS, MOBA_HEAD_DIM)
        outs["k_p"].append(k_rope.reshape(n_batch, seq, *hd))
        outs["v_p"].append(proj_p[:, C_MV:C_MV + W].reshape(n_batch, seq, *hd))
        outs["k_s"].append(k_s.reshape(n_seq, 1, *hd))
        outs["v_s"].append(v_s.reshape(n_seq, 1, *hd))
        outs["gla_p"].append(gla_p)
        outs["gla_s"].append(gla_s)
        wkv_p = wkv_p.reshape(n_batch, RWKV_HEADS // 2, RWKV_HEAD, 2, RWKV_HEAD)
        outs["wkv_p"].append(wkv_p.transpose(0, 1, 3, 2, 4).reshape(n_batch, RWKV_HEADS, RWKV_HEAD, RWKV_HEAD))
        outs["wkv_s"].append(jnp.transpose(wkv_s_t, (3, 0, 1, 2)))
        outs["shift_p"].append(shift_p[:, 0, :])
        outs["shift_s"].append(proj_s[:, C_PR:C_PR + RWKV_PROJ])

    st = lambda k: jnp.stack(outs[k])
    return (x_p.reshape(n_batch, seq, D_MODEL), x_s.reshape(n_seq, 1, D_MODEL),
            st("k_p"), st("v_p"), st("k_s"), st("v_s"), st("gla_p"), st("gla_s"),
            st("wkv_p"), st("wkv_s"), st("shift_p"), st("shift_s"))
```

```python
import functools

import jax
import jax.numpy as jnp
from jax import lax
from jax.experimental import pallas as pl
from jax.experimental.pallas import tpu as pltpu

f32, bf16 = jnp.float32, jnp.bfloat16

D_MODEL = 1024
DEPTH = 2
PAGE_SIZE = 128
BRANCH_WIDTH = D_MODEL // 2
N_BRANCH = 3
GLA_HEADS = 4
GLA_DV = BRANCH_WIDTH // GLA_HEADS
GLA_DK = GLA_DV // 2
GLA_GATE_RANK = 16
GLA_GATE_NORMALIZER = 16.0
GLA_CHUNK = 64
GLA_NORM_EPS = 1e-5
MOBA_HEAD_DIM = 64
MOBA_HEADS = BRANCH_WIDTH // MOBA_HEAD_DIM
MOBA_BLOCK = 256
MOBA_TOPK = 3
ROPE_THETA = 10000.0
RWKV_HEAD = 64
RWKV_HEADS = BRANCH_WIDTH // RWKV_HEAD
RWKV_DECAY_RANK = 64
RWKV_A_RANK = 64
RWKV_GATE_RANK = 128
RWKV_DECAY_SCALE = 0.606531
RWKV_GN_EPS = 64e-5
RWKV_PROJ = 3 * BRANCH_WIDTH + RWKV_DECAY_RANK + RWKV_A_RANK + RWKV_GATE_RANK
RWKV_CHUNK = 64
RWKV_INV_BLOCK = 16
D_FF = 4 * D_MODEL
ALPHA = (2 * DEPTH) ** 0.25
LN_EPS = 1e-5
LOG2_E = 1.4426950408889634
MASK_BIG = 2.0 ** 100

LANES = 128
VMEM_LIMIT_BYTES = 56 * 1024 * 1024

W = BRANCH_WIDTH
C_GQ, C_GK, C_GV, C_GO = 0, 256, 512, 1024
C_MQ, C_MK, C_MV = 1536, 2048, 2560
C_PG = 3072
C_PR = 6144
C_R, C_K7, C_V7 = C_PR, C_PR + W, C_PR + 2 * W
C_WA = C_PR + 3 * W
C_GR = C_WA + LANES
C_GLOW = C_PR + RWKV_PROJ
N_PACK = 8192

_O_GQ, _O_GK, _O_GV, _O_GLOW, _O_GO = 0, 256, 512, 1024, 1040
_O_MQ, _O_MK, _O_MV, _O_PR, _O_PG = 1552, 2064, 2576, 3088, 4880

NT_DIMS = (((1,), (1,)), ((), ()))
TN_DIMS = (((0,), (0,)), ((), ()))


def _params(sem):
    return pltpu.CompilerParams(dimension_semantics=sem, vmem_limit_bytes=VMEM_LIMIT_BYTES)


def _bdot(a, b):
    return jnp.dot(a.astype(bf16), b.astype(bf16), preferred_element_type=f32)


def _bdot_nt(a, b):
    return lax.dot_general(a.astype(bf16), b.astype(bf16), NT_DIMS, preferred_element_type=f32)


def _bdot_tn(a, b):
    return lax.dot_general(a.astype(bf16), b.astype(bf16), TN_DIMS, preferred_element_type=f32)


def _split2(x):
    hi = x.astype(bf16)
    lo = (x - hi.astype(f32)).astype(bf16)
    return hi, lo


def _dot_exact_rhs(x, m_bf16):
    hi, lo = _split2(x)
    return (jnp.dot(hi, m_bf16, preferred_element_type=f32)
            + jnp.dot(lo, m_bf16, preferred_element_type=f32))


def _dot_exact_lhs(m_bf16, x):
    hi, lo = _split2(x)
    return (jnp.dot(m_bf16, hi, preferred_element_type=f32)
            + jnp.dot(m_bf16, lo, preferred_element_type=f32))


def _dot3_nt(a, b):
    ah, al = _split2(a)
    bh, bl = _split2(b)
    d = lambda u, v: lax.dot_general(u, v, NT_DIMS, preferred_element_type=f32)
    return d(ah, bh) + d(al, bh) + d(ah, bl)


def _iota(shape, dim):
    return lax.broadcasted_iota(jnp.int32, shape, dim)


def _seg_matrix(n, seg):
    return (_iota((n, n), 0) // seg == _iota((n, n), 1) // seg).astype(bf16)


def _layer_norm(z, g, b):
    mu = jnp.mean(z, -1, keepdims=True)
    d = z - mu
    var = jnp.mean(d * d, -1, keepdims=True)
    return d * lax.rsqrt(var + LN_EPS) * g + b


def _row_to_col(row, n):
    eye = _iota((n, n), 0) == _iota((n, n), 1)
    return jnp.sum(jnp.where(eye, jnp.broadcast_to(row, (n, n)), 0.0), axis=1, keepdims=True)


def _set_row(ref, idx, row):
    cur = ref[...]
    ref[...] = jnp.where(_iota(cur.shape, 0) == idx, jnp.broadcast_to(row, cur.shape), cur)


def _col_to_row(col, n):
    eye = _iota((n, n), 0) == _iota((n, n), 1)
    return jnp.sum(jnp.where(eye, jnp.broadcast_to(col, (n, n)), 0.0), axis=0, keepdims=True)


def _inproj_body(x_ref, w_ref, o_ref, xb_ref):
    @pl.when(pl.program_id(1) == 0)
    def _():
        xb_ref[...] = x_ref[...].astype(bf16)

    rows = x_ref.shape[0]
    for r in range(0, rows, IN_PROJ_ROW_CHUNK):
        rc = min(IN_PROJ_ROW_CHUNK, rows - r)
        o_ref[r:r + rc, :] = jnp.dot(xb_ref[r:r + rc, :], w_ref[...], preferred_element_type=f32)


def in_proj(x, w, tm, tn):
    m, k = x.shape
    n = w.shape[1]
    return pl.pallas_call(
        _inproj_body,
        grid=(m // tm, n // tn),
        in_specs=[pl.BlockSpec((tm, k), lambda i, j: (i, 0)),
                  pl.BlockSpec((k, tn), lambda i, j: (0, j))],
        out_specs=pl.BlockSpec((tm, tn), lambda i, j: (i, j)),
        out_shape=jax.ShapeDtypeStruct((m, n), f32),
        scratch_shapes=[pltpu.VMEM((tm, k), bf16)],
        compiler_params=_params(("arbitrary", "arbitrary")),
        name="in_proj",
    )(x, w)


def _merge_body(x_ref, pg0_ref, pg1_ref, pg2_ref, y0_ref, y1_ref, y2_ref, wb_ref, bg_ref, wo_ref,
                g_ref, b_ref, o_ref):
    merged = None
    for n, (pg_ref, y_ref) in enumerate(((pg0_ref, y0_ref), (pg1_ref, y1_ref), (pg2_ref, y2_ref))):
        gate = jax.nn.sigmoid(pg_ref[...] + bg_ref[n:n + 1, :])
        term = gate * jnp.dot(y_ref[...], wb_ref[n], preferred_element_type=f32)
        merged = term if merged is None else merged + term
    out = jnp.dot(merged.astype(bf16), wo_ref[...], preferred_element_type=f32)
    o_ref[...] = _layer_norm(ALPHA * x_ref[...] + out, g_ref[...], b_ref[...])


def merge_ln(x, proj, y_gla, y_moba, y_rwkv, w_branch, b_gate, w_out, g, b, tm):
    m = x.shape[0]
    row = lambda i: (i, 0)
    const2 = lambda i: (0, 0)
    pg_spec = lambda n: pl.BlockSpec((tm, D_MODEL), lambda i, n=n: (i, C_PG // D_MODEL + n))
    y_spec = pl.BlockSpec((tm, W), row)
    return pl.pallas_call(
        _merge_body,
        grid=(m // tm,),
        in_specs=[pl.BlockSpec((tm, D_MODEL), row), pg_spec(0), pg_spec(1), pg_spec(2),
                  y_spec, y_spec, y_spec,
                  pl.BlockSpec((N_BRANCH, W, D_MODEL), lambda i: (0, 0, 0)),
                  pl.BlockSpec((N_BRANCH, D_MODEL), const2),
                  pl.BlockSpec((D_MODEL, D_MODEL), const2),
                  pl.BlockSpec((1, D_MODEL), const2), pl.BlockSpec((1, D_MODEL), const2)],
        out_specs=pl.BlockSpec((tm, D_MODEL), row),
        out_shape=jax.ShapeDtypeStruct((m, D_MODEL), f32),
        compiler_params=_params(("arbitrary",)),
        name="merge_ln",
    )(x, proj, proj, proj, y_gla, y_moba, y_rwkv, w_branch, b_gate, w_out, g, b)


FF_CHUNK = 1024


def _mlp_body(x_ref, wu_ref, wd_ref, g_ref, b_ref, o_ref):
    x = x_ref[...]
    xb = x.astype(bf16)
    acc = None
    for c in range(D_FF // FF_CHUNK):
        h = jnp.dot(xb, wu_ref[:, c * FF_CHUNK:(c + 1) * FF_CHUNK], preferred_element_type=f32)
        h = jnp.square(jnp.maximum(h, 0.0)).astype(bf16)
        part = jnp.dot(h, wd_ref[c * FF_CHUNK:(c + 1) * FF_CHUNK, :], preferred_element_type=f32)
        acc = part if acc is None else acc + part
    o_ref[...] = _layer_norm(ALPHA * x + acc, g_ref[...], b_ref[...])


def mlp_ln(x, w_up, w_down, g, b, tm):
    m = x.shape[0]
    row = lambda i: (i, 0)
    const2 = lambda i: (0, 0)
    return pl.pallas_call(
        _mlp_body,
        grid=(m // tm,),
        in_specs=[pl.BlockSpec((tm, D_MODEL), row),
                  pl.BlockSpec((D_MODEL, D_FF), const2, pipeline_mode=pl.Buffered(1)),
                  pl.BlockSpec((D_FF, D_MODEL), const2, pipeline_mode=pl.Buffered(1)),
                  pl.BlockSpec((1, D_MODEL), const2), pl.BlockSpec((1, D_MODEL), const2)],
        out_specs=pl.BlockSpec((tm, D_MODEL), row),
        out_shape=jax.ShapeDtypeStruct((m, D_MODEL), f32),
        compiler_params=_params(("arbitrary",)),
        name="mlp_ln",
    )(x, w_up, w_down, g, b)


def _gla_log_gate(glow, up_ref, bias_ref):
    z = _bdot(glow, up_ref[...]) + bias_ref[...]
    return jax.nn.log_sigmoid(z) * (1.0 / GLA_GATE_NORMALIZER)


def _gla_out(o, norm_w, g_out):
    o = o * lax.rsqrt(jnp.mean(jnp.square(o), -1, keepdims=True) + GLA_NORM_EPS) * norm_w
    return o * jax.nn.silu(g_out)


def _gla_prompt_body(q_ref, k_ref, v_ref, go_ref, glow_ref, up_ref, bias_ref, nw_ref,
                     y_ref, s_out_ref, s_ref):
    t = pl.program_id(1)
    rows = q_ref.shape[0]
    c_sz = GLA_CHUNK

    @pl.when(t == 0)
    def _():
        s_ref[...] = jnp.zeros_like(s_ref)

    lg = _gla_log_gate(glow_ref[...], up_ref, bias_ref)
    tril = (_iota((c_sz, c_sz), 1) <= _iota((c_sz, c_sz), 0))
    tril_b = tril.astype(bf16)
    n_chunks = rows // c_sz
    probs = []
    for c in range(n_chunks):
        rs = slice(c * c_sz, (c + 1) * c_sz)
        g_cum = _dot_exact_lhs(tril_b, lg[rs])
        g_last = g_cum[c_sz - 1:c_sz, :]
        q_dec = q_ref[rs, :] * (GLA_DK ** -0.5) * jnp.exp(g_cum)
        k = k_ref[rs, :]
        k_inv = k * jnp.exp(-g_cum)
        k_tail = k * jnp.exp(g_last - g_cum)
        e_last = jnp.exp(g_last)
        for h in range(GLA_HEADS):
            ks = slice(h * GLA_DK, (h + 1) * GLA_DK)
            vs = slice(h * GLA_DV, (h + 1) * GLA_DV)
            probs.append((q_dec[:, ks], k_inv[:, ks], k_tail[:, ks], e_last[:, ks], v_ref[rs, vs], rs, vs))
    a = [jnp.where(tril, _bdot_nt(q[0], q[1]), 0.0) for q in probs]
    o_intra = [_bdot(x, q[4]) for x, q in zip(a, probs)]
    kv = [_bdot_tn(q[2], q[4]) for q in probs]
    e_col = [_row_to_col(q[3], GLA_DK) for q in probs]

    state = [s_ref[h] for h in range(GLA_HEADS)]
    for c in range(n_chunks):
        sl = slice(c * GLA_HEADS, (c + 1) * GLA_HEADS)
        o = [oi + _bdot(q[0], s) for oi, q, s in zip(o_intra[sl], probs[sl], state)]
        state = [s * e + x for s, e, x in zip(state, e_col[sl], kv[sl])]
        for oo, q in zip(o, probs[sl]):
            rs, vs = q[5], q[6]
            y_ref[rs, vs] = _gla_out(oo, nw_ref[...], go_ref[rs, vs]).astype(y_ref.dtype)
    for h in range(GLA_HEADS):
        s_ref[h] = state[h]
    s_out_ref[...] = s_ref[...]


def gla_prompt(proj, up_pad, bias, norm_w, n_batch, seq, rt):
    nt = seq // rt
    rowmap = lambda width, col: pl.BlockSpec((rt, width), lambda b, t: (b * nt + t, col // width))
    const2 = lambda b, t: (0, 0)
    return pl.pallas_call(
        _gla_prompt_body,
        grid=(n_batch, nt),
        in_specs=[rowmap(GLA_HEADS * GLA_DK, C_GQ), rowmap(GLA_HEADS * GLA_DK, C_GK), rowmap(W, C_GV),
                  rowmap(W, C_GO), rowmap(LANES, C_GLOW),
                  pl.BlockSpec((LANES, GLA_HEADS * GLA_DK), const2),
                  pl.BlockSpec((1, GLA_HEADS * GLA_DK), const2),
                  pl.BlockSpec((1, GLA_DV), const2)],
        out_specs=[pl.BlockSpec((rt, W), lambda b, t: (b * nt + t, 0)),
                   pl.BlockSpec((None, GLA_HEADS, GLA_DK, GLA_DV), lambda b, t: (b, 0, 0, 0))],
        out_shape=[jax.ShapeDtypeStruct((n_batch * seq, W), bf16),
                   jax.ShapeDtypeStruct((n_batch, GLA_HEADS, GLA_DK, GLA_DV), f32)],
        scratch_shapes=[pltpu.VMEM((GLA_HEADS, GLA_DK, GLA_DV), f32)],
        compiler_params=_params(("arbitrary", "arbitrary")),
        name="gla_prompt",
    )(proj, proj, proj, proj, proj, up_pad, bias, norm_w)


def _rwkv_pre(r_s, k_s, v_s, wa_s, gr_s, p):
    seg = _seg_matrix(W, RWKV_HEAD)
    logw = -RWKV_DECAY_SCALE * jax.nn.sigmoid(p["w0"][...] + _bdot(jnp.tanh(wa_s), p["w_up"][...]))
    a = jax.nn.sigmoid(p["a0"][...] + _bdot(wa_s, p["a_up"][...]))
    g7 = _bdot(jax.nn.sigmoid(gr_s), p["g_up"][...])
    kk = k_s * p["k_k"][...]
    kk = kk * lax.rsqrt(_dot_exact_rhs(jnp.square(kk), seg) + 1e-12)
    k7 = k_s * (1.0 + (a - 1.0) * p["k_a"][...])
    bonus = _dot_exact_rhs(r_s * k7 * p["r_k"][...], seg) * v_s
    return dict(r=r_s, logw=logw, k=k7, v=v_s, kk=kk, a=a, g7=g7, bonus=bonus)


def _rwkv_post(yw, bonus, g7, p):
    seg = _seg_matrix(W, RWKV_HEAD)
    inv_n = 1.0 / RWKV_HEAD
    mu = _dot_exact_rhs(yw, seg) * inv_n
    d = yw - mu
    var = _dot_exact_rhs(jnp.square(d), seg) * inv_n
    yn = d * lax.rsqrt(var + RWKV_GN_EPS) * p["ln_w"][...] + p["ln_b"][...]
    return (yn + bonus) * g7


def _pair_bd(x):
    low = _iota(x.shape, 1) < x.shape[1] // 2
    zero = jnp.zeros_like(x)
    return jnp.concatenate([jnp.where(low, x, zero), jnp.where(low, zero, x)], axis=0)


def _pair_mm(a, b):
    return jnp.dot(a.astype(bf16), _pair_bd(b.astype(bf16)), preferred_element_type=f32)


def _pair_nt(a, bd_b):
    return lax.dot_general(a.astype(bf16), bd_b, NT_DIMS, preferred_element_type=f32)


def _pair_tn(a, b):
    full = lax.dot_general(a.astype(bf16), b.astype(bf16), TN_DIMS, preferred_element_type=f32)
    p = b.shape[1] // 2
    low = _iota((p, b.shape[1]), 1) < p
    groups = full.shape[0] // (2 * p)
    return jnp.concatenate([jnp.where(low, full[2 * g * p:(2 * g + 1) * p], full[(2 * g + 1) * p:(2 * g + 2) * p])
                            for g in range(groups)], axis=0)


def _inv_unit_lower(n_mats):
    c = n_mats[0].shape[0]
    row, col = _iota((c, 2 * c), 0), _iota((c, 2 * c), 1) % c
    eye = (row == col).astype(f32)
    blk = (row // RWKV_INV_BLOCK) == (col // RWKV_INV_BLOCK)
    d = [jnp.where(blk, n, 0.0) for n in n_mats]
    lo = [n - x for n, x in zip(n_mats, d)]
    d2 = [_pair_mm(x, x) for x in d]
    d4 = [_pair_mm(x, x) for x in d2]
    d8 = [_pair_mm(x, x) for x in d4]
    pa = [_pair_mm(eye - x, eye + y) for x, y in zip(d, d2)]
    pb = [_pair_mm(eye + x, eye + y) for x, y in zip(d4, d8)]
    d_inv = [_pair_mm(x, y) for x, y in zip(pa, pb)]
    e = [_pair_mm(x, y) for x, y in zip(d_inv, lo)]
    e2 = [_pair_mm(x, x) for x in e]
    f = [_pair_mm(eye - x, eye + y) for x, y in zip(e, e2)]
    return [_pair_mm(x, y) for x, y in zip(f, d_inv)]


def _rwkv_chunk_coeffs(probs):
    c, width = probs[0][0].shape
    row, col = _iota((c, width), 0), _iota((c, width), 1) % (width // 2)
    strict, incl, eye = col < row, col <= row, col == row
    lhs = [jnp.concatenate([q[1], q[0]], 0).astype(bf16) for q in probs]
    with_b = [_pair_nt(x, _pair_bd(q[2].astype(bf16))) for x, q in zip(lhs, probs)]
    with_k = [_pair_nt(x, _pair_bd(q[3].astype(bf16))) for x, q in zip(lhs, probs)]
    n_ab = [jnp.where(strict, x[:c], 0.0) for x in with_b]
    p_rb = [jnp.where(incl, x[c:], 0.0) for x in with_b]
    a_ak = [jnp.where(strict, x[:c], 0.0) for x in with_k]
    p_rk = [jnp.where(incl, x[c:], 0.0) for x in with_k]
    t_inv = _inv_unit_lower(n_ab)
    av = [_pair_mm(x, q[6]) for x, q in zip(a_ak, probs)]
    two = lambda u, w: jnp.concatenate([_pair_bd(u.astype(bf16)), _pair_bd(w.astype(bf16))], axis=1)
    x = [jnp.dot(ti.astype(bf16), two(q[1], a), preferred_element_type=f32)
         for ti, q, a in zip(t_inv, probs, av)]
    px = [jnp.dot(pr.astype(bf16), two(xx[:, :width], xx[:, width:]), preferred_element_type=f32)
          for pr, xx in zip(p_rb, x)]
    prkv = [_pair_mm(pr, q[6]) for pr, q in zip(p_rk, probs)]
    xb = [_pair_tn(xx, q[4]) for xx, q in zip(x, probs)]
    vk = [_pair_tn(q[6], q[5]) for q in probs]
    out = []
    for q, pxx, pv, xbb, vkk in zip(probs, px, prkv, xb, vk):
        tm_t = jnp.where(eye, jnp.broadcast_to(q[7], (c, width)), 0.0) - xbb[:c]
        out.append((q[0] - pxx[:, :width], pv - pxx[:, width:], tm_t, vkk - xbb[c:]))
    return out


_RWKV_PARAM_NAMES = ("mu_r", "mu_k", "mu_v", "mu_wa", "mu_gr", "w0", "w_up", "a0", "a_up", "g_up",
                     "k_k", "k_a", "r_k", "ln_w", "ln_b")


def _rwkv_prompt_body(layer, n_pages, seqs_per_step, pt_ref, r_ref, k_ref, v_ref, wa_ref, gr_ref, *rest):
    np_ = len(_RWKV_PARAM_NAMES)
    p = dict(zip(_RWKV_PARAM_NAMES, rest[:np_]))
    qb_ref, qh_ref, kh_ref, vrow_ref, cache_k_hbm, cache_v_hbm = rest[np_:np_ + 6]
    y_ref, s_out_ref, shift_out_ref, y_moba_ref = rest[np_ + 6:np_ + 10]
    s_ref, prev_ref, k_buf, v_buf, sem = rest[np_ + 10:]
    t = pl.program_id(1)
    step = pl.program_id(0) * pl.num_programs(1) + t
    n_steps = pl.num_programs(0) * pl.num_programs(1)
    pages_per_step = seqs_per_step * n_pages
    rows = r_ref.shape[0]
    c_sz = RWKV_CHUNK

    def page_copies(of_step, slot):
        out = []
        for j in range(pages_per_step):
            page = pt_ref[of_step * pages_per_step + j]
            out.append(pltpu.make_async_copy(cache_k_hbm.at[layer, page], k_buf.at[slot, j], sem.at[0, slot, j]))
            out.append(pltpu.make_async_copy(cache_v_hbm.at[layer, page], v_buf.at[slot, j], sem.at[1, slot, j]))
        return out

    @pl.when(step == 0)
    def _():
        for c in page_copies(0, 0):
            c.start()

    slot = step % 2
    for c in page_copies(step, slot):
        c.wait()

    @pl.when(step + 1 < n_steps)
    def _():
        for c in page_copies(step + 1, (step + 1) % 2):
            c.start()

    @pl.when(t == 0)
    def _():
        s_ref[...] = jnp.zeros_like(s_ref)
        prev_ref[...] = jnp.zeros_like(prev_ref)

    def shifted(x_ref, mu_ref, c0):
        x = x_ref[...]
        width = x.shape[1]
        prev = jnp.where(_iota(x.shape, 0) == 0, prev_ref[0:1, c0:c0 + width], pltpu.roll(x, 1, 0))
        prev_ref[0:1, c0:c0 + width] = x[rows - 1:rows, :]
        return x + mu_ref[...] * (prev - x)

    pre = _rwkv_pre(shifted(r_ref, p["mu_r"], 0), shifted(k_ref, p["mu_k"], W), shifted(v_ref, p["mu_v"], 2 * W),
                    shifted(wa_ref, p["mu_wa"], 3 * W), shifted(gr_ref, p["mu_gr"], 3 * W + LANES), p)
    tril_b = (_iota((c_sz, c_sz), 1) <= _iota((c_sz, c_sz), 0)).astype(bf16)
    n_chunks = rows // c_sz
    n_pairs = RWKV_HEADS // 2
    b_all = pre["kk"] * pre["a"]

    probs = []
    for c in range(n_chunks):
        rs = slice(c * c_sz, (c + 1) * c_sz)
        lw = pre["logw"][rs]
        g_cum = _dot_exact_lhs(tril_b, lw)
        g_last = g_cum[c_sz - 1:c_sz, :]
        e_ng = jnp.exp(-g_cum)
        e_gl = jnp.exp(g_last - g_cum)
        kk, b, k, v = pre["kk"][rs], b_all[rs], pre["k"][rs], pre["v"][rs]
        r_t = pre["r"][rs] * jnp.exp(g_cum)
        a_t = kk * jnp.exp(g_cum - lw)
        b_t, k_t, b_g, k_g = b * e_ng, k * e_ng, b * e_gl, k * e_gl
        gam = jnp.exp(g_last)
        for hp in range(n_pairs):
            hs = slice(hp * LANES, (hp + 1) * LANES)
            probs.append((r_t[:, hs], a_t[:, hs], b_t[:, hs], k_t[:, hs], b_g[:, hs], k_g[:, hs], v[:, hs],
                          gam[:, hs]))
    coeffs = _rwkv_chunk_coeffs(probs)

    state = [s_ref[hp] for hp in range(n_pairs)]
    y_rows = []
    for c in range(n_chunks):
        cf = coeffs[c * n_pairs:(c + 1) * n_pairs]
        bd_s = [_pair_bd(s.astype(bf16)) for s in state]
        y_rows.append(jnp.concatenate([_pair_nt(q[0], bd) + q[1] for q, bd in zip(cf, bd_s)], axis=1))
        state = [_pair_mm(s, q[2]) + q[3] for q, s in zip(cf, state)]
    for hp in range(n_pairs):
        s_ref[hp] = state[hp]
    yw = jnp.concatenate(y_rows, axis=0)
    y_ref[...] = _rwkv_post(yw, pre["bonus"], pre["g7"], p).astype(y_ref.dtype)
    s_out_ref[...] = s_ref[...]
    shift_out_ref[...] = prev_ref[...]

    for u in range(seqs_per_step):
        y_moba_ref[u] = _moba_sample_seq(
            qb_ref[u], qh_ref[u], kh_ref[u], vrow_ref[u],
            lambda g, u=u: k_buf[slot, u * n_pages + g], lambda g, u=u: v_buf[slot, u * n_pages + g], n_pages)


def _rwkv_param_specs(const_map):
    vec = lambda width: pl.BlockSpec((1, width), const_map)
    mat = lambda r: pl.BlockSpec((r, W), const_map)
    return [vec(W), vec(W), vec(W), vec(LANES), vec(LANES), vec(W), mat(LANES), vec(W), mat(LANES),
            mat(RWKV_GATE_RANK), vec(W), vec(W), vec(W), vec(W), vec(W)]


def rwkv_prompt_moba_sample(proj, params, n_batch, seq, rt, layer, page_table_flat, q_lanes, q_heads, k_heads,
                            v_row, cache_kt, cache_vt, n_pages):
    nt = seq // rt
    n_seq = q_heads.shape[0]
    n_steps = n_batch * nt
    assert n_seq % n_steps == 0, "the sample sequences are spread evenly over the prompt row tiles"
    sps = n_seq // n_steps
    nh, dh = MOBA_HEADS, MOBA_HEAD_DIM
    rowmap = lambda width, col: pl.BlockSpec((rt, width), lambda b, t, pt: (b * nt + t, col // width))
    const2 = lambda b, t, pt: (0, 0)
    per_step = lambda *tail: pl.BlockSpec((sps,) + tail, lambda b, t, pt: (b * nt + t,) + (0,) * len(tail))
    grid_spec = pltpu.PrefetchScalarGridSpec(
        num_scalar_prefetch=1,
        grid=(n_batch, nt),
        in_specs=[rowmap(W, C_R), rowmap(W, C_K7), rowmap(W, C_V7), rowmap(LANES, C_WA), rowmap(LANES, C_GR)]
        + _rwkv_param_specs(const2)
        + [per_step(nh * dh, PAGE_SIZE), per_step(nh, dh), per_step(nh, dh), per_step(1, W),
           pl.BlockSpec(memory_space=pl.ANY), pl.BlockSpec(memory_space=pl.ANY)],
        out_specs=[pl.BlockSpec((rt, W), lambda b, t, pt: (b * nt + t, 0)),
                   pl.BlockSpec((None, RWKV_HEADS // 2, RWKV_HEAD, 2 * RWKV_HEAD), lambda b, t, pt: (b, 0, 0, 0)),
                   pl.BlockSpec((None, SUBLANES, RWKV_PROJ), lambda b, t, pt: (b, 0, 0)),
                   per_step(1, W)],
        scratch_shapes=[pltpu.VMEM((RWKV_HEADS // 2, RWKV_HEAD, 2 * RWKV_HEAD), f32),
                        pltpu.VMEM((SUBLANES, RWKV_PROJ), f32),
                        pltpu.VMEM((2, sps * n_pages, nh, dh, PAGE_SIZE), f32),
                        pltpu.VMEM((2, sps * n_pages, nh, dh, PAGE_SIZE), f32),
                        pltpu.SemaphoreType.DMA((2, 2, sps * n_pages))],
    )
    return pl.pallas_call(
        functools.partial(_rwkv_prompt_body, layer, n_pages, sps),
        grid_spec=grid_spec,
        out_shape=[jax.ShapeDtypeStruct((n_batch * seq, W), bf16),
                   jax.ShapeDtypeStruct((n_batch, RWKV_HEADS // 2, RWKV_HEAD, 2 * RWKV_HEAD), f32),
                   jax.ShapeDtypeStruct((n_batch, SUBLANES, RWKV_PROJ), f32),
                   jax.ShapeDtypeStruct((n_seq, 1, W), f32)],
        compiler_params=_params(("arbitrary", "arbitrary")),
        name="rwkv_prompt_moba_sample",
    )(page_table_flat, proj, proj, proj, proj, proj, *[params[n] for n in _RWKV_PARAM_NAMES],
      q_lanes, q_heads, k_heads, v_row, cache_kt, cache_vt)


def _rope(x, cos, sin_signed):
    lane = _iota(x.shape, 1)
    half = MOBA_HEAD_DIM // 2
    width = x.shape[1]
    swapped = jnp.where(lane % MOBA_HEAD_DIM < half, pltpu.roll(x, width - half, 1), pltpu.roll(x, half, 1))
    return x * cos + swapped * sin_signed


def _sample_body(gq_ref, gk_ref, gv_ref, go_ref, glow_ref, r_ref, k_ref, v_ref, wa_ref, gr_ref, mq_ref, mk_ref,
                 sg_ref, sw_ref, shift_ref, cos_ref, sin_ref, up_ref, bias_ref, nw_ref, *rest):
    np_ = len(_RWKV_PARAM_NAMES)
    p = dict(zip(_RWKV_PARAM_NAMES, rest[:np_]))
    yg_ref, yr_ref, qs_ref, ks_ref, sg_out, sw_out = rest[np_:np_ + 6]
    qt_s, kt_s, egt_s, og_s, rt_s, wt_s, k7t_s, vt_s, kkt_s, bt_s, yt_s, bonus_s, g7_s = rest[np_ + 6:]
    hp = pl.program_id(0)
    n_seq = gq_ref.shape[0]

    @pl.when(hp == 0)
    def _():
        qs_ref[...] = _rope(mq_ref[...], cos_ref[...], sin_ref[...]) * (MOBA_HEAD_DIM ** -0.5)
        ks_ref[...] = _rope(mk_ref[...], cos_ref[...], sin_ref[...])
        lg = _gla_log_gate(glow_ref[...], up_ref, bias_ref)
        qt_s[...] = (gq_ref[...] * (GLA_DK ** -0.5)).T
        kt_s[...] = gk_ref[...].T
        egt_s[...] = jnp.exp(lg).T

        def shifted(x_ref, mu_ref, c0):
            x = x_ref[...]
            return x + mu_ref[...] * (shift_ref[:, c0:c0 + x.shape[1]] - x)

        pre = _rwkv_pre(shifted(r_ref, p["mu_r"], 0), shifted(k_ref, p["mu_k"], W), shifted(v_ref, p["mu_v"], 2 * W),
                        shifted(wa_ref, p["mu_wa"], 3 * W), shifted(gr_ref, p["mu_gr"], 3 * W + LANES), p)
        rt_s[...] = pre["r"].T
        wt_s[...] = jnp.exp(pre["logw"]).T
        k7t_s[...] = pre["k"].T
        vt_s[...] = pre["v"].T
        kkt_s[...] = pre["kk"].T
        bt_s[...] = (pre["kk"] * pre["a"]).T
        bonus_s[...] = pre["bonus"]
        g7_s[...] = pre["g7"]
        og_s[...] = jnp.zeros_like(og_s)

    ks = pl.ds(pl.multiple_of(hp * GLA_DK, GLA_DK), GLA_DK)
    eg_t, k_t, q_t = egt_s[ks, :], kt_s[ks, :], qt_s[ks, :]
    gv_h = gv_ref[:, pl.ds(pl.multiple_of(hp * GLA_DV, GLA_DV), GLA_DV)]
    lane_grp = _iota((n_seq, W), 1) // GLA_DV
    o_rows = []
    for s in range(n_seq):
        col = lambda x: jnp.broadcast_to(x[:, s:s + 1], (GLA_DK, GLA_DV))
        st = sg_ref[s] * col(eg_t) + col(k_t) * gv_h[s:s + 1, :]
        sg_out[s] = st
        o_rows.append(jnp.sum(col(q_t) * st, axis=0, keepdims=True))
    o_h = jnp.concatenate(o_rows, axis=0)
    og_s[...] = jnp.where(lane_grp == hp, jnp.concatenate([o_h] * GLA_HEADS, axis=1), og_s[...])

    group = RWKV_HEAD // SUBLANES
    for u in range(2):
        hs = pl.ds(pl.multiple_of((2 * hp + u) * RWKV_HEAD, RWKV_HEAD), RWKV_HEAD)
        neg_kk, w_t, b_t, k_t7, r_t = -kkt_s[hs, :], wt_s[hs, :], bt_s[hs, :], k7t_s[hs, :], rt_s[hs, :]

        def v_group(g, carry, u=u, hs=hs, neg_kk=neg_kk, w_t=w_t, b_t=b_t, k_t7=k_t7, r_t=r_t):
            base = pl.multiple_of((2 * hp + u) * RWKV_HEAD + g * SUBLANES, SUBLANES)
            v_rows = vt_s[pl.ds(base, SUBLANES), :]
            y_rows = []
            for j in range(SUBLANES):
                vi = g * SUBLANES + j
                st = sw_ref[u, vi]
                sa = jnp.sum(st * neg_kk, axis=0, keepdims=True)
                st = st * w_t + sa * b_t + v_rows[j:j + 1, :] * k_t7
                sw_out[u, vi] = st
                y_rows.append(jnp.sum(st * r_t, axis=0, keepdims=True))
            yt_s[pl.ds(base, SUBLANES), :] = jnp.concatenate(y_rows, axis=0)
            return carry

        lax.fori_loop(0, group, v_group, 0)

    @pl.when(hp == pl.num_programs(0) - 1)
    def _():
        og = og_s[...]
        for h in range(GLA_HEADS):
            vs = slice(h * GLA_DV, (h + 1) * GLA_DV)
            yg_ref[:, vs] = _gla_out(og[:, vs], nw_ref[...], go_ref[:, vs]).astype(yg_ref.dtype)
        yr_ref[...] = _rwkv_post(yt_s[...].T, bonus_s[...], g7_s[...], p).astype(yr_ref.dtype)


def sample_step(proj, row0, n_seq, layer, state_gla, state_wkv_t, state_shift, cos_s, sin_s,
                up_pad, bias, norm_w, params):
    r0 = row0 // n_seq
    rowmap = lambda width, col: pl.BlockSpec((n_seq, width), lambda h: (r0, col // width))
    const2 = lambda h: (0, 0)
    hk = GLA_HEADS * GLA_DK
    out_row = pl.BlockSpec((n_seq, W), const2)
    tr = lambda rows: pltpu.VMEM((rows, n_seq), f32)
    rw = lambda: pltpu.VMEM((n_seq, W), f32)
    return pl.pallas_call(
        _sample_body,
        grid=(GLA_HEADS,),
        in_specs=[rowmap(hk, C_GQ), rowmap(hk, C_GK), rowmap(W, C_GV), rowmap(W, C_GO), rowmap(LANES, C_GLOW),
                  rowmap(W, C_R), rowmap(W, C_K7), rowmap(W, C_V7), rowmap(LANES, C_WA), rowmap(LANES, C_GR),
                  rowmap(W, C_MQ), rowmap(W, C_MK),
                  pl.BlockSpec((None, n_seq, None, GLA_DK, GLA_DV), lambda h: (layer, 0, h, 0, 0)),
                  pl.BlockSpec((None, 2, RWKV_HEAD, RWKV_HEAD, n_seq), lambda h: (layer, h, 0, 0, 0)),
                  pl.BlockSpec((None, n_seq, RWKV_PROJ), lambda h: (layer, 0, 0)),
                  pl.BlockSpec((1, W), const2), pl.BlockSpec((1, W), const2),
                  pl.BlockSpec((LANES, hk), const2), pl.BlockSpec((1, hk), const2), pl.BlockSpec((1, GLA_DV), const2)]
        + _rwkv_param_specs(const2),
        out_specs=[out_row, out_row, out_row, out_row,
                   pl.BlockSpec((n_seq, None, GLA_DK, GLA_DV), lambda h: (0, h, 0, 0)),
                   pl.BlockSpec((2, RWKV_HEAD, RWKV_HEAD, n_seq), lambda h: (h, 0, 0, 0))],
        out_shape=[jax.ShapeDtypeStruct((n_seq, W), bf16), jax.ShapeDtypeStruct((n_seq, W), bf16),
                   jax.ShapeDtypeStruct((n_seq, W), f32), jax.ShapeDtypeStruct((n_seq, W), f32),
                   jax.ShapeDtypeStruct((n_seq, GLA_HEADS, GLA_DK, GLA_DV), f32),
                   jax.ShapeDtypeStruct((RWKV_HEADS, RWKV_HEAD, RWKV_HEAD, n_seq), f32)],
        scratch_shapes=[tr(hk), tr(hk), tr(hk), rw(), tr(W), tr(W), tr(W), tr(W), tr(W), tr(W), tr(W), rw(), rw()],
        compiler_params=_params(("arbitrary",)),
        name="sample_step",
    )(proj, proj, proj, proj, proj, proj, proj, proj, proj, proj, proj, proj,
      state_gla, state_wkv_t, state_shift, cos_s, sin_s, up_pad, bias, norm_w,
      *[params[n] for n in _RWKV_PARAM_NAMES])


def _topk_select(gate, n_valid, axis_len, axis):
    idx = _iota(gate.shape, axis)
    cnt = jnp.zeros(gate.shape, f32)
    for m in range(axis_len):
        g_m = lax.slice_in_dim(gate, m, m + 1, axis=axis)
        g_m = jnp.broadcast_to(g_m, gate.shape)
        beats = (g_m > gate) | ((g_m == gate) & (m < idx))
        cnt = cnt + jnp.where(beats, 1.0, 0.0) * jnp.where(m < n_valid, 1.0, 0.0)
    return (idx < n_valid) & (cnt < MOBA_TOPK)


def _rope_prep_body(mq_ref, mk_ref, mv_ref, cos_ref, sin_ref, k_stack_in, v_stack_in,
                    k_out, v_out, qb_out, kb_out, vb_out, sel_out, km_ref):
    del k_stack_in, v_stack_in
    i = pl.program_id(1)
    blk = mq_ref.shape[0]
    nb = km_ref.shape[0]
    nh = MOBA_HEADS

    @pl.when(i == 0)
    def _():
        km_ref[...] = jnp.zeros_like(km_ref)

    q = _rope(mq_ref[...], cos_ref[...], sin_ref[...]) * (MOBA_HEAD_DIM ** -0.5)
    k = _rope(mk_ref[...], cos_ref[...], sin_ref[...])
    k_out[...] = k
    v_out[...] = mv_ref[...]
    qb_out[...] = (q * LOG2_E).astype(bf16)
    kb_out[...] = k.astype(bf16)
    vb_out[...] = mv_ref[...].astype(bf16)

    km = km_ref[...]
    km_rows = jnp.concatenate([km] * nh, axis=0)
    head_of_row = _iota(km_rows.shape, 0) // nb
    km_rows = jnp.where(head_of_row == _iota(km_rows.shape, 1) // MOBA_HEAD_DIM, km_rows, 0.0)
    gate = _dot3_nt(km_rows, q).reshape(nh, nb, blk)
    sel = _topk_select(gate, i, nb, 1).astype(bf16).reshape(nh * nb, blk)
    sel = jnp.concatenate([sel, jnp.zeros((LANES - nh * nb, blk), bf16)], axis=0)
    eye = (_iota((blk, blk), 0) == _iota((blk, blk), 1)).astype(bf16)
    sel_out[...] = (lax.dot_general(eye, sel, NT_DIMS, preferred_element_type=f32) - 1.0) * MASK_BIG

    _set_row(km_ref, i, jnp.sum(k, axis=0, keepdims=True) * (1.0 / blk))


def rope_prep(proj, cos, sin, n_batch, seq, layer, k_stack, v_stack):
    blk = MOBA_BLOCK
    nb = seq // blk
    m = n_batch * seq
    rowmap = lambda col: pl.BlockSpec((blk, W), lambda b, i: (b * nb + i, col // W))
    tab = pl.BlockSpec((blk, W), lambda b, i: (i, 0))
    out = lambda width: pl.BlockSpec((blk, width), lambda b, i: (b * nb + i, 0))
    slab = pl.BlockSpec((None, blk, W), lambda b, i: (layer, b * nb + i, 0))
    anywhere = pl.BlockSpec(memory_space=pl.ANY)
    return pl.pallas_call(
        _rope_prep_body,
        grid=(n_batch, nb),
        in_specs=[rowmap(C_MQ), rowmap(C_MK), rowmap(C_MV), tab, tab, anywhere, anywhere],
        out_specs=[slab, slab, out(W), out(W), out(W), out(LANES)],
        out_shape=[jax.ShapeDtypeStruct(k_stack.shape, f32), jax.ShapeDtypeStruct(v_stack.shape, f32),
                   jax.ShapeDtypeStruct((m, W), bf16), jax.ShapeDtypeStruct((m, W), bf16),
                   jax.ShapeDtypeStruct((m, W), bf16), jax.ShapeDtypeStruct((m, LANES), f32)],
        scratch_shapes=[pltpu.VMEM((nb, W), f32)],
        input_output_aliases={5: 0, 6: 1},
        compiler_params=_params(("arbitrary", "arbitrary")),
        name="rope_prep",
    )(proj, proj, proj, cos, sin, k_stack, v_stack)


def _moba_prompt_body(qb_ref, kb_ref, vb_ref, sel_ref, y_ref, m_ref, acc_ref):
    i = pl.program_id(1)
    blk = qb_ref.shape[0]
    nb = kb_ref.shape[0] // blk
    lane = _iota((blk, LANES), 1)
    low_half = lane < MOBA_HEAD_DIM
    causal = _iota((blk, blk), 1) <= _iota((blk, blk), 0)
    neg_inf = -jnp.inf

    def head_q(pair, u):
        qp = qb_ref[:, pair * LANES:(pair + 1) * LANES]
        return jnp.where(low_half if u == 0 else ~low_half, qp, jnp.zeros_like(qp))

    def kv(n, pair):
        rs = pl.ds(pl.multiple_of(n * blk, blk), blk)
        ls = slice(pair * LANES, (pair + 1) * LANES)
        return kb_ref[rs, ls], vb_ref[rs, ls]

    wide = lambda x: jnp.concatenate([x, x], axis=1)

    def head_v(v_n, u):
        return jnp.where(low_half if u == 0 else ~low_half, v_n, jnp.ones_like(v_n))

    all_heads = [(pair, u) for pair in range(MOBA_HEADS // 2) for u in range(2)]
    kv_own = [kv(i, pair) for pair in range(MOBA_HEADS // 2)]
    s_own = [lax.dot_general(head_q(pair, u), kv_own[pair][0], NT_DIMS, preferred_element_type=f32)
             for pair, u in all_heads]
    s_own = [jnp.where(causal, x, neg_inf) for x in s_own]
    m_own = [jnp.broadcast_to(jnp.max(x, axis=1, keepdims=True), (blk, LANES)) for x in s_own]
    p_own = [jnp.exp2(x - wide(m)).astype(bf16) for x, m in zip(s_own, m_own)]
    for h, (pair, u) in enumerate(all_heads):
        m_ref[h] = m_own[h]
        acc_ref[h] = jnp.dot(p_own[h], head_v(kv_own[pair][1], u), preferred_element_type=f32)

    mask_b = sel_ref[...].astype(bf16)
    n_heads = MOBA_HEADS
    mask_lane = lambda u: MOBA_HEAD_DIM if u == 0 else 0
    hot_row = _iota((LANES, n_heads * LANES), 0)
    hot_col = _iota((LANES, n_heads * LANES), 1)
    hot_head = hot_col // LANES
    hot_lane_ok = hot_col % LANES == jnp.where(hot_head % 2 == 0, mask_lane(0), mask_lane(1))

    def past(n, carry):
        heads = [(pair, u) for pair in range(n_heads // 2) for u in range(2)]
        kvs = [kv(n, pair) for pair in range(n_heads // 2)]
        one_hot = (hot_lane_ok & (hot_row == hot_head * nb + n)).astype(bf16)
        bias = jnp.dot(mask_b, one_hot, preferred_element_type=f32).astype(bf16)
        q_m = [head_q(pair, u) + bias[:, (2 * pair + u) * LANES:(2 * pair + u + 1) * LANES] for pair, u in heads]
        k_e = [jnp.where(lane == mask_lane(u), jnp.ones_like(kvs[pair][0]), kvs[pair][0]) for pair, u in heads]
        s = [lax.dot_general(q, k, NT_DIMS, preferred_element_type=f32) for q, k in zip(q_m, k_e)]
        m_old = [m_ref[h] for h in range(n_heads)]
        m_new = [jnp.maximum(mo, jnp.broadcast_to(jnp.max(x, axis=1, keepdims=True), (blk, LANES)))
                 for mo, x in zip(m_old, s)]
        p = [jnp.exp2(x - wide(mn)).astype(bf16) for x, mn in zip(s, m_new)]
        pv = [jnp.dot(pp, head_v(kvs[pair][1], u), preferred_element_type=f32) for pp, (pair, u) in zip(p, heads)]
        for h in range(n_heads):
            m_ref[h] = m_new[h]
            acc_ref[h] = jnp.exp2(m_old[h] - m_new[h]) * acc_ref[h] + pv[h]
        return carry

    lax.fori_loop(0, i, past, 0)

    for pair in range(MOBA_HEADS // 2):
        halves = []
        for u in range(2):
            acc = acc_ref[2 * pair + u]
            halves.append(acc / pltpu.roll(acc, MOBA_HEAD_DIM, 1))
        y_ref[:, pair * LANES:(pair + 1) * LANES] = jnp.where(low_half, halves[0], halves[1]).astype(y_ref.dtype)


def moba_prompt(qb, kb, vb, sel, n_batch, seq):
    blk = MOBA_BLOCK
    nb = seq // blk
    row = lambda width: pl.BlockSpec((blk, width), lambda b, i: (b * nb + i, 0))
    full = pl.BlockSpec((seq, W), lambda b, i: (b, 0))
    scr = lambda: pltpu.VMEM((MOBA_HEADS, blk, LANES), f32)
    return pl.pallas_call(
        _moba_prompt_body,
        grid=(n_batch, nb),
        in_specs=[row(W), full, full, row(LANES)],
        out_specs=row(W),
        out_shape=jax.ShapeDtypeStruct((n_batch * seq, W), bf16),
        scratch_shapes=[scr(), scr()],
        compiler_params=_params(("arbitrary", "arbitrary")),
        name="moba_prompt",
    )(qb, kb, vb, sel)


def _moba_sample_seq(q_b, q_heads, k_heads, v_row, k_page, v_page, n_pages):
    nh, dh = MOBA_HEADS, MOBA_HEAD_DIM
    pages_per_block = MOBA_BLOCK // PAGE_SIZE
    nb = n_pages // pages_per_block
    lane = _iota((nh, LANES), 1)

    def head_sum(x):
        return jnp.concatenate([jnp.sum(x[h * dh:(h + 1) * dh], axis=0, keepdims=True) for h in range(nh)], axis=0)

    def head_rows(x):
        return jnp.concatenate([jnp.broadcast_to(x[h:h + 1], (dh, x.shape[1])) for h in range(nh)], axis=0)

    s_pages = [head_sum(k_page(g).reshape(nh * dh, PAGE_SIZE) * q_b) for g in range(n_pages)]
    gate = jnp.zeros((nh, LANES), f32)
    for n in range(nb):
        blk_sum = sum(jnp.sum(s_pages[n * pages_per_block + t], axis=1, keepdims=True)
                      for t in range(pages_per_block))
        gate = jnp.where(lane == n, blk_sum * (1.0 / MOBA_BLOCK), gate)
    sel = _topk_select(gate, nb, nb, 1).astype(f32)
    s_own = jnp.sum(q_heads * k_heads, axis=1, keepdims=True)
    s_pages = [jnp.where(jnp.broadcast_to(sel[:, g // pages_per_block:g // pages_per_block + 1],
                                          (nh, PAGE_SIZE)) > 0.5, s, -jnp.inf)
               for g, s in enumerate(s_pages)]
    m = s_own
    for s in s_pages:
        m = jnp.maximum(m, jnp.max(s, axis=1, keepdims=True))
    p_pages = [jnp.exp(s - m) for s in s_pages]
    w_own = jnp.exp(s_own - m)
    l_tot = w_own + sum(jnp.sum(p, axis=1, keepdims=True) for p in p_pages)

    acc = jnp.zeros((nh * dh, PAGE_SIZE), f32)
    for g in range(n_pages):
        acc = acc + head_rows(p_pages[g]) * v_page(g).reshape(nh * dh, PAGE_SIZE)
    hi = acc.astype(bf16)
    mid = (acc - hi.astype(f32))
    lo = (mid - mid.astype(bf16).astype(f32)).astype(bf16)
    ones = jnp.ones((nh, PAGE_SIZE), bf16)
    nt = lambda a: lax.dot_general(ones, a, NT_DIMS, preferred_element_type=f32)
    out = nt(hi) + nt(mid.astype(bf16)) + nt(lo)
    out = (out + w_own * v_row) / l_tot
    head_lane = _iota((nh, nh * dh), 1) // dh == _iota((nh, nh * dh), 0)
    return jnp.sum(jnp.where(head_lane, out, 0.0), axis=0, keepdims=True)


def _rope_tables(pos):
    half = MOBA_HEAD_DIM // 2
    inv = ROPE_THETA ** (-jnp.arange(half, dtype=f32) / half)
    ang = pos.astype(f32)[:, None] * inv[None, :]
    cos, sin = jnp.cos(ang), jnp.sin(ang)
    cos_h = jnp.concatenate([cos, cos], -1)
    sin_h = jnp.concatenate([-sin, sin], -1)
    return jnp.tile(cos_h, (1, MOBA_HEADS)), jnp.tile(sin_h, (1, MOBA_HEADS))


def _pack_w_in(w):
    cols = lambda o, n: w[:, o:o + n]
    hk = GLA_HEADS * GLA_DK
    parts = [cols(_O_GQ, hk), cols(_O_GK, hk), cols(_O_GV, W), cols(_O_GO, W),
             cols(_O_MQ, W), cols(_O_MK, W), cols(_O_MV, W), cols(_O_PG, N_BRANCH * D_MODEL),
             cols(_O_PR, RWKV_PROJ), cols(_O_GLOW, GLA_GATE_RANK),
             jnp.zeros((w.shape[0], N_PACK - C_GLOW - GLA_GATE_RANK), w.dtype)]
    return jnp.concatenate(parts, axis=1).astype(bf16)


def _pad_rows(m, rows):
    return jnp.concatenate([m, jnp.zeros((rows - m.shape[0], m.shape[1]), m.dtype)], axis=0)


SUBLANES = 8
IN_PROJ_ROWS_TARGET, IN_PROJ_TN = 2048, 1024
IN_PROJ_ROW_CHUNK = 256
TOKEN_ROWS_TARGET = 1024
MERGE_ROWS_TARGET = 512
MIXER_ROWS = 256


def _row_tile(m, target):
    best = SUBLANES
    for t in range(SUBLANES, target + 1, SUBLANES):
        if m % t == 0:
            best = t
    assert m % best == 0
    return best


def kernel(x_prompt, x_sample, cache_k, cache_v, page_table, state_gla, state_wkv, state_shift, w_in, b_gate, gla_gk_up, gla_gk_bias, gla_norm_w, rwkv_mu, rwkv_w0, rwkv_w_up, rwkv_a0, rwkv_a_up, rwkv_g_up, rwkv_k_k, rwkv_k_a, rwkv_r_k, rwkv_ln_w, rwkv_ln_b, w_branch, w_out, ln1_g, ln1_b, w_up, w_down, ln2_g, ln2_b):
    n_batch, seq, _ = x_prompt.shape
    n_seq = x_sample.shape[0]
    n_pages = page_table.shape[1]
    mp = n_batch * seq
    past = n_pages * PAGE_SIZE

    cos_p, sin_p = _rope_tables(jnp.arange(seq, dtype=jnp.int32))
    cos_s, sin_s = _rope_tables(jnp.full((1,), past, jnp.int32))
    pt_flat = page_table.reshape(-1)
    cache_kt = jnp.transpose(cache_k, (0, 1, 3, 4, 2))
    cache_vt = jnp.transpose(cache_v, (0, 1, 3, 4, 2))
    state_wkv_t = jnp.transpose(state_wkv, (0, 2, 3, 4, 1))

    x_p, x_s = x_prompt.reshape(mp, D_MODEL), x_sample.reshape(n_seq, D_MODEL)
    outs = {k: [] for k in ("k_s", "v_s", "gla_p", "gla_s", "wkv_p", "wkv_s", "shift_p", "shift_s")}
    k_stack = jnp.zeros((DEPTH, mp, W), f32)
    v_stack = jnp.zeros((DEPTH, mp, W), f32)
    for l in range(DEPTH):
        row2 = lambda v: v.reshape(1, -1)
        mu = rwkv_mu[l]
        rp = dict(
            mu_r=row2(mu[:W]), mu_k=row2(mu[W:2 * W]), mu_v=row2(mu[2 * W:3 * W]),
            mu_wa=row2(mu[3 * W:3 * W + LANES]), mu_gr=row2(mu[3 * W + LANES:]),
            w0=row2(rwkv_w0[l]), w_up=_pad_rows(rwkv_w_up[l], LANES).astype(bf16),
            a0=row2(rwkv_a0[l]),
            a_up=jnp.concatenate([jnp.zeros((RWKV_DECAY_RANK, W), f32), rwkv_a_up[l]], axis=0).astype(bf16),
            g_up=rwkv_g_up[l].astype(bf16), k_k=row2(rwkv_k_k[l]), k_a=row2(rwkv_k_a[l]),
            r_k=row2(rwkv_r_k[l]), ln_w=row2(rwkv_ln_w[l]), ln_b=row2(rwkv_ln_b[l]))
        up_pad = _pad_rows(gla_gk_up[l], LANES).astype(bf16)
        gk_bias = row2(gla_gk_bias[l])
        norm_w = row2(gla_norm_w[l])

        w_packed = _pack_w_in(w_in[l])
        wb, wo = w_branch[l].astype(bf16), w_out[l].astype(bf16)
        wu, wd = w_up[l].astype(bf16), w_down[l].astype(bf16)
        ln1, ln2 = (row2(ln1_g[l]), row2(ln1_b[l])), (row2(ln2_g[l]), row2(ln2_b[l]))

        def token_tail(x_rows, proj_rows, y_gla, y_moba, y_rwkv):
            tm = _row_tile(x_rows.shape[0], TOKEN_ROWS_TARGET)
            x1 = merge_ln(x_rows, proj_rows, y_gla, y_moba, y_rwkv, wb, b_gate[l], wo, *ln1,
                          _row_tile(x_rows.shape[0], MERGE_ROWS_TARGET))
            return mlp_ln(x1, wu, wd, *ln2, tm)

        proj_p = in_proj(x_p, w_packed, _row_tile(mp, IN_PROJ_ROWS_TARGET), IN_PROJ_TN)
        proj_s = in_proj(x_s, w_packed, n_seq, IN_PROJ_TN)

        y_gla_s, y_rwkv_s, q_s, k_s, gla_s, wkv_s_t = sample_step(
            proj_s, 0, n_seq, l, state_gla, state_wkv_t, state_shift, cos_s, sin_s, up_pad, gk_bias, norm_w, rp)
        v_s = proj_s[:, C_MV:C_MV + W]
        heads = (n_seq, MOBA_HEADS, MOBA_HEAD_DIM)
        q_lanes = jnp.broadcast_to(q_s[:, :, None], (n_seq, W, PAGE_SIZE))

        y_gla_p, gla_p = gla_prompt(proj_p, up_pad, gk_bias, norm_w, n_batch, seq, MIXER_ROWS)
        y_rwkv_p, wkv_p, shift_p, y_moba_s = rwkv_prompt_moba_sample(
            proj_p, rp, n_batch, seq, MIXER_ROWS, l, pt_flat, q_lanes, q_s.reshape(heads), k_s.reshape(heads),
            v_s.reshape(n_seq, 1, W), cache_kt, cache_vt, n_pages)
        k_stack, v_stack, qb, kb, vb, sel = rope_prep(proj_p, cos_p, sin_p, n_batch, seq, l, k_stack, v_stack)
        y_moba_p = moba_prompt(qb, kb, vb, sel, n_batch, seq)
        x_p_next = token_tail(x_p, proj_p, y_gla_p, y_moba_p, y_rwkv_p)
        x_s_next = token_tail(x_s, proj_s, y_gla_s, y_moba_s.reshape(n_seq, W).astype(bf16), y_rwkv_s)

        x_p, x_s = x_p_next, x_s_next
        hd = (MOBA_HEADS, MOBA_HEAD_DIM)
        outs["k_s"].append(k_s.reshape(n_seq, 1, *hd))
        outs["v_s"].append(v_s.reshape(n_seq, 1, *hd))
        outs["gla_p"].append(gla_p)
        outs["gla_s"].append(gla_s)
        wkv_p = wkv_p.reshape(n_batch, RWKV_HEADS // 2, RWKV_HEAD, 2, RWKV_HEAD)
        outs["wkv_p"].append(wkv_p.transpose(0, 1, 3, 2, 4).reshape(n_batch, RWKV_HEADS, RWKV_HEAD, RWKV_HEAD))
        outs["wkv_s"].append(jnp.transpose(wkv_s_t, (3, 0, 1, 2)))
        outs["shift_p"].append(shift_p[:, 0, :])
        outs["shift_s"].append(proj_s[:, C_PR:C_PR + RWKV_PROJ])

    st = lambda k: jnp.stack(outs[k])
    kv_shape = (DEPTH, n_batch, seq, MOBA_HEADS, MOBA_HEAD_DIM)
    return (x_p.reshape(n_batch, seq, D_MODEL), x_s.reshape(n_seq, 1, D_MODEL),
            k_stack.reshape(kv_shape), v_stack.reshape(kv_shape), st("k_s"), st("v_s"), st("gla_p"), st("gla_s"),
            st("wkv_p"), st("wkv_s"), st("shift_p"), st("shift_s"))
```

```python
import functools

import jax
import jax.numpy as jnp
from jax import lax
from jax.experimental import pallas as pl
from jax.experimental.pallas import tpu as pltpu

f32, bf16 = jnp.float32, jnp.bfloat16

D_MODEL = 1024
DEPTH = 2
PAGE_SIZE = 128
BRANCH_WIDTH = D_MODEL // 2
N_BRANCH = 3
GLA_HEADS = 4
GLA_DV = BRANCH_WIDTH // GLA_HEADS
GLA_DK = GLA_DV // 2
GLA_GATE_RANK = 16
GLA_GATE_NORMALIZER = 16.0
GLA_CHUNK = 64
GLA_NORM_EPS = 1e-5
MOBA_HEAD_DIM = 64
MOBA_HEADS = BRANCH_WIDTH // MOBA_HEAD_DIM
MOBA_BLOCK = 256
MOBA_TOPK = 3
ROPE_THETA = 10000.0
RWKV_HEAD = 64
RWKV_HEADS = BRANCH_WIDTH // RWKV_HEAD
RWKV_DECAY_RANK = 64
RWKV_A_RANK = 64
RWKV_GATE_RANK = 128
RWKV_DECAY_SCALE = 0.606531
RWKV_GN_EPS = 64e-5
RWKV_PROJ = 3 * BRANCH_WIDTH + RWKV_DECAY_RANK + RWKV_A_RANK + RWKV_GATE_RANK
RWKV_CHUNK = 64
RWKV_INV_BLOCK = 16
D_FF = 4 * D_MODEL
ALPHA = (2 * DEPTH) ** 0.25
LN_EPS = 1e-5
LOG2_E = 1.4426950408889634
MASK_BIG = 2.0 ** 100

LANES = 128
VMEM_LIMIT_BYTES = 56 * 1024 * 1024

W = BRANCH_WIDTH
C_GQ, C_GK, C_GV, C_GO = 0, 256, 512, 1024
C_MQ, C_MK, C_MV = 1536, 2048, 2560
C_PG = 3072
C_PR = 6144
C_R, C_K7, C_V7 = C_PR, C_PR + W, C_PR + 2 * W
C_WA = C_PR + 3 * W
C_GR = C_WA + LANES
C_GLOW = C_PR + RWKV_PROJ
N_PACK = 8192

_O_GQ, _O_GK, _O_GV, _O_GLOW, _O_GO = 0, 256, 512, 1024, 1040
_O_MQ, _O_MK, _O_MV, _O_PR, _O_PG = 1552, 2064, 2576, 3088, 4880

NT_DIMS = (((1,), (1,)), ((), ()))
TN_DIMS = (((0,), (0,)), ((), ()))


def _params(sem):
    return pltpu.CompilerParams(dimension_semantics=sem, vmem_limit_bytes=VMEM_LIMIT_BYTES)


def _bdot(a, b):
    return jnp.dot(a.astype(bf16), b.astype(bf16), preferred_element_type=f32)


def _bdot_nt(a, b):
    return lax.dot_general(a.astype(bf16), b.astype(bf16), NT_DIMS, preferred_element_type=f32)


def _bdot_tn(a, b):
    return lax.dot_general(a.astype(bf16), b.astype(bf16), TN_DIMS, preferred_element_type=f32)


def _split2(x):
    hi = x.astype(bf16)
    lo = (x - hi.astype(f32)).astype(bf16)
    return hi, lo


def _dot_exact_rhs(x, m_bf16):
    hi, lo = _split2(x)
    return (jnp.dot(hi, m_bf16, preferred_element_type=f32)
            + jnp.dot(lo, m_bf16, preferred_element_type=f32))


def _dot_exact_lhs(m_bf16, x):
    hi, lo = _split2(x)
    return (jnp.dot(m_bf16, hi, preferred_element_type=f32)
            + jnp.dot(m_bf16, lo, preferred_element_type=f32))


def _dot3_nt(a, b):
    ah, al = _split2(a)
    bh, bl = _split2(b)
    d = lambda u, v: lax.dot_general(u, v, NT_DIMS, preferred_element_type=f32)
    return d(ah, bh) + d(al, bh) + d(ah, bl)


def _iota(shape, dim):
    return lax.broadcasted_iota(jnp.int32, shape, dim)


def _seg_matrix(n, seg):
    return (_iota((n, n), 0) // seg == _iota((n, n), 1) // seg).astype(bf16)


def _layer_norm(z, g, b):
    mu = jnp.mean(z, -1, keepdims=True)
    d = z - mu
    var = jnp.mean(d * d, -1, keepdims=True)
    return d * lax.rsqrt(var + LN_EPS) * g + b


def _row_to_col(row, n):
    eye = _iota((n, n), 0) == _iota((n, n), 1)
    return jnp.sum(jnp.where(eye, jnp.broadcast_to(row, (n, n)), 0.0), axis=1, keepdims=True)


def _set_row(ref, idx, row):
    cur = ref[...]
    ref[...] = jnp.where(_iota(cur.shape, 0) == idx, jnp.broadcast_to(row, cur.shape), cur)


def _col_to_row(col, n):
    eye = _iota((n, n), 0) == _iota((n, n), 1)
    return jnp.sum(jnp.where(eye, jnp.broadcast_to(col, (n, n)), 0.0), axis=0, keepdims=True)


def _inproj_body(x_ref, w_ref, o_ref, xb_ref):
    @pl.when(pl.program_id(1) == 0)
    def _():
        xb_ref[...] = x_ref[...].astype(bf16)

    rows = x_ref.shape[0]
    for r in range(0, rows, IN_PROJ_ROW_CHUNK):
        rc = min(IN_PROJ_ROW_CHUNK, rows - r)
        o_ref[r:r + rc, :] = jnp.dot(xb_ref[r:r + rc, :], w_ref[...], preferred_element_type=f32)


def in_proj(x, w, tm, tn):
    m, k = x.shape
    n = w.shape[1]
    return pl.pallas_call(
        _inproj_body,
        grid=(m // tm, n // tn),
        in_specs=[pl.BlockSpec((tm, k), lambda i, j: (i, 0)),
                  pl.BlockSpec((k, tn), lambda i, j: (0, j))],
        out_specs=pl.BlockSpec((tm, tn), lambda i, j: (i, j)),
        out_shape=jax.ShapeDtypeStruct((m, n), f32),
        scratch_shapes=[pltpu.VMEM((tm, k), bf16)],
        compiler_params=_params(("arbitrary", "arbitrary")),
        name="in_proj",
    )(x, w)


def _merge_body(x_ref, pg0_ref, pg1_ref, pg2_ref, y0_ref, y1_ref, y2_ref, wb_ref, bg_ref, wo_ref,
                g_ref, b_ref, o_ref):
    merged = None
    for n, (pg_ref, y_ref) in enumerate(((pg0_ref, y0_ref), (pg1_ref, y1_ref), (pg2_ref, y2_ref))):
        gate = jax.nn.sigmoid(pg_ref[...] + bg_ref[n:n + 1, :])
        term = gate * jnp.dot(y_ref[...], wb_ref[n], preferred_element_type=f32)
        merged = term if merged is None else merged + term
    out = jnp.dot(merged.astype(bf16), wo_ref[...], preferred_element_type=f32)
    o_ref[...] = _layer_norm(ALPHA * x_ref[...] + out, g_ref[...], b_ref[...])


def merge_ln(x, proj, y_gla, y_moba, y_rwkv, w_branch, b_gate, w_out, g, b, tm):
    m = x.shape[0]
    row = lambda i: (i, 0)
    const2 = lambda i: (0, 0)
    pg_spec = lambda n: pl.BlockSpec((tm, D_MODEL), lambda i, n=n: (i, C_PG // D_MODEL + n))
    y_spec = pl.BlockSpec((tm, W), row)
    return pl.pallas_call(
        _merge_body,
        grid=(m // tm,),
        in_specs=[pl.BlockSpec((tm, D_MODEL), row), pg_spec(0), pg_spec(1), pg_spec(2),
                  y_spec, y_spec, y_spec,
                  pl.BlockSpec((N_BRANCH, W, D_MODEL), lambda i: (0, 0, 0)),
                  pl.BlockSpec((N_BRANCH, D_MODEL), const2),
                  pl.BlockSpec((D_MODEL, D_MODEL), const2),
                  pl.BlockSpec((1, D_MODEL), const2), pl.BlockSpec((1, D_MODEL), const2)],
        out_specs=pl.BlockSpec((tm, D_MODEL), row),
        out_shape=jax.ShapeDtypeStruct((m, D_MODEL), f32),
        compiler_params=_params(("arbitrary",)),
        name="merge_ln",
    )(x, proj, proj, proj, y_gla, y_moba, y_rwkv, w_branch, b_gate, w_out, g, b)


FF_CHUNK = 1024


def _mlp_body(x_ref, wu_ref, wd_ref, g_ref, b_ref, o_ref):
    x = x_ref[...]
    xb = x.astype(bf16)
    acc = None
    for c in range(D_FF // FF_CHUNK):
        h = jnp.dot(xb, wu_ref[:, c * FF_CHUNK:(c + 1) * FF_CHUNK], preferred_element_type=f32)
        h = jnp.square(jnp.maximum(h, 0.0)).astype(bf16)
        part = jnp.dot(h, wd_ref[c * FF_CHUNK:(c + 1) * FF_CHUNK, :], preferred_element_type=f32)
        acc = part if acc is None else acc + part
    o_ref[...] = _layer_norm(ALPHA * x + acc, g_ref[...], b_ref[...])


def mlp_ln(x, w_up, w_down, g, b, tm):
    m = x.shape[0]
    row = lambda i: (i, 0)
    const2 = lambda i: (0, 0)
    return pl.pallas_call(
        _mlp_body,
        grid=(m // tm,),
        in_specs=[pl.BlockSpec((tm, D_MODEL), row),
                  pl.BlockSpec((D_MODEL, D_FF), const2, pipeline_mode=pl.Buffered(1)),
                  pl.BlockSpec((D_FF, D_MODEL), const2, pipeline_mode=pl.Buffered(1)),
                  pl.BlockSpec((1, D_MODEL), const2), pl.BlockSpec((1, D_MODEL), const2)],
        out_specs=pl.BlockSpec((tm, D_MODEL), row),
        out_shape=jax.ShapeDtypeStruct((m, D_MODEL), f32),
        compiler_params=_params(("arbitrary",)),
        name="mlp_ln",
    )(x, w_up, w_down, g, b)


def _gla_log_gate(glow, up_ref, bias_ref):
    z = _bdot(glow, up_ref[...]) + bias_ref[...]
    return jax.nn.log_sigmoid(z) * (1.0 / GLA_GATE_NORMALIZER)


def _gla_out(o, norm_w, g_out):
    o = o * lax.rsqrt(jnp.mean(jnp.square(o), -1, keepdims=True) + GLA_NORM_EPS) * norm_w
    return o * jax.nn.silu(g_out)


def _gla_prompt_body(q_ref, k_ref, v_ref, go_ref, glow_ref, up_ref, bias_ref, nw_ref,
                     y_ref, s_out_ref, s_ref):
    t = pl.program_id(1)
    rows = q_ref.shape[0]
    c_sz = GLA_CHUNK

    @pl.when(t == 0)
    def _():
        s_ref[...] = jnp.zeros_like(s_ref)

    lg = _gla_log_gate(glow_ref[...], up_ref, bias_ref)
    tril = (_iota((c_sz, c_sz), 1) <= _iota((c_sz, c_sz), 0))
    tril_b = tril.astype(bf16)
    n_chunks = rows // c_sz
    probs = []
    for c in range(n_chunks):
        rs = slice(c * c_sz, (c + 1) * c_sz)
        g_cum = _dot_exact_lhs(tril_b, lg[rs])
        g_last = g_cum[c_sz - 1:c_sz, :]
        q_dec = q_ref[rs, :] * (GLA_DK ** -0.5) * jnp.exp(g_cum)
        k = k_ref[rs, :]
        k_inv = k * jnp.exp(-g_cum)
        k_tail = k * jnp.exp(g_last - g_cum)
        e_last = jnp.exp(g_last)
        for h in range(GLA_HEADS):
            ks = slice(h * GLA_DK, (h + 1) * GLA_DK)
            vs = slice(h * GLA_DV, (h + 1) * GLA_DV)
            probs.append((q_dec[:, ks], k_inv[:, ks], k_tail[:, ks], e_last[:, ks], v_ref[rs, vs], rs, vs))
    a = [jnp.where(tril, _bdot_nt(q[0], q[1]), 0.0) for q in probs]
    o_intra = [_bdot(x, q[4]) for x, q in zip(a, probs)]
    kv = [_bdot_tn(q[2], q[4]) for q in probs]
    e_col = [_row_to_col(q[3], GLA_DK) for q in probs]

    state = [s_ref[h] for h in range(GLA_HEADS)]
    for c in range(n_chunks):
        sl = slice(c * GLA_HEADS, (c + 1) * GLA_HEADS)
        o = [oi + _bdot(q[0], s) for oi, q, s in zip(o_intra[sl], probs[sl], state)]
        state = [s * e + x for s, e, x in zip(state, e_col[sl], kv[sl])]
        for oo, q in zip(o, probs[sl]):
            rs, vs = q[5], q[6]
            y_ref[rs, vs] = _gla_out(oo, nw_ref[...], go_ref[rs, vs]).astype(y_ref.dtype)
    for h in range(GLA_HEADS):
        s_ref[h] = state[h]
    s_out_ref[...] = s_ref[...]


def gla_prompt(proj, up_pad, bias, norm_w, n_batch, seq, rt):
    nt = seq // rt
    rowmap = lambda width, col: pl.BlockSpec((rt, width), lambda b, t: (b * nt + t, col // width))
    const2 = lambda b, t: (0, 0)
    return pl.pallas_call(
        _gla_prompt_body,
        grid=(n_batch, nt),
        in_specs=[rowmap(GLA_HEADS * GLA_DK, C_GQ), rowmap(GLA_HEADS * GLA_DK, C_GK), rowmap(W, C_GV),
                  rowmap(W, C_GO), rowmap(LANES, C_GLOW),
                  pl.BlockSpec((LANES, GLA_HEADS * GLA_DK), const2),
                  pl.BlockSpec((1, GLA_HEADS * GLA_DK), const2),
                  pl.BlockSpec((1, GLA_DV), const2)],
        out_specs=[pl.BlockSpec((rt, W), lambda b, t: (b * nt + t, 0)),
                   pl.BlockSpec((None, GLA_HEADS, GLA_DK, GLA_DV), lambda b, t: (b, 0, 0, 0))],
        out_shape=[jax.ShapeDtypeStruct((n_batch * seq, W), bf16),
                   jax.ShapeDtypeStruct((n_batch, GLA_HEADS, GLA_DK, GLA_DV), f32)],
        scratch_shapes=[pltpu.VMEM((GLA_HEADS, GLA_DK, GLA_DV), f32)],
        compiler_params=_params(("arbitrary", "arbitrary")),
        name="gla_prompt",
    )(proj, proj, proj, proj, proj, up_pad, bias, norm_w)


def _rwkv_pre(r_s, k_s, v_s, wa_s, gr_s, p):
    seg = _seg_matrix(W, RWKV_HEAD)
    logw = -RWKV_DECAY_SCALE * jax.nn.sigmoid(p["w0"][...] + _bdot(jnp.tanh(wa_s), p["w_up"][...]))
    a = jax.nn.sigmoid(p["a0"][...] + _bdot(wa_s, p["a_up"][...]))
    g7 = _bdot(jax.nn.sigmoid(gr_s), p["g_up"][...])
    kk = k_s * p["k_k"][...]
    kk = kk * lax.rsqrt(_dot_exact_rhs(jnp.square(kk), seg) + 1e-12)
    k7 = k_s * (1.0 + (a - 1.0) * p["k_a"][...])
    bonus = _dot_exact_rhs(r_s * k7 * p["r_k"][...], seg) * v_s
    return dict(r=r_s, logw=logw, k=k7, v=v_s, kk=kk, a=a, g7=g7, bonus=bonus)


def _rwkv_post(yw, bonus, g7, p):
    seg = _seg_matrix(W, RWKV_HEAD)
    inv_n = 1.0 / RWKV_HEAD
    mu = _dot_exact_rhs(yw, seg) * inv_n
    d = yw - mu
    var = _dot_exact_rhs(jnp.square(d), seg) * inv_n
    yn = d * lax.rsqrt(var + RWKV_GN_EPS) * p["ln_w"][...] + p["ln_b"][...]
    return (yn + bonus) * g7


def _pair_bd(x):
    low = _iota(x.shape, 1) < x.shape[1] // 2
    zero = jnp.zeros_like(x)
    return jnp.concatenate([jnp.where(low, x, zero), jnp.where(low, zero, x)], axis=0)


def _pair_mm(a, b):
    return jnp.dot(a.astype(bf16), _pair_bd(b.astype(bf16)), preferred_element_type=f32)


def _pair_nt(a, bd_b):
    return lax.dot_general(a.astype(bf16), bd_b, NT_DIMS, preferred_element_type=f32)


def _pair_tn(a, b):
    full = lax.dot_general(a.astype(bf16), b.astype(bf16), TN_DIMS, preferred_element_type=f32)
    p = b.shape[1] // 2
    low = _iota((p, b.shape[1]), 1) < p
    groups = full.shape[0] // (2 * p)
    return jnp.concatenate([jnp.where(low, full[2 * g * p:(2 * g + 1) * p], full[(2 * g + 1) * p:(2 * g + 2) * p])
                            for g in range(groups)], axis=0)


def _inv_unit_lower(n_mats):
    c = n_mats[0].shape[0]
    row, col = _iota((c, 2 * c), 0), _iota((c, 2 * c), 1) % c
    eye = (row == col).astype(f32)
    blk = (row // RWKV_INV_BLOCK) == (col // RWKV_INV_BLOCK)
    d = [jnp.where(blk, n, 0.0) for n in n_mats]
    lo = [n - x for n, x in zip(n_mats, d)]
    d2 = [_pair_mm(x, x) for x in d]
    d4 = [_pair_mm(x, x) for x in d2]
    d8 = [_pair_mm(x, x) for x in d4]
    pa = [_pair_mm(eye - x, eye + y) for x, y in zip(d, d2)]
    pb = [_pair_mm(eye + x, eye + y) for x, y in zip(d4, d8)]
    d_inv = [_pair_mm(x, y) for x, y in zip(pa, pb)]
    e = [_pair_mm(x, y) for x, y in zip(d_inv, lo)]
    e2 = [_pair_mm(x, x) for x in e]
    f = [_pair_mm(eye - x, eye + y) for x, y in zip(e, e2)]
    return [_pair_mm(x, y) for x, y in zip(f, d_inv)]


def _rwkv_chunk_coeffs(probs):
    c, width = probs[0][0].shape
    row, col = _iota((c, width), 0), _iota((c, width), 1) % (width // 2)
    strict, incl, eye = col < row, col <= row, col == row
    lhs = [jnp.concatenate([q[1], q[0]], 0).astype(bf16) for q in probs]
    with_b = [_pair_nt(x, _pair_bd(q[2].astype(bf16))) for x, q in zip(lhs, probs)]
    with_k = [_pair_nt(x, _pair_bd(q[3].astype(bf16))) for x, q in zip(lhs, probs)]
    n_ab = [jnp.where(strict, x[:c], 0.0) for x in with_b]
    p_rb = [jnp.where(incl, x[c:], 0.0) for x in with_b]
    a_ak = [jnp.where(strict, x[:c], 0.0) for x in with_k]
    p_rk = [jnp.where(incl, x[c:], 0.0) for x in with_k]
    t_inv = _inv_unit_lower(n_ab)
    av = [_pair_mm(x, q[6]) for x, q in zip(a_ak, probs)]
    two = lambda u, w: jnp.concatenate([_pair_bd(u.astype(bf16)), _pair_bd(w.astype(bf16))], axis=1)
    x = [jnp.dot(ti.astype(bf16), two(q[1], a), preferred_element_type=f32)
         for ti, q, a in zip(t_inv, probs, av)]
    px = [jnp.dot(pr.astype(bf16), two(xx[:, :width], xx[:, width:]), preferred_element_type=f32)
          for pr, xx in zip(p_rb, x)]
    prkv = [_pair_mm(pr, q[6]) for pr, q in zip(p_rk, probs)]
    xb = [_pair_tn(xx, q[4]) for xx, q in zip(x, probs)]
    vk = [_pair_tn(q[6], q[5]) for q in probs]
    out = []
    for q, pxx, pv, xbb, vkk in zip(probs, px, prkv, xb, vk):
        tm_t = jnp.where(eye, jnp.broadcast_to(q[7], (c, width)), 0.0) - xbb[:c]
        out.append((q[0] - pxx[:, :width], pv - pxx[:, width:], tm_t, vkk - xbb[c:]))
    return out


_RWKV_PARAM_NAMES = ("mu_r", "mu_k", "mu_v", "mu_wa", "mu_gr", "w0", "w_up", "a0", "a_up", "g_up",
                     "k_k", "k_a", "r_k", "ln_w", "ln_b")


def _rwkv_prompt_body(layer, n_pages, seqs_per_step, pt_ref, r_ref, k_ref, v_ref, wa_ref, gr_ref, *rest):
    np_ = len(_RWKV_PARAM_NAMES)
    p = dict(zip(_RWKV_PARAM_NAMES, rest[:np_]))
    qb_ref, qh_ref, kh_ref, vrow_ref, cache_k_hbm, cache_v_hbm = rest[np_:np_ + 6]
    y_ref, s_out_ref, shift_out_ref, y_moba_ref = rest[np_ + 6:np_ + 10]
    s_ref, prev_ref, k_buf, v_buf, sem = rest[np_ + 10:]
    t = pl.program_id(1)
    step = pl.program_id(0) * pl.num_programs(1) + t
    n_steps = pl.num_programs(0) * pl.num_programs(1)
    pages_per_step = seqs_per_step * n_pages
    rows = r_ref.shape[0]
    c_sz = RWKV_CHUNK

    def page_copies(of_step, slot):
        out = []
        for j in range(pages_per_step):
            page = pt_ref[of_step * pages_per_step + j]
            out.append(pltpu.make_async_copy(cache_k_hbm.at[layer, page], k_buf.at[slot, j], sem.at[0, slot, j]))
            out.append(pltpu.make_async_copy(cache_v_hbm.at[layer, page], v_buf.at[slot, j], sem.at[1, slot, j]))
        return out

    @pl.when(step == 0)
    def _():
        for c in page_copies(0, 0):
            c.start()

    slot = step % 2
    for c in page_copies(step, slot):
        c.wait()

    @pl.when(step + 1 < n_steps)
    def _():
        for c in page_copies(step + 1, (step + 1) % 2):
            c.start()

    @pl.when(t == 0)
    def _():
        s_ref[...] = jnp.zeros_like(s_ref)
        prev_ref[...] = jnp.zeros_like(prev_ref)

    def shifted(x_ref, mu_ref, c0):
        x = x_ref[...]
        width = x.shape[1]
        prev = jnp.where(_iota(x.shape, 0) == 0, prev_ref[0:1, c0:c0 + width], pltpu.roll(x, 1, 0))
        prev_ref[0:1, c0:c0 + width] = x[rows - 1:rows, :]
        return x + mu_ref[...] * (prev - x)

    pre = _rwkv_pre(shifted(r_ref, p["mu_r"], 0), shifted(k_ref, p["mu_k"], W), shifted(v_ref, p["mu_v"], 2 * W),
                    shifted(wa_ref, p["mu_wa"], 3 * W), shifted(gr_ref, p["mu_gr"], 3 * W + LANES), p)
    tril_b = (_iota((c_sz, c_sz), 1) <= _iota((c_sz, c_sz), 0)).astype(bf16)
    n_chunks = rows // c_sz
    n_pairs = RWKV_HEADS // 2
    b_all = pre["kk"] * pre["a"]

    probs = []
    for c in range(n_chunks):
        rs = slice(c * c_sz, (c + 1) * c_sz)
        lw = pre["logw"][rs]
        g_cum = _dot_exact_lhs(tril_b, lw)
        g_last = g_cum[c_sz - 1:c_sz, :]
        e_ng = jnp.exp(-g_cum)
        e_gl = jnp.exp(g_last - g_cum)
        kk, b, k, v = pre["kk"][rs], b_all[rs], pre["k"][rs], pre["v"][rs]
        r_t = pre["r"][rs] * jnp.exp(g_cum)
        a_t = kk * jnp.exp(g_cum - lw)
        b_t, k_t, b_g, k_g = b * e_ng, k * e_ng, b * e_gl, k * e_gl
        gam = jnp.exp(g_last)
        for hp in range(n_pairs):
            hs = slice(hp * LANES, (hp + 1) * LANES)
            probs.append((r_t[:, hs], a_t[:, hs], b_t[:, hs], k_t[:, hs], b_g[:, hs], k_g[:, hs], v[:, hs],
                          gam[:, hs]))
    coeffs = _rwkv_chunk_coeffs(probs)

    state = [s_ref[hp] for hp in range(n_pairs)]
    y_rows = []
    for c in range(n_chunks):
        cf = coeffs[c * n_pairs:(c + 1) * n_pairs]
        bd_s = [_pair_bd(s.astype(bf16)) for s in state]
        y_rows.append(jnp.concatenate([_pair_nt(q[0], bd) + q[1] for q, bd in zip(cf, bd_s)], axis=1))
        state = [_pair_mm(s, q[2]) + q[3] for q, s in zip(cf, state)]
    for hp in range(n_pairs):
        s_ref[hp] = state[hp]
    yw = jnp.concatenate(y_rows, axis=0)
    y_ref[...] = _rwkv_post(yw, pre["bonus"], pre["g7"], p).astype(y_ref.dtype)
    s_out_ref[...] = s_ref[...]
    shift_out_ref[...] = prev_ref[...]

    for u in range(seqs_per_step):
        y_moba_ref[u] = _moba_sample_seq(
            qb_ref[u], qh_ref[u], kh_ref[u], vrow_ref[u],
            lambda g, u=u: k_buf[slot, u * n_pages + g], lambda g, u=u: v_buf[slot, u * n_pages + g], n_pages)


def _rwkv_param_specs(const_map):
    vec = lambda width: pl.BlockSpec((1, width), const_map)
    mat = lambda r: pl.BlockSpec((r, W), const_map)
    return [vec(W), vec(W), vec(W), vec(LANES), vec(LANES), vec(W), mat(LANES), vec(W), mat(LANES),
            mat(RWKV_GATE_RANK), vec(W), vec(W), vec(W), vec(W), vec(W)]


def rwkv_prompt_moba_sample(proj, params, n_batch, seq, rt, layer, page_table_flat, q_lanes, q_heads, k_heads,
                            v_row, cache_kt, cache_vt, n_pages):
    nt = seq // rt
    n_seq = q_heads.shape[0]
    n_steps = n_batch * nt
    assert n_seq % n_steps == 0, "the sample sequences are spread evenly over the prompt row tiles"
    sps = n_seq // n_steps
    nh, dh = MOBA_HEADS, MOBA_HEAD_DIM
    rowmap = lambda width, col: pl.BlockSpec((rt, width), lambda b, t, pt: (b * nt + t, col // width))
    const2 = lambda b, t, pt: (0, 0)
    per_step = lambda *tail: pl.BlockSpec((sps,) + tail, lambda b, t, pt: (b * nt + t,) + (0,) * len(tail))
    grid_spec = pltpu.PrefetchScalarGridSpec(
        num_scalar_prefetch=1,
        grid=(n_batch, nt),
        in_specs=[rowmap(W, C_R), rowmap(W, C_K7), rowmap(W, C_V7), rowmap(LANES, C_WA), rowmap(LANES, C_GR)]
        + _rwkv_param_specs(const2)
        + [per_step(nh * dh, PAGE_SIZE), per_step(nh, dh), per_step(nh, dh), per_step(1, W),
           pl.BlockSpec(memory_space=pl.ANY), pl.BlockSpec(memory_space=pl.ANY)],
        out_specs=[pl.BlockSpec((rt, W), lambda b, t, pt: (b * nt + t, 0)),
                   pl.BlockSpec((None, RWKV_HEADS // 2, RWKV_HEAD, 2 * RWKV_HEAD), lambda b, t, pt: (b, 0, 0, 0)),
                   pl.BlockSpec((None, SUBLANES, RWKV_PROJ), lambda b, t, pt: (b, 0, 0)),
                   per_step(1, W)],
        scratch_shapes=[pltpu.VMEM((RWKV_HEADS // 2, RWKV_HEAD, 2 * RWKV_HEAD), f32),
                        pltpu.VMEM((SUBLANES, RWKV_PROJ), f32),
                        pltpu.VMEM((2, sps * n_pages, nh, dh, PAGE_SIZE), f32),
                        pltpu.VMEM((2, sps * n_pages, nh, dh, PAGE_SIZE), f32),
                        pltpu.SemaphoreType.DMA((2, 2, sps * n_pages))],
    )
    return pl.pallas_call(
        functools.partial(_rwkv_prompt_body, layer, n_pages, sps),
        grid_spec=grid_spec,
        out_shape=[jax.ShapeDtypeStruct((n_batch * seq, W), bf16),
                   jax.ShapeDtypeStruct((n_batch, RWKV_HEADS // 2, RWKV_HEAD, 2 * RWKV_HEAD), f32),
                   jax.ShapeDtypeStruct((n_batch, SUBLANES, RWKV_PROJ), f32),
                   jax.ShapeDtypeStruct((n_seq, 1, W), f32)],
        compiler_params=_params(("arbitrary", "arbitrary")),
        name="rwkv_prompt_moba_sample",
    )(page_table_flat, proj, proj, proj, proj, proj, *[params[n] for n in _RWKV_PARAM_NAMES],
      q_lanes, q_heads, k_heads, v_row, cache_kt, cache_vt)


def _rope(x, cos, sin_signed):
    lane = _iota(x.shape, 1)
    half = MOBA_HEAD_DIM // 2
    width = x.shape[1]
    swapped = jnp.where(lane % MOBA_HEAD_DIM < half, pltpu.roll(x, width - half, 1), pltpu.roll(x, half, 1))
    return x * cos + swapped * sin_signed


def _sample_body(gq_ref, gk_ref, gv_ref, go_ref, glow_ref, r_ref, k_ref, v_ref, wa_ref, gr_ref, mq_ref, mk_ref,
                 sg_ref, sw_ref, shift_ref, cos_ref, sin_ref, up_ref, bias_ref, nw_ref, *rest):
    np_ = len(_RWKV_PARAM_NAMES)
    p = dict(zip(_RWKV_PARAM_NAMES, rest[:np_]))
    yg_ref, yr_ref, qs_ref, ks_ref, sg_out, sw_out = rest[np_:np_ + 6]
    qt_s, kt_s, egt_s, og_s, rt_s, wt_s, k7t_s, vt_s, kkt_s, bt_s, yt_s, bonus_s, g7_s = rest[np_ + 6:]
    hp = pl.program_id(0)
    n_seq = gq_ref.shape[0]

    @pl.when(hp == 0)
    def _():
        qs_ref[...] = _rope(mq_ref[...], cos_ref[...], sin_ref[...]) * (MOBA_HEAD_DIM ** -0.5)
        ks_ref[...] = _rope(mk_ref[...], cos_ref[...], sin_ref[...])
        lg = _gla_log_gate(glow_ref[...], up_ref, bias_ref)
        qt_s[...] = (gq_ref[...] * (GLA_DK ** -0.5)).T
        kt_s[...] = gk_ref[...].T
        egt_s[...] = jnp.exp(lg).T

        def shifted(x_ref, mu_ref, c0):
            x = x_ref[...]
            return x + mu_ref[...] * (shift_ref[:, c0:c0 + x.shape[1]] - x)

        pre = _rwkv_pre(shifted(r_ref, p["mu_r"], 0), shifted(k_ref, p["mu_k"], W), shifted(v_ref, p["mu_v"], 2 * W),
                        shifted(wa_ref, p["mu_wa"], 3 * W), shifted(gr_ref, p["mu_gr"], 3 * W + LANES), p)
        rt_s[...] = pre["r"].T
        wt_s[...] = jnp.exp(pre["logw"]).T
        k7t_s[...] = pre["k"].T
        vt_s[...] = pre["v"].T
        kkt_s[...] = pre["kk"].T
        bt_s[...] = (pre["kk"] * pre["a"]).T
        bonus_s[...] = pre["bonus"]
        g7_s[...] = pre["g7"]
        og_s[...] = jnp.zeros_like(og_s)

    ks = pl.ds(pl.multiple_of(hp * GLA_DK, GLA_DK), GLA_DK)
    eg_t, k_t, q_t = egt_s[ks, :], kt_s[ks, :], qt_s[ks, :]
    gv_h = gv_ref[:, pl.ds(pl.multiple_of(hp * GLA_DV, GLA_DV), GLA_DV)]
    lane_grp = _iota((n_seq, W), 1) // GLA_DV
    o_rows = []
    for s in range(n_seq):
        col = lambda x: jnp.broadcast_to(x[:, s:s + 1], (GLA_DK, GLA_DV))
        st = sg_ref[s] * col(eg_t) + col(k_t) * gv_h[s:s + 1, :]
        sg_out[s] = st
        o_rows.append(jnp.sum(col(q_t) * st, axis=0, keepdims=True))
    o_h = jnp.concatenate(o_rows, axis=0)
    og_s[...] = jnp.where(lane_grp == hp, jnp.concatenate([o_h] * GLA_HEADS, axis=1), og_s[...])

    group = RWKV_HEAD // SUBLANES
    for u in range(2):
        hs = pl.ds(pl.multiple_of((2 * hp + u) * RWKV_HEAD, RWKV_HEAD), RWKV_HEAD)
        neg_kk, w_t, b_t, k_t7, r_t = -kkt_s[hs, :], wt_s[hs, :], bt_s[hs, :], k7t_s[hs, :], rt_s[hs, :]

        def v_group(g, carry, u=u, hs=hs, neg_kk=neg_kk, w_t=w_t, b_t=b_t, k_t7=k_t7, r_t=r_t):
            base = pl.multiple_of((2 * hp + u) * RWKV_HEAD + g * SUBLANES, SUBLANES)
            v_rows = vt_s[pl.ds(base, SUBLANES), :]
            y_rows = []
            for j in range(SUBLANES):
                vi = g * SUBLANES + j
                st = sw_ref[u, vi]
                sa = jnp.sum(st * neg_kk, axis=0, keepdims=True)
                st = st * w_t + sa * b_t + v_rows[j:j + 1, :] * k_t7
                sw_out[u, vi] = st
                y_rows.append(jnp.sum(st * r_t, axis=0, keepdims=True))
            yt_s[pl.ds(base, SUBLANES), :] = jnp.concatenate(y_rows, axis=0)
            return carry

        lax.fori_loop(0, group, v_group, 0)

    @pl.when(hp == pl.num_programs(0) - 1)
    def _():
        og = og_s[...]
        for h in range(GLA_HEADS):
            vs = slice(h * GLA_DV, (h + 1) * GLA_DV)
            yg_ref[:, vs] = _gla_out(og[:, vs], nw_ref[...], go_ref[:, vs]).astype(yg_ref.dtype)
        yr_ref[...] = _rwkv_post(yt_s[...].T, bonus_s[...], g7_s[...], p).astype(yr_ref.dtype)


def sample_step(proj, row0, n_seq, layer, state_gla, state_wkv_t, state_shift, cos_s, sin_s,
                up_pad, bias, norm_w, params):
    r0 = row0 // n_seq
    rowmap = lambda width, col: pl.BlockSpec((n_seq, width), lambda h: (r0, col // width))
    const2 = lambda h: (0, 0)
    hk = GLA_HEADS * GLA_DK
    out_row = pl.BlockSpec((n_seq, W), const2)
    tr = lambda rows: pltpu.VMEM((rows, n_seq), f32)
    rw = lambda: pltpu.VMEM((n_seq, W), f32)
    return pl.pallas_call(
        _sample_body,
        grid=(GLA_HEADS,),
        in_specs=[rowmap(hk, C_GQ), rowmap(hk, C_GK), rowmap(W, C_GV), rowmap(W, C_GO), rowmap(LANES, C_GLOW),
                  rowmap(W, C_R), rowmap(W, C_K7), rowmap(W, C_V7), rowmap(LANES, C_WA), rowmap(LANES, C_GR),
                  rowmap(W, C_MQ), rowmap(W, C_MK),
                  pl.BlockSpec((None, n_seq, None, GLA_DK, GLA_DV), lambda h: (layer, 0, h, 0, 0)),
                  pl.BlockSpec((None, 2, RWKV_HEAD, RWKV_HEAD, n_seq), lambda h: (layer, h, 0, 0, 0)),
                  pl.BlockSpec((None, n_seq, RWKV_PROJ), lambda h: (layer, 0, 0)),
                  pl.BlockSpec((1, W), const2), pl.BlockSpec((1, W), const2),
                  pl.BlockSpec((LANES, hk), const2), pl.BlockSpec((1, hk), const2), pl.BlockSpec((1, GLA_DV), const2)]
        + _rwkv_param_specs(const2),
        out_specs=[out_row, out_row, out_row, out_row,
                   pl.BlockSpec((n_seq, None, GLA_DK, GLA_DV), lambda h: (0, h, 0, 0)),
                   pl.BlockSpec((2, RWKV_HEAD, RWKV_HEAD, n_seq), lambda h: (h, 0, 0, 0))],
        out_shape=[jax.ShapeDtypeStruct((n_seq, W), bf16), jax.ShapeDtypeStruct((n_seq, W), bf16),
                   jax.ShapeDtypeStruct((n_seq, W), f32), jax.ShapeDtypeStruct((n_seq, W), f32),
                   jax.ShapeDtypeStruct((n_seq, GLA_HEADS, GLA_DK, GLA_DV), f32),
                   jax.ShapeDtypeStruct((RWKV_HEADS, RWKV_HEAD, RWKV_HEAD, n_seq), f32)],
        scratch_shapes=[tr(hk), tr(hk), tr(hk), rw(), tr(W), tr(W), tr(W), tr(W), tr(W), tr(W), tr(W), rw(), rw()],
        compiler_params=_params(("arbitrary",)),
        name="sample_step",
    )(proj, proj, proj, proj, proj, proj, proj, proj, proj, proj, proj, proj,
      state_gla, state_wkv_t, state_shift, cos_s, sin_s, up_pad, bias, norm_w,
      *[params[n] for n in _RWKV_PARAM_NAMES])


def _topk_select(gate, n_valid, axis_len, axis):
    idx = _iota(gate.shape, axis)
    cnt = jnp.zeros(gate.shape, f32)
    for m in range(axis_len):
        g_m = lax.slice_in_dim(gate, m, m + 1, axis=axis)
        g_m = jnp.broadcast_to(g_m, gate.shape)
        beats = (g_m > gate) | ((g_m == gate) & (m < idx))
        cnt = cnt + jnp.where(beats, 1.0, 0.0) * jnp.where(m < n_valid, 1.0, 0.0)
    return (idx < n_valid) & (cnt < MOBA_TOPK)


def _rope_prep_body(mq_ref, mk_ref, mv_ref, cos_ref, sin_ref, k_stack_in, v_stack_in,
                    k_out, v_out, qb_out, kb_out, vb_out, sel_out, km_ref):
    del k_stack_in, v_stack_in
    i = pl.program_id(1)
    blk = mq_ref.shape[0]
    nb = km_ref.shape[0]
    nh = MOBA_HEADS

    @pl.when(i == 0)
    def _():
        km_ref[...] = jnp.zeros_like(km_ref)

    q = _rope(mq_ref[...], cos_ref[...], sin_ref[...]) * (MOBA_HEAD_DIM ** -0.5)
    k = _rope(mk_ref[...], cos_ref[...], sin_ref[...])
    k_out[...] = k.T.reshape(nh, MOBA_HEAD_DIM, blk)
    v_out[...] = mv_ref[...].T.reshape(nh, MOBA_HEAD_DIM, blk)
    qb_out[...] = (q * LOG2_E).astype(bf16)
    kb_out[...] = k.astype(bf16)
    vb_out[...] = mv_ref[...].astype(bf16)

    km = km_ref[...]
    km_rows = jnp.concatenate([km] * nh, axis=0)
    head_of_row = _iota(km_rows.shape, 0) // nb
    km_rows = jnp.where(head_of_row == _iota(km_rows.shape, 1) // MOBA_HEAD_DIM, km_rows, 0.0)
    gate = _dot3_nt(km_rows, q).reshape(nh, nb, blk)
    sel = _topk_select(gate, i, nb, 1).astype(bf16).reshape(nh * nb, blk)
    sel = jnp.concatenate([sel, jnp.zeros((LANES - nh * nb, blk), bf16)], axis=0)
    eye = (_iota((blk, blk), 0) == _iota((blk, blk), 1)).astype(bf16)
    sel_out[...] = (lax.dot_general(eye, sel, NT_DIMS, preferred_element_type=f32) - 1.0) * MASK_BIG

    _set_row(km_ref, i, jnp.sum(k, axis=0, keepdims=True) * (1.0 / blk))


def rope_prep(proj, cos, sin, n_batch, seq, layer, k_stack, v_stack):
    blk = MOBA_BLOCK
    nb = seq // blk
    m = n_batch * seq
    rowmap = lambda col: pl.BlockSpec((blk, W), lambda b, i: (b * nb + i, col // W))
    tab = pl.BlockSpec((blk, W), lambda b, i: (i, 0))
    out = lambda width: pl.BlockSpec((blk, width), lambda b, i: (b * nb + i, 0))
    slab = pl.BlockSpec((None, None, MOBA_HEADS, MOBA_HEAD_DIM, blk), lambda b, i: (layer, b, 0, 0, i))
    anywhere = pl.BlockSpec(memory_space=pl.ANY)
    return pl.pallas_call(
        _rope_prep_body,
        grid=(n_batch, nb),
        in_specs=[rowmap(C_MQ), rowmap(C_MK), rowmap(C_MV), tab, tab, anywhere, anywhere],
        out_specs=[slab, slab, out(W), out(W), out(W), out(LANES)],
        out_shape=[jax.ShapeDtypeStruct(k_stack.shape, f32), jax.ShapeDtypeStruct(v_stack.shape, f32),
                   jax.ShapeDtypeStruct((m, W), bf16), jax.ShapeDtypeStruct((m, W), bf16),
                   jax.ShapeDtypeStruct((m, W), bf16), jax.ShapeDtypeStruct((m, LANES), f32)],
        scratch_shapes=[pltpu.VMEM((nb, W), f32)],
        input_output_aliases={5: 0, 6: 1},
        compiler_params=_params(("arbitrary", "arbitrary")),
        name="rope_prep",
    )(proj, proj, proj, cos, sin, k_stack, v_stack)


def _moba_prompt_body(qb_ref, kb_ref, vb_ref, sel_ref, y_ref, m_ref, acc_ref):
    i = pl.program_id(1)
    blk = qb_ref.shape[0]
    nb = kb_ref.shape[0] // blk
    lane = _iota((blk, LANES), 1)
    low_half = lane < MOBA_HEAD_DIM
    causal = _iota((blk, blk), 1) <= _iota((blk, blk), 0)
    neg_inf = -jnp.inf

    def head_q(pair, u):
        qp = qb_ref[:, pair * LANES:(pair + 1) * LANES]
        return jnp.where(low_half if u == 0 else ~low_half, qp, jnp.zeros_like(qp))

    def kv(n, pair):
        rs = pl.ds(pl.multiple_of(n * blk, blk), blk)
        ls = slice(pair * LANES, (pair + 1) * LANES)
        return kb_ref[rs, ls], vb_ref[rs, ls]

    wide = lambda x: jnp.concatenate([x, x], axis=1)

    def head_v(v_n, u):
        return jnp.where(low_half if u == 0 else ~low_half, v_n, jnp.ones_like(v_n))

    all_heads = [(pair, u) for pair in range(MOBA_HEADS // 2) for u in range(2)]
    kv_own = [kv(i, pair) for pair in range(MOBA_HEADS // 2)]
    s_own = [lax.dot_general(head_q(pair, u), kv_own[pair][0], NT_DIMS, preferred_element_type=f32)
             for pair, u in all_heads]
    s_own = [jnp.where(causal, x, neg_inf) for x in s_own]
    m_own = [jnp.broadcast_to(jnp.max(x, axis=1, keepdims=True), (blk, LANES)) for x in s_own]
    p_own = [jnp.exp2(x - wide(m)).astype(bf16) for x, m in zip(s_own, m_own)]
    for h, (pair, u) in enumerate(all_heads):
        m_ref[h] = m_own[h]
        acc_ref[h] = jnp.dot(p_own[h], head_v(kv_own[pair][1], u), preferred_element_type=f32)

    mask_b = sel_ref[...].astype(bf16)
    n_heads = MOBA_HEADS
    mask_lane = lambda u: MOBA_HEAD_DIM if u == 0 else 0
    hot_row = _iota((LANES, n_heads * LANES), 0)
    hot_col = _iota((LANES, n_heads * LANES), 1)
    hot_head = hot_col // LANES
    hot_lane_ok = hot_col % LANES == jnp.where(hot_head % 2 == 0, mask_lane(0), mask_lane(1))

    def past(n, carry):
        heads = [(pair, u) for pair in range(n_heads // 2) for u in range(2)]
        kvs = [kv(n, pair) for pair in range(n_heads // 2)]
        one_hot = (hot_lane_ok & (hot_row == hot_head * nb + n)).astype(bf16)
        bias = jnp.dot(mask_b, one_hot, preferred_element_type=f32).astype(bf16)
        q_m = [head_q(pair, u) + bias[:, (2 * pair + u) * LANES:(2 * pair + u + 1) * LANES] for pair, u in heads]
        k_e = [jnp.where(lane == mask_lane(u), jnp.ones_like(kvs[pair][0]), kvs[pair][0]) for pair, u in heads]
        s = [lax.dot_general(q, k, NT_DIMS, preferred_element_type=f32) for q, k in zip(q_m, k_e)]
        m_old = [m_ref[h] for h in range(n_heads)]
        m_new = [jnp.maximum(mo, jnp.broadcast_to(jnp.max(x, axis=1, keepdims=True), (blk, LANES)))
                 for mo, x in zip(m_old, s)]
        p = [jnp.exp2(x - wide(mn)).astype(bf16) for x, mn in zip(s, m_new)]
        pv = [jnp.dot(pp, head_v(kvs[pair][1], u), preferred_element_type=f32) for pp, (pair, u) in zip(p, heads)]
        for h in range(n_heads):
            m_ref[h] = m_new[h]
            acc_ref[h] = jnp.exp2(m_old[h] - m_new[h]) * acc_ref[h] + pv[h]
        return carry

    lax.fori_loop(0, i, past, 0)

    for pair in range(MOBA_HEADS // 2):
        halves = []
        for u in range(2):
            acc = acc_ref[2 * pair + u]
            halves.append(acc / pltpu.roll(acc, MOBA_HEAD_DIM, 1))
        y_ref[:, pair * LANES:(pair + 1) * LANES] = jnp.where(low_half, halves[0], halves[1]).astype(y_ref.dtype)


def moba_prompt(qb, kb, vb, sel, n_batch, seq):
    blk = MOBA_BLOCK
    nb = seq // blk
    row = lambda width: pl.BlockSpec((blk, width), lambda b, i: (b * nb + i, 0))
    full = pl.BlockSpec((seq, W), lambda b, i: (b, 0))
    scr = lambda: pltpu.VMEM((MOBA_HEADS, blk, LANES), f32)
    return pl.pallas_call(
        _moba_prompt_body,
        grid=(n_batch, nb),
        in_specs=[row(W), full, full, row(LANES)],
        out_specs=row(W),
        out_shape=jax.ShapeDtypeStruct((n_batch * seq, W), bf16),
        scratch_shapes=[scr(), scr()],
        compiler_params=_params(("arbitrary", "arbitrary")),
        name="moba_prompt",
    )(qb, kb, vb, sel)


def _moba_sample_seq(q_b, q_heads, k_heads, v_row, k_page, v_page, n_pages):
    nh, dh = MOBA_HEADS, MOBA_HEAD_DIM
    pages_per_block = MOBA_BLOCK // PAGE_SIZE
    nb = n_pages // pages_per_block
    lane = _iota((nh, LANES), 1)

    def head_sum(x):
        return jnp.concatenate([jnp.sum(x[h * dh:(h + 1) * dh], axis=0, keepdims=True) for h in range(nh)], axis=0)

    def head_rows(x):
        return jnp.concatenate([jnp.broadcast_to(x[h:h + 1], (dh, x.shape[1])) for h in range(nh)], axis=0)

    s_pages = [head_sum(k_page(g).reshape(nh * dh, PAGE_SIZE) * q_b) for g in range(n_pages)]
    gate = jnp.zeros((nh, LANES), f32)
    for n in range(nb):
        blk_sum = sum(jnp.sum(s_pages[n * pages_per_block + t], axis=1, keepdims=True)
                      for t in range(pages_per_block))
        gate = jnp.where(lane == n, blk_sum * (1.0 / MOBA_BLOCK), gate)
    sel = _topk_select(gate, nb, nb, 1).astype(f32)
    s_own = jnp.sum(q_heads * k_heads, axis=1, keepdims=True)
    s_pages = [jnp.where(jnp.broadcast_to(sel[:, g // pages_per_block:g // pages_per_block + 1],
                                          (nh, PAGE_SIZE)) > 0.5, s, -jnp.inf)
               for g, s in enumerate(s_pages)]
    m = s_own
    for s in s_pages:
        m = jnp.maximum(m, jnp.max(s, axis=1, keepdims=True))
    p_pages = [jnp.exp(s - m) for s in s_pages]
    w_own = jnp.exp(s_own - m)
    l_tot = w_own + sum(jnp.sum(p, axis=1, keepdims=True) for p in p_pages)

    acc = jnp.zeros((nh * dh, PAGE_SIZE), f32)
    for g in range(n_pages):
        acc = acc + head_rows(p_pages[g]) * v_page(g).reshape(nh * dh, PAGE_SIZE)
    hi = acc.astype(bf16)
    mid = (acc - hi.astype(f32))
    lo = (mid - mid.astype(bf16).astype(f32)).astype(bf16)
    ones = jnp.ones((nh, PAGE_SIZE), bf16)
    nt = lambda a: lax.dot_general(ones, a, NT_DIMS, preferred_element_type=f32)
    out = nt(hi) + nt(mid.astype(bf16)) + nt(lo)
    out = (out + w_own * v_row) / l_tot
    head_lane = _iota((nh, nh * dh), 1) // dh == _iota((nh, nh * dh), 0)
    return jnp.sum(jnp.where(head_lane, out, 0.0), axis=0, keepdims=True)


def _rope_tables(pos):
    half = MOBA_HEAD_DIM // 2
    inv = ROPE_THETA ** (-jnp.arange(half, dtype=f32) / half)
    ang = pos.astype(f32)[:, None] * inv[None, :]
    cos, sin = jnp.cos(ang), jnp.sin(ang)
    cos_h = jnp.concatenate([cos, cos], -1)
    sin_h = jnp.concatenate([-sin, sin], -1)
    return jnp.tile(cos_h, (1, MOBA_HEADS)), jnp.tile(sin_h, (1, MOBA_HEADS))


def _pack_w_in(w):
    cols = lambda o, n: w[:, o:o + n]
    hk = GLA_HEADS * GLA_DK
    parts = [cols(_O_GQ, hk), cols(_O_GK, hk), cols(_O_GV, W), cols(_O_GO, W),
             cols(_O_MQ, W), cols(_O_MK, W), cols(_O_MV, W), cols(_O_PG, N_BRANCH * D_MODEL),
             cols(_O_PR, RWKV_PROJ), cols(_O_GLOW, GLA_GATE_RANK),
             jnp.zeros((w.shape[0], N_PACK - C_GLOW - GLA_GATE_RANK), w.dtype)]
    return jnp.concatenate(parts, axis=1).astype(bf16)


def _pad_rows(m, rows):
    return jnp.concatenate([m, jnp.zeros((rows - m.shape[0], m.shape[1]), m.dtype)], axis=0)


SUBLANES = 8
IN_PROJ_ROWS_TARGET, IN_PROJ_TN = 2048, 1024
IN_PROJ_ROW_CHUNK = 256
TOKEN_ROWS_TARGET = 1024
MERGE_ROWS_TARGET = 512
MIXER_ROWS = 256


def _row_tile(m, target):
    best = SUBLANES
    for t in range(SUBLANES, target + 1, SUBLANES):
        if m % t == 0:
            best = t
    assert m % best == 0
    return best


def kernel(x_prompt, x_sample, cache_k, cache_v, page_table, state_gla, state_wkv, state_shift, w_in, b_gate, gla_gk_up, gla_gk_bias, gla_norm_w, rwkv_mu, rwkv_w0, rwkv_w_up, rwkv_a0, rwkv_a_up, rwkv_g_up, rwkv_k_k, rwkv_k_a, rwkv_r_k, rwkv_ln_w, rwkv_ln_b, w_branch, w_out, ln1_g, ln1_b, w_up, w_down, ln2_g, ln2_b):
    n_batch, seq, _ = x_prompt.shape
    n_seq = x_sample.shape[0]
    n_pages = page_table.shape[1]
    mp = n_batch * seq
    past = n_pages * PAGE_SIZE

    cos_p, sin_p = _rope_tables(jnp.arange(seq, dtype=jnp.int32))
    cos_s, sin_s = _rope_tables(jnp.full((1,), past, jnp.int32))
    pt_flat = page_table.reshape(-1)
    cache_kt = jnp.transpose(cache_k, (0, 1, 3, 4, 2))
    cache_vt = jnp.transpose(cache_v, (0, 1, 3, 4, 2))
    state_wkv_t = jnp.transpose(state_wkv, (0, 2, 3, 4, 1))

    x_p, x_s = x_prompt.reshape(mp, D_MODEL), x_sample.reshape(n_seq, D_MODEL)
    outs = {k: [] for k in ("k_s", "v_s", "gla_p", "gla_s", "wkv_p", "wkv_s", "shift_p", "shift_s")}
    k_stack = jnp.zeros((DEPTH, n_batch, MOBA_HEADS, MOBA_HEAD_DIM, seq), f32)
    v_stack = jnp.zeros((DEPTH, n_batch, MOBA_HEADS, MOBA_HEAD_DIM, seq), f32)
    for l in range(DEPTH):
        row2 = lambda v: v.reshape(1, -1)
        mu = rwkv_mu[l]
        rp = dict(
            mu_r=row2(mu[:W]), mu_k=row2(mu[W:2 * W]), mu_v=row2(mu[2 * W:3 * W]),
            mu_wa=row2(mu[3 * W:3 * W + LANES]), mu_gr=row2(mu[3 * W + LANES:]),
            w0=row2(rwkv_w0[l]), w_up=_pad_rows(rwkv_w_up[l], LANES).astype(bf16),
            a0=row2(rwkv_a0[l]),
            a_up=jnp.concatenate([jnp.zeros((RWKV_DECAY_RANK, W), f32), rwkv_a_up[l]], axis=0).astype(bf16),
            g_up=rwkv_g_up[l].astype(bf16), k_k=row2(rwkv_k_k[l]), k_a=row2(rwkv_k_a[l]),
            r_k=row2(rwkv_r_k[l]), ln_w=row2(rwkv_ln_w[l]), ln_b=row2(rwkv_ln_b[l]))
        up_pad = _pad_rows(gla_gk_up[l], LANES).astype(bf16)
        gk_bias = row2(gla_gk_bias[l])
        norm_w = row2(gla_norm_w[l])

        w_packed = _pack_w_in(w_in[l])
        wb, wo = w_branch[l].astype(bf16), w_out[l].astype(bf16)
        wu, wd = w_up[l].astype(bf16), w_down[l].astype(bf16)
        ln1, ln2 = (row2(ln1_g[l]), row2(ln1_b[l])), (row2(ln2_g[l]), row2(ln2_b[l]))

        def token_tail(x_rows, proj_rows, y_gla, y_moba, y_rwkv):
            tm = _row_tile(x_rows.shape[0], TOKEN_ROWS_TARGET)
            x1 = merge_ln(x_rows, proj_rows, y_gla, y_moba, y_rwkv, wb, b_gate[l], wo, *ln1,
                          _row_tile(x_rows.shape[0], MERGE_ROWS_TARGET))
            return mlp_ln(x1, wu, wd, *ln2, tm)

        proj_p = in_proj(x_p, w_packed, _row_tile(mp, IN_PROJ_ROWS_TARGET), IN_PROJ_TN)
        proj_s = in_proj(x_s, w_packed, n_seq, IN_PROJ_TN)

        y_gla_s, y_rwkv_s, q_s, k_s, gla_s, wkv_s_t = sample_step(
            proj_s, 0, n_seq, l, state_gla, state_wkv_t, state_shift, cos_s, sin_s, up_pad, gk_bias, norm_w, rp)
        v_s = proj_s[:, C_MV:C_MV + W]
        heads = (n_seq, MOBA_HEADS, MOBA_HEAD_DIM)
        q_lanes = jnp.broadcast_to(q_s[:, :, None], (n_seq, W, PAGE_SIZE))

        y_gla_p, gla_p = gla_prompt(proj_p, up_pad, gk_bias, norm_w, n_batch, seq, MIXER_ROWS)
        y_rwkv_p, wkv_p, shift_p, y_moba_s = rwkv_prompt_moba_sample(
            proj_p, rp, n_batch, seq, MIXER_ROWS, l, pt_flat, q_lanes, q_s.reshape(heads), k_s.reshape(heads),
            v_s.reshape(n_seq, 1, W), cache_kt, cache_vt, n_pages)
        k_stack, v_stack, qb, kb, vb, sel = rope_prep(proj_p, cos_p, sin_p, n_batch, seq, l, k_stack, v_stack)
        y_moba_p = moba_prompt(qb, kb, vb, sel, n_batch, seq)
        x_p_next = token_tail(x_p, proj_p, y_gla_p, y_moba_p, y_rwkv_p)
        x_s_next = token_tail(x_s, proj_s, y_gla_s, y_moba_s.reshape(n_seq, W).astype(bf16), y_rwkv_s)

        x_p, x_s = x_p_next, x_s_next
        hd = (MOBA_HEADS, MOBA_HEAD_DIM)
        outs["k_s"].append(k_s.reshape(n_seq, 1, *hd))
        outs["v_s"].append(v_s.reshape(n_seq, 1, *hd))
        outs["gla_p"].append(gla_p)
        outs["gla_s"].append(gla_s)
        wkv_p = wkv_p.reshape(n_batch, RWKV_HEADS // 2, RWKV_HEAD, 2, RWKV_HEAD)
        outs["wkv_p"].append(wkv_p.transpose(0, 1, 3, 2, 4).reshape(n_batch, RWKV_HEADS, RWKV_HEAD, RWKV_HEAD))
        outs["wkv_s"].append(jnp.transpose(wkv_s_t, (3, 0, 1, 2)))
        outs["shift_p"].append(shift_p[:, 0, :])
        outs["shift_s"].append(proj_s[:, C_PR:C_PR + RWKV_PROJ])

    st = lambda k: jnp.stack(outs[k])
    kv_order = (0, 1, 4, 2, 3)
    return (x_p.reshape(n_batch, seq, D_MODEL), x_s.reshape(n_seq, 1, D_MODEL),
            jnp.transpose(k_stack, kv_order), jnp.transpose(v_stack, kv_order), st("k_s"), st("v_s"), st("gla_p"), st("gla_s"),
            st("wkv_p"), st("wkv_s"), st("shift_p"), st("shift_s"))
```
